```python
import math
import numpy as np
import jax
import jax.numpy as jnp
from jax import lax

D_MODEL = 1024
BATCH = 8
SEQ = 4096
DEPTH = 2
DEC_BATCH = 32
DEC_SEQ = 1
PAST_LEN = 16384
PAGE_SIZE = 128

DN_ALPHA = (2.0 * DEPTH) ** 0.25
DN_BETA = (8.0 * DEPTH) ** -0.25
LN_EPS = 1e-5
D_FF = 2816
FFN_RES = 0.5
S5_WIDTH = 512
S5_GROUP = 16
S5_GROUPS = S5_WIDTH // S5_GROUP
S5_STATE = 64
S5_DT_MIN = 1e-3
S5_DT_MAX = 1e-1
NSA_HEADS = 8
NSA_KV_HEADS = 2
NSA_HEAD_DIM = 64
NSA_GQ = NSA_HEADS // NSA_KV_HEADS
CMP_LEN = 32
CMP_STRIDE = 16
CMP_HIDDEN = 128
SEL_LEN = 64
SEL_TOP = 16
WINDOW = 512
NSA_QBLOCK = 64
ROPE_THETA = 500000.0
ROT_DIM = NSA_HEAD_DIM // 4
FORCE_SCORE = 1e9
CONV_CH = 512
CONV_K = 3
N_BRANCH = 3
KV_WIDTH = NSA_KV_HEADS * NSA_HEAD_DIM
IN_WIDTHS = (S5_WIDTH, NSA_HEADS * NSA_HEAD_DIM, 6 * KV_WIDTH, 3 * NSA_HEADS, 3 * CONV_CH, N_BRANCH * D_MODEL)
IN_OFFSETS = tuple(int(o) for o in np.cumsum(IN_WIDTHS)[:-1])
D_IN = int(sum(IN_WIDTHS))

kernel_name = 'hybrid_s5_nsa_shortconv_decoder_step'


def layer_norm(x, g, b):
    xf = x.astype(jnp.float32)
    mu = xf.mean(-1, keepdims=True)
    var = jnp.square(xf - mu).mean(-1, keepdims=True)
    return ((xf - mu) * lax.rsqrt(var + LN_EPS) * g + b).astype(x.dtype)


def swiglu(x, w_gu, w_down):
    g, u = jnp.split(x @ w_gu, 2, axis=-1)
    return (jax.nn.silu(g) * u) @ w_down


def partial_rope(x, pos):
    half = ROT_DIM // 2
    inv_freq = ROPE_THETA ** (-jnp.arange(half, dtype=jnp.float32) / half)
    ang = pos.astype(jnp.float32)[:, None] * inv_freq
    cos = jnp.cos(ang)[:, None, :]
    sin = jnp.sin(ang)[:, None, :]
    xr = x[..., :ROT_DIM].astype(jnp.float32)
    x1, x2 = xr[..., :half], xr[..., half:]
    rot = jnp.concatenate([x1 * cos - x2 * sin, x2 * cos + x1 * sin], axis=-1).astype(x.dtype)
    return jnp.concatenate([rot, x[..., ROT_DIM:]], axis=-1)


def masked_softmax(s, mask):
    s = jnp.where(mask, s.astype(jnp.float32), -jnp.inf)
    m = jnp.max(s, axis=-1, keepdims=True)
    m = jnp.where(jnp.isfinite(m), m, 0.0)
    e = jnp.where(mask, jnp.exp(s - m), 0.0)
    return e / jnp.maximum(e.sum(-1, keepdims=True), 1e-30)


def s5_discretize(lam_re, lam_im, log_dt, b_re, b_im):
    dt = jnp.exp(log_dt.astype(jnp.float32))[:, None]
    lr = lam_re.astype(jnp.float32)
    li = lam_im.astype(jnp.float32)
    mag = jnp.exp(lr * dt)
    a_re = mag * jnp.cos(li * dt)
    a_im = mag * jnp.sin(li * dt)
    den = lr * lr + li * li
    r_re = ((a_re - 1.0) * lr + a_im * li) / den
    r_im = (a_im * lr - (a_re - 1.0) * li) / den
    br = b_re.astype(jnp.float32)
    bi = b_im.astype(jnp.float32)
    bb_re = r_re[..., None] * br - r_im[..., None] * bi
    bb_im = r_re[..., None] * bi + r_im[..., None] * br
    return a_re, a_im, bb_re, bb_im


def complex_affine_combine(e1, e2):
    a1r, a1i, b1r, b1i = e1
    a2r, a2i, b2r, b2i = e2
    return (a2r * a1r - a2i * a1i,
            a2r * a1i + a2i * a1r,
            a2r * b1r - a2i * b1i + b2r,
            a2r * b1i + a2i * b1r + b2i)


def s5_scan(u, s0, lam_re, lam_im, log_dt, b_re, b_im, c_re, c_im, d_skip):
    B, T, _ = u.shape
    a_re, a_im, bb_re, bb_im = s5_discretize(lam_re, lam_im, log_dt, b_re, b_im)
    uf = u.astype(jnp.float32)
    ug = uf.reshape(B, T, S5_GROUPS, S5_GROUP)
    bu_re = jnp.einsum('btgi,gpi->tbgp', ug, bb_re)
    bu_im = jnp.einsum('btgi,gpi->tbgp', ug, bb_im)
    a_shape = (T, 1, S5_GROUPS, S5_STATE)
    acc_re, acc_im, s_re, s_im = lax.associative_scan(
        complex_affine_combine,
        (jnp.broadcast_to(a_re, a_shape), jnp.broadcast_to(a_im, a_shape), bu_re, bu_im), axis=0)
    if s0 is not None:
        s0_re = s0[0].astype(jnp.float32)[None]
        s0_im = s0[1].astype(jnp.float32)[None]
        s_re, s_im = (s_re + acc_re * s0_re - acc_im * s0_im,
                      s_im + acc_re * s0_im + acc_im * s0_re)
    y = (jnp.einsum('tbgp,gop->btgo', s_re, c_re.astype(jnp.float32))
         - jnp.einsum('tbgp,gop->btgo', s_im, c_im.astype(jnp.float32)))
    y = y.reshape(B, T, S5_WIDTH) + d_skip.astype(jnp.float32) * uf
    return y.astype(u.dtype), s_re[-1].astype(u.dtype), s_im[-1].astype(u.dtype)


def causal_short_conv(v, w, buf):
    B, T, C = v.shape
    if buf is None:
        buf = jnp.zeros((B, CONV_K - 1, C), v.dtype)
    vf = jnp.concatenate([buf.astype(v.dtype), v], axis=1)
    y = lax.conv_general_dilated(vf, w[:, None, :].astype(v.dtype), window_strides=(1,), padding='VALID',
                                 dimension_numbers=('NWC', 'WIO', 'NWC'), feature_group_count=C)
    return y, vf[:, -(CONV_K - 1):]


def nsa_compress(x_raw, pe, w1, w2):
    B, T = x_raw.shape[:2]
    n_ch = -(-T // CMP_STRIDE)
    x = jnp.pad(x_raw, ((0, 0), (0, n_ch * CMP_STRIDE - T), (0, 0), (0, 0)))
    ch = x.reshape(B, n_ch, CMP_STRIDE, NSA_KV_HEADS, NSA_HEAD_DIM)
    w1 = w1.reshape(CMP_LEN, NSA_HEAD_DIM, CMP_HIDDEN)
    h_a = jnp.einsum('bcskd,sdh->bckh', ch, w1[:CMP_STRIDE])
    h_b = jnp.einsum('bcskd,sdh->bckh', ch, w1[CMP_STRIDE:])
    h_pe = jnp.einsum('sd,sdh->h', pe, w1)
    h = jax.nn.gelu(h_a[:, :-1] + h_b[:, 1:] + h_pe)
    return h @ w2


def cmp_to_sel_map(n_cmp, n_sel):
    c0 = np.arange(n_cmp) * CMP_STRIDE
    s0 = np.arange(n_sel) * SEL_LEN
    m = (c0[:, None] < s0[None, :] + SEL_LEN) & (s0[None, :] < c0[:, None] + CMP_LEN)
    return jnp.asarray(m.astype(np.float32))


def to_sel_blocks(x):
    B, T = x.shape[:2]
    ns = -(-T // SEL_LEN)
    x = jnp.pad(x, ((0, 0), (0, ns * SEL_LEN - T), (0, 0), (0, 0)))
    return x.reshape(B, ns, SEL_LEN, NSA_KV_HEADS, NSA_HEAD_DIM).transpose(0, 3, 1, 2, 4)


def nsa_core(q, q_pos, gates, kc, vc, c_end, ksb, vsb, kw, vw, kw_pos):
    B, Tq = q.shape[:2]
    qg = q.reshape(B, Tq, NSA_KV_HEADS, NSA_GQ, NSA_HEAD_DIM) * (NSA_HEAD_DIM ** -0.5)
    s_c = jnp.einsum('btkgd,bnkd->btkgn', qg, kc)
    p_c = masked_softmax(s_c, (c_end[None, :] <= q_pos[:, None])[None, :, None, None, :])
    o_c = jnp.einsum('btkgn,bnkd->btkgd', p_c.astype(vc.dtype), vc)
    n_sel = ksb.shape[2]
    imp = jnp.einsum('btkn,nj->btkj', p_c.sum(3), cmp_to_sel_map(kc.shape[1], n_sel))
    blk = jnp.arange(n_sel)[None, :]
    cur = (q_pos // SEL_LEN)[:, None]
    valid = blk <= cur
    forced = valid & ((blk == 0) | (blk >= cur - 1))
    score = jnp.where(forced[None, :, None, :], FORCE_SCORE,
                      jnp.where(valid[None, :, None, :], imp, -jnp.inf))
    _, idx = lax.top_k(score, min(SEL_TOP, n_sel))
    b_i = jnp.arange(B)[:, None, None, None]
    k_i = jnp.arange(NSA_KV_HEADS)[None, None, :, None]
    ks_sel = ksb[b_i, k_i, idx]
    vs_sel = vsb[b_i, k_i, idx]
    k_pos = idx[..., None] * SEL_LEN + jnp.arange(SEL_LEN)
    s_s = jnp.einsum('btkgd,btknsd->btkgns', qg, ks_sel)
    m_s = jnp.broadcast_to((k_pos <= q_pos[None, :, None, None, None])[:, :, :, None], s_s.shape)
    flat = (B, Tq, NSA_KV_HEADS, NSA_GQ, -1)
    p_s = masked_softmax(s_s.reshape(flat), m_s.reshape(flat))
    o_s = jnp.einsum('btkgm,btkmd->btkgd', p_s.astype(vsb.dtype),
                     vs_sel.reshape(B, Tq, NSA_KV_HEADS, -1, NSA_HEAD_DIM))
    s_w = jnp.einsum('btkgd,bwkd->btkgw', qg, kw)
    dpos = q_pos[:, None] - kw_pos[None, :]
    m_w = (dpos >= 0) & (dpos < WINDOW) & (kw_pos[None, :] >= 0)
    p_w = masked_softmax(s_w, m_w[None, :, None, None, :])
    o_w = jnp.einsum('btkgw,bwkd->btkgd', p_w.astype(vw.dtype), vw)
    g = gates.reshape(B, Tq, NSA_KV_HEADS, NSA_GQ, 3)
    o = g[..., 0:1] * o_c + g[..., 1:2] * o_s + g[..., 2:3] * o_w
    return o.reshape(B, Tq, NSA_HEADS * NSA_HEAD_DIM)


def nsa_prompt_blocks(q, gates, kc, vc, c_end, ksb, vsb, kw, vw):
    B, T = q.shape[:2]
    nb = T // NSA_QBLOCK
    qb = q.reshape(B, nb, NSA_QBLOCK, NSA_HEADS, NSA_HEAD_DIM).swapaxes(0, 1)
    gb = gates.reshape(B, nb, NSA_QBLOCK, NSA_HEADS, 3).swapaxes(0, 1)
    pad = ((0, 0), (WINDOW, 0), (0, 0), (0, 0))
    kw_pad = jnp.pad(kw, pad)
    vw_pad = jnp.pad(vw, pad)
    span = WINDOW + NSA_QBLOCK

    def one_block(args):
        n, q_n, g_n = args
        start = n * NSA_QBLOCK
        q_pos = start + jnp.arange(NSA_QBLOCK)
        kw_n = lax.dynamic_slice_in_dim(kw_pad, start, span, axis=1)
        vw_n = lax.dynamic_slice_in_dim(vw_pad, start, span, axis=1)
        kw_pos = start - WINDOW + jnp.arange(span)
        return nsa_core(q_n, q_pos, g_n, kc, vc, c_end, ksb, vsb, kw_n, vw_n, kw_pos)

    out = lax.map(one_block, (jnp.arange(nb), qb, gb))
    return out.swapaxes(0, 1).reshape(B, T, NSA_HEADS * NSA_HEAD_DIM)


def gather_pages(pool, layer, page_table):
    g = pool[layer, page_table]
    B, n_pages = page_table.shape
    return g.reshape(B, n_pages * PAGE_SIZE, 2, NSA_KV_HEADS, NSA_HEAD_DIM)


def token_mixer(h, lp, past):
    B, T, _ = h.shape
    z = h @ lp['w_in']
    u, q, kv, g_nsa, conv_in, g_mrg = jnp.split(z, IN_OFFSETS, axis=-1)
    if past is None:
        layer = None
        offset = 0
    else:
        layer = past['layer']
        offset = past['page_table'].shape[1] * PAGE_SIZE
    pos = offset + jnp.arange(T)

    s0 = None if past is None else (past['s5_re'][layer], past['s5_im'][layer])
    y_a, s5_re, s5_im = s5_scan(u, s0, lp['s5_lambda_re'], lp['s5_lambda_im'], lp['s5_log_dt'],
                                lp['s5_b_re'], lp['s5_b_im'], lp['s5_c_re'], lp['s5_c_im'], lp['s5_d'])
    y_a = jax.nn.gelu(y_a)
    y_a = y_a * jax.nn.sigmoid(y_a @ lp['s5_w_glu'] + lp['s5_b_glu'])

    c_b, c_c, c_h = jnp.split(conv_in, 3, axis=-1)
    conv_y, conv_state = causal_short_conv(c_c * c_h, lp['conv_w'], None if past is None else past['conv'][layer])
    y_c = c_b * conv_y

    qh = partial_rope(q.reshape(B, T, NSA_HEADS, NSA_HEAD_DIM), pos)
    kc_raw, vc_raw, ks, vs, kw, vw = [t.reshape(B, T, NSA_KV_HEADS, NSA_HEAD_DIM) for t in jnp.split(kv, 6, axis=-1)]
    ks = partial_rope(ks, pos)
    kw = partial_rope(kw, pos)
    gates = jax.nn.sigmoid(g_nsa).reshape(B, T, NSA_HEADS, 3)
    cmp_rows = jnp.stack([kc_raw, vc_raw], axis=2)
    sel_rows = jnp.stack([ks, vs], axis=2)
    win_rows = jnp.stack([kw, vw], axis=2)
    if past is None:
        kc_all, vc_all, ks_all, vs_all = kc_raw, vc_raw, ks, vs
        win_all = win_rows
        keep = min(WINDOW, T)
    else:
        past_cmp = gather_pages(past['cmp'], layer, past['page_table'])
        past_sel = gather_pages(past['sel'], layer, past['page_table'])
        kc_all = jnp.concatenate([past_cmp[:, :, 0], kc_raw], axis=1)
        vc_all = jnp.concatenate([past_cmp[:, :, 1], vc_raw], axis=1)
        ks_all = jnp.concatenate([past_sel[:, :, 0], ks], axis=1)
        vs_all = jnp.concatenate([past_sel[:, :, 1], vs], axis=1)
        win_all = jnp.concatenate([past['win'][layer], win_rows], axis=1)
        keep = past['win'].shape[2]
    kc = nsa_compress(kc_all, lp['nsa_pe_k'], lp['nsa_phi_k1'], lp['nsa_phi_k2'])
    vc = nsa_compress(vc_all, lp['nsa_pe_v'], lp['nsa_phi_v1'], lp['nsa_phi_v2'])
    c_end = jnp.arange(kc.shape[1]) * CMP_STRIDE + CMP_LEN - 1
    kc = partial_rope(kc, c_end)
    ksb = to_sel_blocks(ks_all)
    vsb = to_sel_blocks(vs_all)
    if past is None:
        y_b = nsa_prompt_blocks(qh, gates, kc, vc, c_end, ksb, vsb, kw, vw)
    else:
        kw_pos = offset - keep + jnp.arange(keep + T)
        y_b = nsa_core(qh, pos, gates, kc, vc, c_end, ksb, vsb, win_all[:, :, 0], win_all[:, :, 1], kw_pos)
    win_state = win_all[:, -keep:]

    g = jax.nn.sigmoid(g_mrg).reshape(B, T, N_BRANCH, D_MODEL)
    merged = (g[:, :, 0] * (y_a @ lp['w_proj_a'])
              + g[:, :, 1] * (y_b @ lp['w_proj_b'])
              + g[:, :, 2] * (y_c @ lp['w_proj_c']))
    return merged @ lp['w_o'], (cmp_rows, sel_rows, win_state, conv_state, s5_re, s5_im)


def decoder_layer(x, lp, past):
    x = layer_norm(DN_ALPHA * x + FFN_RES * swiglu(x, lp['ffn1_w_gu'], lp['ffn1_w_down']), lp['ln_g'][0], lp['ln_b'][0])
    m, states = token_mixer(x, lp, past)
    x = layer_norm(DN_ALPHA * x + m, lp['ln_g'][1], lp['ln_b'][1])
    x = layer_norm(DN_ALPHA * x + FFN_RES * swiglu(x, lp['ffn2_w_gu'], lp['ffn2_w_down']), lp['ln_g'][2], lp['ln_b'][2])
    return x, states


def run_trunk(x, weights, past):
    per_layer = []
    for l in range(DEPTH):
        lp = {name: w[l] for name, w in weights.items()}
        x, st = decoder_layer(x, lp, None if past is None else dict(past, layer=l))
        per_layer.append(st)
    stacked = [jnp.stack(s, axis=0) for s in zip(*per_layer)]
    return x, stacked


def setup_inputs(seed: int = 0) -> dict:
    key = jax.random.key(seed)
    ks = jax.random.split(key, 48)
    cnt = [0]

    def nxt():
        cnt[0] += 1
        return ks[cnt[0] - 1]

    def nrm(shape, scale):
        return jax.random.normal(nxt(), shape, jnp.float32) * scale

    n_pages = PAST_LEN // PAGE_SIZE
    n_used = DEC_BATCH * n_pages
    n_pool = n_used + n_used // 4
    win_keep = min(WINDOW, PAST_LEN)
    kv_row = (2, NSA_KV_HEADS, NSA_HEAD_DIM)
    page_table = jax.random.permutation(nxt(), n_pool)[:n_used].reshape(DEC_BATCH, n_pages).astype(jnp.int32)
    lam_n = jnp.arange(S5_STATE, dtype=jnp.float32)
    gp = (DEPTH, S5_GROUPS, S5_STATE)
    return {
        'x_prompt': nrm((BATCH, SEQ, D_MODEL), 1.0),
        'x_sample': nrm((DEC_BATCH, DEC_SEQ, D_MODEL), 1.0),
        'cache_cmp_kv': nrm((DEPTH, n_pool, PAGE_SIZE) + kv_row, 1.0),
        'cache_sel_kv': nrm((DEPTH, n_pool, PAGE_SIZE) + kv_row, 1.0),
        'cache_win_kv': nrm((DEPTH, DEC_BATCH, win_keep) + kv_row, 1.0),
        'cache_conv': nrm((DEPTH, DEC_BATCH, CONV_K - 1, CONV_CH), 1.0),
        'state_s5_re': nrm((DEPTH, DEC_BATCH, S5_GROUPS, S5_STATE), 0.3),
        'state_s5_im': nrm((DEPTH, DEC_BATCH, S5_GROUPS, S5_STATE), 0.3),
        'page_table': page_table,
        'ln_g': 1.0 + nrm((DEPTH, 3, D_MODEL), 0.01),
        'ln_b': nrm((DEPTH, 3, D_MODEL), 0.01),
        'ffn1_w_gu': nrm((DEPTH, D_MODEL, 2 * D_FF), DN_BETA * D_MODEL ** -0.5),
        'ffn1_w_down': nrm((DEPTH, D_FF, D_MODEL), DN_BETA * D_FF ** -0.5),
        'ffn2_w_gu': nrm((DEPTH, D_MODEL, 2 * D_FF), DN_BETA * D_MODEL ** -0.5),
        'ffn2_w_down': nrm((DEPTH, D_FF, D_MODEL), DN_BETA * D_FF ** -0.5),
        'w_in': nrm((DEPTH, D_MODEL, D_IN), D_MODEL ** -0.5),
        's5_lambda_re': -0.5 + nrm(gp, 0.01),
        's5_lambda_im': math.pi * lam_n + nrm(gp, 0.01),
        's5_log_dt': jax.random.uniform(nxt(), (DEPTH, S5_GROUPS), jnp.float32,
                                        minval=math.log(S5_DT_MIN), maxval=math.log(S5_DT_MAX)),
        's5_b_re': nrm(gp + (S5_GROUP,), (2.0 * S5_GROUP) ** -0.5),
        's5_b_im': nrm(gp + (S5_GROUP,), (2.0 * S5_GROUP) ** -0.5),
        's5_c_re': nrm((DEPTH, S5_GROUPS, S5_GROUP, S5_STATE), (2.0 * S5_STATE) ** -0.5),
        's5_c_im': nrm((DEPTH, S5_GROUPS, S5_GROUP, S5_STATE), (2.0 * S5_STATE) ** -0.5),
        's5_d': nrm((DEPTH, S5_WIDTH), 1.0),
        's5_w_glu': nrm((DEPTH, S5_WIDTH, S5_WIDTH), S5_WIDTH ** -0.5),
        's5_b_glu': nrm((DEPTH, S5_WIDTH), 0.01),
        'nsa_pe_k': nrm((DEPTH, CMP_LEN, NSA_HEAD_DIM), 0.02),
        'nsa_pe_v': nrm((DEPTH, CMP_LEN, NSA_HEAD_DIM), 0.02),
        'nsa_phi_k1': nrm((DEPTH, CMP_LEN * NSA_HEAD_DIM, CMP_HIDDEN), (CMP_LEN * NSA_HEAD_DIM) ** -0.5),
        'nsa_phi_k2': nrm((DEPTH, CMP_HIDDEN, NSA_HEAD_DIM), CMP_HIDDEN ** -0.5),
        'nsa_phi_v1': nrm((DEPTH, CMP_LEN * NSA_HEAD_DIM, CMP_HIDDEN), (CMP_LEN * NSA_HEAD_DIM) ** -0.5),
        'nsa_phi_v2': nrm((DEPTH, CMP_HIDDEN, NSA_HEAD_DIM), CMP_HIDDEN ** -0.5),
        'conv_w': nrm((DEPTH, CONV_K, CONV_CH), CONV_K ** -0.5),
        'w_proj_a': nrm((DEPTH, S5_WIDTH, D_MODEL), DN_BETA * S5_WIDTH ** -0.5),
        'w_proj_b': nrm((DEPTH, NSA_HEADS * NSA_HEAD_DIM, D_MODEL), DN_BETA * (NSA_HEADS * NSA_HEAD_DIM) ** -0.5),
        'w_proj_c': nrm((DEPTH, CONV_CH, D_MODEL), DN_BETA * CONV_CH ** -0.5),
        'w_o': nrm((DEPTH, D_MODEL, D_MODEL), DN_BETA * D_MODEL ** -0.5),
    }


def reference(x_prompt, x_sample, cache_cmp_kv, cache_sel_kv, cache_win_kv, cache_conv, state_s5_re, state_s5_im,
              page_table, ln_g, ln_b, ffn1_w_gu, ffn1_w_down, ffn2_w_gu, ffn2_w_down, w_in,
              s5_lambda_re, s5_lambda_im, s5_log_dt, s5_b_re, s5_b_im, s5_c_re, s5_c_im, s5_d, s5_w_glu, s5_b_glu,
              nsa_pe_k, nsa_pe_v, nsa_phi_k1, nsa_phi_k2, nsa_phi_v1, nsa_phi_v2,
              conv_w, w_proj_a, w_proj_b, w_proj_c, w_o):
    weights = {
        'ln_g': ln_g, 'ln_b': ln_b,
        'ffn1_w_gu': ffn1_w_gu, 'ffn1_w_down': ffn1_w_down,
        'ffn2_w_gu': ffn2_w_gu, 'ffn2_w_down': ffn2_w_down,
        'w_in': w_in,
        's5_lambda_re': s5_lambda_re, 's5_lambda_im': s5_lambda_im, 's5_log_dt': s5_log_dt,
        's5_b_re': s5_b_re, 's5_b_im': s5_b_im, 's5_c_re': s5_c_re, 's5_c_im': s5_c_im,
        's5_d': s5_d, 's5_w_glu': s5_w_glu, 's5_b_glu': s5_b_glu,
        'nsa_pe_k': nsa_pe_k, 'nsa_pe_v': nsa_pe_v,
        'nsa_phi_k1': nsa_phi_k1, 'nsa_phi_k2': nsa_phi_k2, 'nsa_phi_v1': nsa_phi_v1, 'nsa_phi_v2': nsa_phi_v2,
        'conv_w': conv_w, 'w_proj_a': w_proj_a, 'w_proj_b': w_proj_b, 'w_proj_c': w_proj_c, 'w_o': w_o,
    }
    past = {
        'cmp': cache_cmp_kv, 'sel': cache_sel_kv, 'win': cache_win_kv, 'conv': cache_conv,
        's5_re': state_s5_re, 's5_im': state_s5_im, 'page_table': page_table,
    }
    y_prompt, (p_cmp, p_sel, p_win, p_conv, p_re, p_im) = run_trunk(x_prompt, weights, None)
    y_sample, (s_cmp, s_sel, s_win, s_conv, s_re, s_im) = run_trunk(x_sample, weights, past)
    return (y_prompt, y_sample, p_cmp, s_cmp, p_sel, s_sel, p_win, s_win, p_conv, s_conv, p_re, s_re, p_im, s_im)
```

```python
import functools
import math

import numpy as np
import jax
import jax.numpy as jnp
from jax import lax
from jax.experimental import pallas as pl
from jax.experimental.pallas import tpu as pltpu

F32 = jnp.float32
MXU_DTYPE = jnp.bfloat16

D_MODEL = 1024
DEPTH = 2
PAGE_SIZE = 128
DN_ALPHA = (2.0 * DEPTH) ** 0.25
LN_EPS = 1e-5
D_FF = 2816
FFN_RES = 0.5
S5_WIDTH = 512
S5_GROUP = 16
S5_GROUPS = S5_WIDTH // S5_GROUP
S5_STATE = 64
S5_FLAT = S5_GROUPS * S5_STATE
NSA_HEADS = 8
NSA_KV_HEADS = 2
NSA_HEAD_DIM = 64
NSA_GQ = NSA_HEADS // NSA_KV_HEADS
CMP_LEN = 32
CMP_STRIDE = 16
CMP_HIDDEN = 128
SEL_LEN = 64
SEL_TOP = 16
WINDOW = 512
ROPE_THETA = 500000.0
ROT_DIM = NSA_HEAD_DIM // 4
ROT_HALF = ROT_DIM // 2
FORCE_SCORE = 1e9
CONV_CH = 512
CONV_K = 3
N_BRANCH = 3
KV_WIDTH = NSA_KV_HEADS * NSA_HEAD_DIM
ROW_WIDTH = 2 * KV_WIDTH
IN_WIDTHS = (S5_WIDTH, NSA_HEADS * NSA_HEAD_DIM, 6 * KV_WIDTH, 3 * NSA_HEADS, 3 * CONV_CH, N_BRANCH * D_MODEL)
IN_OFFSETS = tuple(int(o) for o in np.cumsum(IN_WIDTHS)[:-1])

LANES = 128
V7X_VMEM_BYTES = 64 * 1024 * 1024
VMEM_LIMIT = (V7X_VMEM_BYTES * 7) // 8

QP_WIDTH = NSA_HEADS * LANES
MASK_NEG = -1e30
M_INIT = -1e29

C_U = 0
C_Q = C_U + S5_WIDTH
C_KV = C_Q + QP_WIDTH
C_G = C_KV + 6 * KV_WIDTH
C_CONV = C_G + LANES
C_END = C_CONV + 3 * CONV_CH


def _sigmoid(x):
    return 1.0 / (1.0 + jnp.exp(-x))


def _gelu_tanh(x):
    return 0.5 * x * (1.0 + jnp.tanh(math.sqrt(2.0 / math.pi) * (x + 0.044715 * (x * x * x))))


def _layer_norm(x, g, b):
    mu = jnp.mean(x, axis=-1, keepdims=True)
    xc = x - mu
    var = jnp.mean(xc * xc, axis=-1, keepdims=True)
    return xc * lax.rsqrt(var + LN_EPS) * g + b


def _dot(a, b):
    return jnp.dot(a.astype(MXU_DTYPE), b.astype(MXU_DTYPE), preferred_element_type=F32)


def _dot_nt(a, b):
    return lax.dot_general(a.astype(MXU_DTYPE), b.astype(MXU_DTYPE), (((1,), (1,)), ((), ())),
                           preferred_element_type=F32)


def _rope_lanes(x, cm, sa, sb):
    return x * cm + pltpu.roll(x, LANES - ROT_HALF, 1) * sa + pltpu.roll(x, ROT_HALF, 1) * sb


def _rope_tables(pos):
    inv_freq = ROPE_THETA ** (-jnp.arange(ROT_HALF, dtype=F32) / ROT_HALF)
    ang = pos.astype(F32)[:, None] * inv_freq
    cos, sin = jnp.cos(ang), jnp.sin(ang)
    n = pos.shape[0]
    rest = NSA_HEAD_DIM - ROT_DIM
    cm = jnp.concatenate([cos, cos, jnp.ones((n, rest), F32)], axis=1)
    sa = jnp.concatenate([-sin, jnp.zeros((n, ROT_HALF + rest), F32)], axis=1)
    sb = jnp.concatenate([jnp.zeros((n, ROT_HALF), F32), sin, jnp.zeros((n, rest), F32)], axis=1)
    return tuple(jnp.tile(t, (1, LANES // NSA_HEAD_DIM)) for t in (cm, sa, sb))


def _params(sem, big=False):
    return pltpu.CompilerParams(dimension_semantics=sem, vmem_limit_bytes=VMEM_LIMIT if big else None)


def _ffn_ln_kernel(x_ref, wg_ref, wu_ref, wd_ref, lng_ref, lnb_ref, o_ref, acc_ref):
    f = pl.program_id(1)
    x = x_ref[...]
    xb = x.astype(MXU_DTYPE)
    gate = jnp.dot(xb, wg_ref[...], preferred_element_type=F32)
    up = jnp.dot(xb, wu_ref[...], preferred_element_type=F32)
    part = _dot(gate * _sigmoid(gate) * up, wd_ref[...])

    @pl.when(f == 0)
    def _():
        acc_ref[...] = part

    @pl.when(f != 0)
    def _():
        acc_ref[...] += part

    @pl.when(f == pl.num_programs(1) - 1)
    def _():
        o_ref[...] = _layer_norm(DN_ALPHA * x + FFN_RES * acc_ref[...], lng_ref[...], lnb_ref[...])


def ffn_ln(x, wg, wu, wd, lng, lnb):
    n, d = x.shape
    tm = min(n, 512)
    tf = D_FF // 2
    return pl.pallas_call(
        _ffn_ln_kernel,
        grid=(n // tm, D_FF // tf),
        in_specs=[pl.BlockSpec((tm, d), lambda i, f: (i, 0)),
                  pl.BlockSpec((d, tf), lambda i, f: (0, f)),
                  pl.BlockSpec((d, tf), lambda i, f: (0, f)),
                  pl.BlockSpec((tf, d), lambda i, f: (f, 0)),
                  pl.BlockSpec((1, d), lambda i, f: (0, 0)),
                  pl.BlockSpec((1, d), lambda i, f: (0, 0))],
        out_specs=pl.BlockSpec((tm, d), lambda i, f: (i, 0)),
        out_shape=jax.ShapeDtypeStruct((n, d), F32),
        scratch_shapes=[pltpu.VMEM((tm, d), F32)],
        compiler_params=_params(("parallel", "arbitrary"), big=True),
        name="ffn_ln",
    )(x, wg, wu, wd, lng, lnb)


def _in_proj_kernel(x_ref, w_ref, cm_ref, sa_ref, sb_ref, cw_ref, cprev_ref,
                    u_ref, q_ref, cmp_ref, sel_ref, win_ref, selb_ref, winb_ref, g_ref, yc_ref, cst_ref,
                    vbuf_ref, *, decode):
    x = x_ref[0]
    rows = x.shape[0]
    z = _dot(x, w_ref[...])
    cm, sa, sb = cm_ref[...], sa_ref[...], sb_ref[...]

    u_ref[...] = z[:, C_U:C_U + S5_WIDTH]
    scale = NSA_HEAD_DIM ** -0.5
    for h in range(NSA_HEADS):
        c = C_Q + h * LANES
        q_ref[0, :, h * LANES:(h + 1) * LANES] = (_rope_lanes(z[:, c:c + LANES], cm, sa, sb) * scale).astype(q_ref.dtype)
    cmp_ref[0] = z[:, C_KV:C_KV + ROW_WIDTH]
    ks = _rope_lanes(z[:, C_KV + 2 * KV_WIDTH:C_KV + 3 * KV_WIDTH], cm, sa, sb)
    vs = z[:, C_KV + 3 * KV_WIDTH:C_KV + 4 * KV_WIDTH]
    kw = _rope_lanes(z[:, C_KV + 4 * KV_WIDTH:C_KV + 5 * KV_WIDTH], cm, sa, sb)
    vw = z[:, C_KV + 5 * KV_WIDTH:C_KV + 6 * KV_WIDTH]
    sel_ref[0, :, 0:KV_WIDTH] = ks
    sel_ref[0, :, KV_WIDTH:ROW_WIDTH] = vs
    win_ref[0, :, 0:KV_WIDTH] = kw
    win_ref[0, :, KV_WIDTH:ROW_WIDTH] = vw
    selb_ref[0, :, 0:KV_WIDTH] = ks.astype(selb_ref.dtype)
    selb_ref[0, :, KV_WIDTH:ROW_WIDTH] = vs.astype(selb_ref.dtype)
    winb_ref[0, :, 0:KV_WIDTH] = kw.astype(winb_ref.dtype)
    winb_ref[0, :, KV_WIDTH:ROW_WIDTH] = vw.astype(winb_ref.dtype)
    g_ref[0] = _sigmoid(z[:, C_G:C_G + LANES])

    cb = z[:, C_CONV:C_CONV + CONV_CH]
    v = z[:, C_CONV + CONV_CH:C_CONV + 2 * CONV_CH] * z[:, C_CONV + 2 * CONV_CH:C_CONV + 3 * CONV_CH]
    w0, w1, w2 = cw_ref[0:1, :], cw_ref[1:2, :], cw_ref[2:3, :]
    if decode:
        p0 = cprev_ref[:, 0:CONV_CH]
        p1 = cprev_ref[:, CONV_CH:2 * CONV_CH]
        yc_ref[0] = (cb * (w0 * p0 + w1 * p1 + w2 * v)).astype(yc_ref.dtype)
        cst_ref[:, 0:CONV_CH] = p1
        cst_ref[:, CONV_CH:2 * CONV_CH] = v
    else:
        @pl.when(pl.program_id(1) == 0)
        def _():
            vbuf_ref[0:8, :] = jnp.zeros((8, CONV_CH), F32)

        vbuf_ref[8:8 + rows, :] = v
        conv = w0 * vbuf_ref[6:6 + rows, :] + w1 * vbuf_ref[7:7 + rows, :] + w2 * v
        yc_ref[0] = (cb * conv).astype(yc_ref.dtype)
        last = vbuf_ref[rows:rows + 8, :]
        vbuf_ref[0:8, :] = last
        cst_ref[0] = last[6:8, :]


def in_proj(x, w, tables, conv_w, conv_prev, *, decode):
    nb, t, d = x.shape
    tt = t if decode else min(t, 512)
    grid = (nb, t // tt)
    cm, sa, sb = tables
    row3 = lambda w_: pl.BlockSpec((1, tt, w_), lambda b, i: (b, i, 0))
    tab = pl.BlockSpec((tt, LANES), lambda b, i: (i, 0))
    if decode:
        cprev_spec = pl.BlockSpec((t, 2 * CONV_CH), lambda b, i: (0, 0))
        cst_spec = pl.BlockSpec((t, 2 * CONV_CH), lambda b, i: (0, 0))
        cst_shape = jax.ShapeDtypeStruct((t, 2 * CONV_CH), F32)
    else:
        cprev_spec = pl.BlockSpec((8, LANES), lambda b, i: (0, 0))
        cst_spec = pl.BlockSpec((1, CONV_K - 1, CONV_CH), lambda b, i: (b, 0, 0))
        cst_shape = jax.ShapeDtypeStruct((nb, CONV_K - 1, CONV_CH), F32)
    out_shape = (
        jax.ShapeDtypeStruct((t, nb * S5_WIDTH), F32),
        jax.ShapeDtypeStruct((nb, t, QP_WIDTH), MXU_DTYPE),
        jax.ShapeDtypeStruct((nb, t, ROW_WIDTH), F32),
        jax.ShapeDtypeStruct((nb, t, ROW_WIDTH), F32),
        jax.ShapeDtypeStruct((nb, t, ROW_WIDTH), F32),
        jax.ShapeDtypeStruct((nb, t, ROW_WIDTH), MXU_DTYPE),
        jax.ShapeDtypeStruct((nb, t, ROW_WIDTH), MXU_DTYPE),
        jax.ShapeDtypeStruct((nb, t, LANES), F32),
        jax.ShapeDtypeStruct((nb, t, CONV_CH), MXU_DTYPE),
        cst_shape,
    )
    out_specs = (pl.BlockSpec((tt, S5_WIDTH), lambda b, i: (i, b)), row3(QP_WIDTH), row3(ROW_WIDTH), row3(ROW_WIDTH),
                 row3(ROW_WIDTH), row3(ROW_WIDTH), row3(ROW_WIDTH), row3(LANES), row3(CONV_CH), cst_spec)
    return pl.pallas_call(
        functools.partial(_in_proj_kernel, decode=decode),
        grid=grid,
        in_specs=[row3(d), pl.BlockSpec((d, C_END), lambda b, i: (0, 0)), tab, tab, tab,
                  pl.BlockSpec((8, CONV_CH), lambda b, i: (0, 0)), cprev_spec],
        out_specs=out_specs,
        out_shape=out_shape,
        scratch_shapes=[pltpu.VMEM((tt + 8, CONV_CH), F32)],
        compiler_params=_params(("parallel", "arbitrary"), big=True),
        name="in_proj",
    )(x, w, cm, sa, sb, conv_w, conv_prev)


S5_COLS = 512


def _s5_kernel(u_ref, s0r_ref, s0i_ref, ar_ref, ai_ref, br_ref, bi_ref, cr_ref, ci_ref, d_ref, wglu_ref, bglu_ref,
               y_ref, sr_ref, si_ref, xr_ref, xi_ref, *, nb, tc):
    @pl.when(pl.program_id(0) == 0)
    def _():
        sr_ref[...] = s0r_ref[...]
        si_ref[...] = s0i_ref[...]

    u = u_ref[...]
    ub = u.astype(MXU_DTYPE)
    xr_ref[...] = jnp.dot(ub, br_ref[...], preferred_element_type=F32)
    xi_ref[...] = jnp.dot(ub, bi_ref[...], preferred_element_type=F32)

    for c in range(S5_FLAT // S5_COLS):
        cols = slice(c * S5_COLS, (c + 1) * S5_COLS)
        ar = jnp.broadcast_to(ar_ref[:, cols], (nb, S5_COLS))
        ai = jnp.broadcast_to(ai_ref[:, cols], (nb, S5_COLS))

        def step(t, carry):
            sr, si = carry
            rows = pl.ds(pl.multiple_of(t * nb, nb), nb)
            nr = ar * sr - ai * si + xr_ref[rows, cols]
            ni = ar * si + ai * sr + xi_ref[rows, cols]
            xr_ref[rows, cols] = nr
            xi_ref[rows, cols] = ni
            return nr, ni

        sr, si = lax.fori_loop(0, tc, step, (sr_ref[:, cols], si_ref[:, cols]), unroll=min(tc, 8))
        sr_ref[:, cols] = sr
        si_ref[:, cols] = si

    y = _dot(xr_ref[...], cr_ref[...]) - _dot(xi_ref[...], ci_ref[...]) + d_ref[...] * u
    y = _gelu_tanh(y)
    y_ref[...] = (y * _sigmoid(_dot(y, wglu_ref[...]) + bglu_ref[...])).astype(y_ref.dtype)


def s5_branch(u, s0r, s0i, p, *, nb):
    n = u.shape[0]
    t = n // nb
    tc = min(t, 128)
    full = lambda a: pl.BlockSpec(a.shape, lambda i: (0,) * a.ndim)
    args = (u, s0r, s0i, p['a_re'], p['a_im'], p['bb_re'], p['bb_im'], p['c_re'], p['c_im'], p['d'], p['w_glu'], p['b_glu'])
    return pl.pallas_call(
        functools.partial(_s5_kernel, nb=nb, tc=tc),
        grid=(t // tc,),
        in_specs=[pl.BlockSpec((tc * nb, S5_WIDTH), lambda i: (i, 0))] + [full(a) for a in args[1:]],
        out_specs=(pl.BlockSpec((tc * nb, S5_WIDTH), lambda i: (i, 0)),
                   pl.BlockSpec((nb, S5_FLAT), lambda i: (0, 0)),
                   pl.BlockSpec((nb, S5_FLAT), lambda i: (0, 0))),
        out_shape=(jax.ShapeDtypeStruct((n, S5_WIDTH), MXU_DTYPE),
                   jax.ShapeDtypeStruct((nb, S5_FLAT), F32),
                   jax.ShapeDtypeStruct((nb, S5_FLAT), F32)),
        scratch_shapes=[pltpu.VMEM((tc * nb, S5_FLAT), F32), pltpu.VMEM((tc * nb, S5_FLAT), F32)],
        compiler_params=_params(("arbitrary",), big=True),
        name="s5",
    )(*args)


def _s5_params(lam_re, lam_im, log_dt, b_re, b_im, c_re, c_im, d_skip, w_glu, b_glu):
    dt = jnp.exp(log_dt.astype(F32))[:, None]
    lr, li = lam_re.astype(F32), lam_im.astype(F32)
    mag = jnp.exp(lr * dt)
    a_re = mag * jnp.cos(li * dt)
    a_im = mag * jnp.sin(li * dt)
    den = lr * lr + li * li
    r_re = ((a_re - 1.0) * lr + a_im * li) / den
    r_im = (a_im * lr - (a_re - 1.0) * li) / den
    bb_re = r_re[..., None] * b_re - r_im[..., None] * b_im
    bb_im = r_re[..., None] * b_im + r_im[..., None] * b_re
    eye = jnp.eye(S5_GROUPS, dtype=F32)
    blk_in = lambda bb: jnp.einsum('gpi,gh->gihp', bb, eye).reshape(S5_WIDTH, S5_FLAT).astype(MXU_DTYPE)
    blk_out = lambda c: jnp.einsum('gop,gh->gpho', c.astype(F32), eye).reshape(S5_FLAT, S5_WIDTH).astype(MXU_DTYPE)
    return dict(a_re=a_re.reshape(1, S5_FLAT), a_im=a_im.reshape(1, S5_FLAT),
                bb_re=blk_in(bb_re), bb_im=blk_in(bb_im), c_re=blk_out(c_re), c_im=blk_out(c_im),
                d=d_skip.reshape(1, S5_WIDTH).astype(F32), w_glu=w_glu.astype(MXU_DTYPE),
                b_glu=b_glu.reshape(1, S5_WIDTH).astype(F32))


CHUNK_WIDTH = CMP_STRIDE * ROW_WIDTH
HID_WIDTH = 4 * CMP_HIDDEN


def _compress_tail(ha, hb_next, hpe, w2_ref, cm, sa, sb):
    kv = _dot(_gelu_tanh(ha + hb_next + hpe), w2_ref[...])
    return _rope_lanes(kv[:, 0:KV_WIDTH], cm, sa, sb), kv[:, KV_WIDTH:ROW_WIDTH]


def _compress_kernel(x_ref, w1a_ref, w1b_ref, pea_ref, peb_ref, w2_ref, cm_ref, sa_ref, sb_ref,
                     kc_ref, vc_ref, hb_ref):
    x = x_ref[0].astype(MXU_DTYPE)
    n = x.shape[0]
    ha = jnp.dot(x, w1a_ref[...], preferred_element_type=F32)
    hb_ref[0:n, :] = jnp.dot(x, w1b_ref[...], preferred_element_type=F32)
    hb_ref[n:n + 8, :] = jnp.zeros((8, HID_WIDTH), F32)
    hpe = (_dot(pea_ref[...], w1a_ref[...]) + _dot(peb_ref[...], w1b_ref[...]))[0:1, :]
    kc, vc = _compress_tail(ha, hb_ref[1:n + 1, :], hpe, w2_ref, cm_ref[...], sa_ref[...], sb_ref[...])
    kc_ref[0] = kc.astype(kc_ref.dtype)
    vc_ref[0] = vc.astype(vc_ref.dtype)


def compress(rows, cw, tables):
    b, n, _ = rows.shape
    cm, sa, sb = tables
    full = lambda a: pl.BlockSpec(a.shape, lambda i: (0,) * a.ndim)
    args = (rows, cw['w1a'], cw['w1b'], cw['pe_a'], cw['pe_b'], cw['w2'], cm, sa, sb)
    return pl.pallas_call(
        _compress_kernel,
        grid=(b,),
        in_specs=[pl.BlockSpec((1, n, CHUNK_WIDTH), lambda i: (i, 0, 0))] + [full(a) for a in args[1:]],
        out_specs=(pl.BlockSpec((1, n, KV_WIDTH), lambda i: (i, 0, 0)),) * 2,
        out_shape=(jax.ShapeDtypeStruct((b, n, KV_WIDTH), MXU_DTYPE),) * 2,
        scratch_shapes=[pltpu.VMEM((n + 8, HID_WIDTH), F32)],
        compiler_params=_params(("parallel",), big=True),
        name="compress",
    )(*args)


def _compress_params(phi_k1, phi_k2, phi_v1, phi_v2, pe_k, pe_v):
    wk = phi_k1.reshape(CMP_LEN, NSA_HEAD_DIM, CMP_HIDDEN)
    wv = phi_v1.reshape(CMP_LEN, NSA_HEAD_DIM, CMP_HIDDEN)
    w = jnp.stack([wk, wk, wv, wv], axis=1)
    w1 = jnp.einsum('sjdu,jm->sjdmu', w, jnp.eye(4, dtype=F32)).reshape(CMP_LEN * ROW_WIDTH, HID_WIDTH)
    w2 = jnp.einsum('jud,jm->jumd', jnp.stack([phi_k2, phi_k2, phi_v2, phi_v2]), jnp.eye(4, dtype=F32))
    pe = jnp.concatenate([pe_k, pe_k, pe_v, pe_v], axis=1)
    pad8 = lambda r: jnp.concatenate([r, jnp.zeros((7, CHUNK_WIDTH), F32)], axis=0)
    return dict(w1a=w1[:CHUNK_WIDTH].astype(MXU_DTYPE), w1b=w1[CHUNK_WIDTH:].astype(MXU_DTYPE),
                w2=w2.reshape(HID_WIDTH, ROW_WIDTH).astype(MXU_DTYPE),
                pe_a=pad8(pe[:CMP_STRIDE].reshape(1, CHUNK_WIDTH)), pe_b=pad8(pe[CMP_STRIDE:].reshape(1, CHUNK_WIDTH)))


def _cmp_to_sel_map(n_cmp, n_rows, n_sel):
    n_cols = -(-n_sel // LANES) * LANES
    c0 = np.arange(n_rows) * CMP_STRIDE
    s0 = np.arange(n_cols) * SEL_LEN
    m = (c0[:, None] < s0[None, :] + SEL_LEN) & (s0[None, :] < c0[:, None] + CMP_LEN)
    m &= (np.arange(n_rows) < n_cmp)[:, None] & (np.arange(n_cols) < n_sel)[None, :]
    return jnp.asarray(m.astype(np.float32))


def _masked_softmax_rows(s, mask):
    s = jnp.where(mask, s, MASK_NEG)
    m = jnp.max(s, axis=-1, keepdims=True)
    e = jnp.where(mask, jnp.exp(s - m), 0.0)
    return e / jnp.maximum(jnp.sum(e, axis=-1, keepdims=True), 1e-30)


def _flash_update(slot, s, mask, v, m_ref, l_ref, acc_ref):
    s = jnp.where(mask, s, MASK_NEG)
    m_prev = m_ref[slot]
    m_new = jnp.maximum(m_prev, jnp.max(s, axis=-1, keepdims=True))
    alpha = jnp.exp(m_prev - m_new)
    p = jnp.exp(s - jnp.concatenate([m_new] * (s.shape[1] // LANES), axis=1))
    l_ref[slot] = alpha * l_ref[slot] + jnp.sum(p, axis=-1, keepdims=True)
    acc_ref[slot] = alpha * acc_ref[slot] + _dot(p, v)
    m_ref[slot] = m_new


def _top_blocks_t(score_ref, blk_t, n_sel):
    s_all = score_ref[...]

    def step(j, rank):
        r = score_ref[pl.ds(j, 1), :]
        ahead = (r > s_all) | ((r == s_all) & (blk_t > j))
        return rank + jnp.where(ahead, 1.0, 0.0)

    rank = lax.fori_loop(0, n_sel, step, jnp.zeros(s_all.shape, F32), unroll=8)
    return jnp.where(rank < float(min(SEL_TOP, n_sel)), 1.0, 0.0)


def _nsa_prompt_kernel(q_ref, g_ref, kc_ref, vc_ref, sel_ref, win_ref, map_ref, o_ref,
                       sc_ref, oc_ref, m_ref, l_ref, acc_ref, *, tq, nc, n_sel):
    qi = pl.program_id(1)
    tk = tq
    q0 = qi * tq
    qpos = q0 + lax.broadcasted_iota(jnp.int32, (tq, 1), 0)
    ncp = kc_ref.shape[1]
    n_idx = lax.broadcasted_iota(jnp.int32, (1, ncp), 1)
    cmask = (CMP_STRIDE * n_idx + (CMP_LEN - 1) <= qpos) & (n_idx < nc)
    kc = kc_ref[0]
    vc = vc_ref[0]
    nsr = sc_ref.shape[0]
    blk_t = lax.broadcasted_iota(jnp.int32, (nsr, tq), 0)
    cur_t = lax.div(q0 + lax.broadcasted_iota(jnp.int32, (nsr, tq), 1), SEL_LEN)
    valid_t = blk_t <= cur_t
    forced_t = valid_t & ((blk_t == 0) | (blk_t >= cur_t - 1))
    gates = g_ref[0]
    lane_half = lax.div(lax.broadcasted_iota(jnp.int32, (tq, LANES), 1), NSA_HEAD_DIM)
    kidx = lax.broadcasted_iota(jnp.int32, (1, tk), 1)
    e_blk = lax.broadcasted_iota(jnp.int32, (LANES, tk), 0)
    e_key = lax.broadcasted_iota(jnp.int32, (LANES, tk), 1)

    for k in range(NSA_KV_HEADS):
        heads = [k * NSA_GQ + gq for gq in range(NSA_GQ)]
        qslice = lambda h: q_ref[0, :, h * LANES:(h + 1) * LANES]

        psum = jnp.zeros((tq, ncp), F32)
        for gq, h in enumerate(heads):
            p = _masked_softmax_rows(_dot_nt(qslice(h), kc), cmask)
            psum = psum + p
            oc_ref[gq] = _dot(p, vc)
        imp = jnp.dot(psum, map_ref[...], precision=lax.Precision.HIGHEST, preferred_element_type=F32)
        sc_ref[...] = jnp.where(forced_t, FORCE_SCORE, jnp.where(valid_t, imp.T[0:nsr, :], -jnp.inf))
        sel_t = _top_blocks_t(sc_ref, blk_t, n_sel)
        if nsr < LANES:
            sel_t = jnp.concatenate([sel_t, jnp.zeros((LANES - nsr, tq), F32)], axis=0)
        sel = sel_t.T.astype(MXU_DTYPE)

        for slot in range(2 * NSA_GQ):
            m_ref[slot] = jnp.full((tq, LANES), M_INIT, F32)
            l_ref[slot] = jnp.zeros((tq, LANES), F32)
            acc_ref[slot] = jnp.zeros((tq, LANES), F32)

        def sel_step(j, carry):
            k0 = pl.multiple_of(j * tk, tk)
            kj = sel_ref[0, pl.ds(k0, tk), 0:KV_WIDTH]
            vj = sel_ref[0, pl.ds(k0, tk), KV_WIDTH:ROW_WIDTH]
            expand = jnp.where(e_blk == lax.div(k0 + e_key, SEL_LEN), 1.0, 0.0).astype(MXU_DTYPE)
            chosen = jnp.dot(sel, expand, preferred_element_type=F32) > 0.5
            mask = chosen & (k0 + kidx <= qpos)
            for gq, h in enumerate(heads):
                _flash_update(gq, _dot_nt(qslice(h), kj), mask, vj, m_ref, l_ref, acc_ref)
            return carry

        lax.fori_loop(0, qi + 1, sel_step, 0)

        def win_step(j, carry):
            k0 = pl.multiple_of(j * tk, tk)
            kj = win_ref[0, pl.ds(k0, tk), 0:KV_WIDTH]
            vj = win_ref[0, pl.ds(k0, tk), KV_WIDTH:ROW_WIDTH]
            kpos = k0 + kidx
            mask = (kpos <= qpos) & (kpos > qpos - WINDOW)
            for gq, h in enumerate(heads):
                _flash_update(NSA_GQ + gq, _dot_nt(qslice(h), kj), mask, vj, m_ref, l_ref, acc_ref)
            return carry

        lax.fori_loop(lax.div(jnp.maximum(q0 - (WINDOW - 1), 0), tk), qi + 1, win_step, 0)

        for gq, h in enumerate(heads):
            o_s = acc_ref[gq] / jnp.maximum(l_ref[gq], 1e-30)
            o_w = acc_ref[NSA_GQ + gq] / jnp.maximum(l_ref[NSA_GQ + gq], 1e-30)
            c = 3 * h
            out = gates[:, c:c + 1] * oc_ref[gq] + gates[:, c + 1:c + 2] * o_s + gates[:, c + 2:c + 3] * o_w
            o_ref[0, :, h * LANES:(h + 1) * LANES] = jnp.where(lane_half == k, out, 0.0).astype(o_ref.dtype)


def nsa_prompt(qp, gates, kc, vc, selb, winb):
    b, t, _ = qp.shape
    tq = min(t, 256)
    ncp = kc.shape[1]
    nc = ncp - 1
    n_sel = -(-t // SEL_LEN)
    nsr = -(-n_sel // 8) * 8
    cmap = _cmp_to_sel_map(nc, ncp, n_sel)
    per_b = lambda a: pl.BlockSpec((1,) + a.shape[1:], lambda i, j: (i, 0, 0))
    return pl.pallas_call(
        functools.partial(_nsa_prompt_kernel, tq=tq, nc=nc, n_sel=n_sel),
        grid=(b, t // tq),
        in_specs=[pl.BlockSpec((1, tq, QP_WIDTH), lambda i, j: (i, j, 0)),
                  pl.BlockSpec((1, tq, LANES), lambda i, j: (i, j, 0)),
                  per_b(kc), per_b(vc), per_b(selb), per_b(winb),
                  pl.BlockSpec(cmap.shape, lambda i, j: (0, 0))],
        out_specs=pl.BlockSpec((1, tq, QP_WIDTH), lambda i, j: (i, j, 0)),
        out_shape=jax.ShapeDtypeStruct((b, t, QP_WIDTH), MXU_DTYPE),
        scratch_shapes=[pltpu.VMEM((nsr, tq), F32),
                        pltpu.VMEM((NSA_GQ, tq, LANES), F32),
                        pltpu.VMEM((2 * NSA_GQ, tq, LANES), F32),
                        pltpu.VMEM((2 * NSA_GQ, tq, LANES), F32),
                        pltpu.VMEM((2 * NSA_GQ, tq, LANES), F32)],
        compiler_params=_params(("parallel", "parallel"), big=True),
        name="nsa_prompt",
    )(qp, gates, kc, vc, selb, winb, cmap)


PAGE_CHUNKS = PAGE_SIZE // CMP_STRIDE
PAGE_BLOCKS = PAGE_SIZE // SEL_LEN


def _paged_hidden_kernel(pt_ref, pool_ref, w1a_ref, w1b_ref, pea_ref, peb_ref, ha_ref, hb_ref, hpe_ref, buf_ref, sem,
                         *, layer, n_pages, pg):
    b = pl.program_id(0)
    hpe_ref[...] = _dot(pea_ref[...], w1a_ref[...]) + _dot(peb_ref[...], w1b_ref[...])
    base = b * n_pages + pl.program_id(1) * pg

    def page_copy(r):
        return pltpu.make_async_copy(pool_ref.at[layer, pt_ref[base + r]],
                                     buf_ref.at[pl.ds(pl.multiple_of(r * PAGE_CHUNKS, PAGE_CHUNKS), PAGE_CHUNKS), :], sem)

    def start(r, c):
        page_copy(r).start()
        return c

    def wait(r, c):
        page_copy(r).wait()
        return c

    lax.fori_loop(0, pg, start, 0)
    lax.fori_loop(0, pg, wait, 0)
    x = buf_ref[...].astype(MXU_DTYPE)
    ha_ref[0] = jnp.dot(x, w1a_ref[...], preferred_element_type=F32)
    hb_ref[0] = jnp.dot(x, w1b_ref[...], preferred_element_type=F32)


def paged_hidden(pool, page_table, cw, *, layer):
    depth, n_pool = pool.shape[:2]
    b, n_pages = page_table.shape
    pg = math.gcd(n_pages, 64)
    chunks = pool.reshape(depth, n_pool, PAGE_CHUNKS, CHUNK_WIDTH)
    full = lambda a: pl.BlockSpec(a.shape, lambda i, j, pt: (0,) * a.ndim)
    out = jax.ShapeDtypeStruct((b, n_pages * PAGE_CHUNKS, HID_WIDTH), F32)
    out_spec = pl.BlockSpec((1, pg * PAGE_CHUNKS, HID_WIDTH), lambda i, j, pt: (i, j, 0))
    ws = (cw['w1a'], cw['w1b'], cw['pe_a'], cw['pe_b'])
    return pl.pallas_call(
        functools.partial(_paged_hidden_kernel, layer=layer, n_pages=n_pages, pg=pg),
        grid_spec=pltpu.PrefetchScalarGridSpec(
            num_scalar_prefetch=1,
            grid=(b, n_pages // pg),
            in_specs=[pl.BlockSpec(memory_space=pl.ANY)] + [full(a) for a in ws],
            out_specs=(out_spec, out_spec, pl.BlockSpec((8, HID_WIDTH), lambda i, j, pt: (0, 0))),
            scratch_shapes=[pltpu.VMEM((pg * PAGE_CHUNKS, CHUNK_WIDTH), F32), pltpu.SemaphoreType.DMA(())]),
        out_shape=(out, out, jax.ShapeDtypeStruct((8, HID_WIDTH), F32)),
        compiler_params=_params(("arbitrary", "arbitrary"), big=True),
        name="paged_hidden",
    )(page_table.reshape(-1), chunks, *ws)


SEL_KEYS = SEL_TOP * SEL_LEN


def _decode_cmp_kernel(ha_ref, hb_ref, hpe_ref, new_ref, q_ref, w1b0_ref, w2_ref, cm_ref, sa_ref, sb_ref, map_ref,
                       oc_ref, idx_ref, kpos_ref, hbs_ref, *, q_pos, n_sel):
    n = ha_ref.shape[1]
    hbs_ref[0:n, :] = hb_ref[0]
    new8 = jnp.concatenate([new_ref[0], jnp.zeros((7, ROW_WIDTH), F32)], axis=0)
    hbs_ref[n:n + 8, :] = _dot(new8, w1b0_ref[...])
    kc, vc = _compress_tail(ha_ref[0], hbs_ref[1:n + 1, :], hpe_ref[0:1, :], w2_ref,
                            cm_ref[...], sa_ref[...], sb_ref[...])
    q8 = q_ref[0]
    n_idx = lax.broadcasted_iota(jnp.int32, (1, n), 1)
    p = _masked_softmax_rows(_dot_nt(q8, kc), CMP_STRIDE * n_idx + (CMP_LEN - 1) <= q_pos)
    oc_ref[0] = _dot(p, vc)

    row = lax.broadcasted_iota(jnp.int32, p.shape, 0)
    psum = jnp.zeros(p.shape, F32)
    for k in range(NSA_KV_HEADS):
        grp = jnp.sum(jnp.where(lax.div(row, NSA_GQ) == k, p, 0.0), axis=0, keepdims=True)
        psum = jnp.where(row == k, grp, psum)
    imp = jnp.dot(psum, map_ref[...], precision=lax.Precision.HIGHEST, preferred_element_type=F32)
    nsp = imp.shape[1]
    blk = lax.broadcasted_iota(jnp.int32, (NSA_HEADS, nsp), 1)
    cur = q_pos // SEL_LEN
    forced = (blk == 0) | (blk >= cur - 1)
    score = jnp.where(blk < n_sel, jnp.where(forced, FORCE_SCORE, imp), -jnp.inf)
    blk_f = blk.astype(F32)
    lane = lax.broadcasted_iota(jnp.int32, (NSA_HEADS, LANES), 1)
    key = lax.broadcasted_iota(jnp.int32, (NSA_HEADS, SEL_KEYS), 1)
    key_slot = lax.div(key, SEL_LEN)
    key_off = key - key_slot * SEL_LEN
    idx = jnp.zeros((NSA_HEADS, LANES), jnp.int32)
    kpos = jnp.zeros((NSA_HEADS, SEL_KEYS), jnp.int32)
    for i in range(SEL_TOP):
        top = jnp.max(score, axis=1, keepdims=True)
        pick_f = jnp.min(jnp.where(score == top, blk_f, float(nsp)), axis=1, keepdims=True)
        pick = pick_f.astype(jnp.int32)
        idx = jnp.where(lane == i, pick, idx)
        kpos = jnp.where(key_slot == i, pick * SEL_LEN + key_off, kpos)
        score = jnp.where(blk == pick, -jnp.inf, score)
    idx_ref[0] = idx
    kpos_ref[0] = kpos


def decode_cmp(ha, hb, hpe, cmp_new, q8, cw, tables, cmap, *, q_pos, n_sel):
    b, n, _ = ha.shape
    cm, sa, sb = tables
    w1b0 = cw['w1b'][0:ROW_WIDTH]
    per_b = lambda a: pl.BlockSpec((1,) + a.shape[1:], lambda i: (i, 0, 0))
    full = lambda a: pl.BlockSpec(a.shape, lambda i: (0,) * a.ndim)
    out8 = lambda w_, dt: (jax.ShapeDtypeStruct((b, NSA_HEADS, w_), dt), pl.BlockSpec((1, NSA_HEADS, w_), lambda i: (i, 0, 0)))
    outs = (out8(LANES, F32), out8(LANES, jnp.int32), out8(SEL_KEYS, jnp.int32))
    return pl.pallas_call(
        functools.partial(_decode_cmp_kernel, q_pos=q_pos, n_sel=n_sel),
        grid=(b,),
        in_specs=[per_b(ha), per_b(hb), full(hpe), per_b(cmp_new), per_b(q8), full(w1b0), full(cw['w2']),
                  full(cm), full(sa), full(sb), full(cmap)],
        out_specs=tuple(o[1] for o in outs),
        out_shape=tuple(o[0] for o in outs),
        scratch_shapes=[pltpu.VMEM((n + 8, HID_WIDTH), F32)],
        compiler_params=_params(("parallel",), big=True),
        name="decode_cmp",
    )(ha, hb, hpe, cmp_new, q8, w1b0, cw['w2'], cm, sa, sb, cmap)


def _decode_attn_kernel(idx_ref, pt_ref, pool_ref, win_ref, seln_ref, winn_ref, q_ref, g_ref, oc_ref, kpos_ref,
                        o_ref, wst_ref, kv_ref, wbuf_ref, sem, *, layer, n_pages, q_pos):
    b = pl.program_id(0)
    n_past = n_pages * PAGE_BLOCKS
    keep = win_ref.shape[2]
    new_block = jnp.where(lax.broadcasted_iota(jnp.int32, (SEL_LEN, ROW_WIDTH), 0) == 0,
                          jnp.broadcast_to(seln_ref[0], (SEL_LEN, ROW_WIDTH)), 0.0)

    def block(k, n_):
        blk = idx_ref[(b * NSA_KV_HEADS + k) * SEL_TOP + n_]
        page = pt_ref[b * n_pages + jnp.minimum(lax.div(blk, PAGE_BLOCKS), n_pages - 1)]
        src = pool_ref.at[layer, page * PAGE_BLOCKS + lax.rem(blk, PAGE_BLOCKS)]
        return blk, pltpu.make_async_copy(src, kv_ref.at[k, n_ * SEL_LEN:(n_ + 1) * SEL_LEN, :], sem)

    for k in range(NSA_KV_HEADS):
        for n_ in range(SEL_TOP):
            blk, cp = block(k, n_)

            @pl.when(blk < n_past)
            def _():
                cp.start()

            @pl.when(blk >= n_past)
            def _():
                kv_ref[k, n_ * SEL_LEN:(n_ + 1) * SEL_LEN, :] = new_block

    wbuf_ref[0:keep, :] = win_ref[0, 0]
    wbuf_ref[keep:keep + 8, :] = jnp.concatenate([winn_ref[0], jnp.zeros((7, ROW_WIDTH), F32)], axis=0)
    wst_ref[0] = wbuf_ref[1:keep + 1, :]
    q8 = q_ref[0]
    wall = wbuf_ref[...]
    r = lax.broadcasted_iota(jnp.int32, (1, keep + 8), 1)
    p_w = _masked_softmax_rows(_dot_nt(q8, wall[:, 0:KV_WIDTH]), (r <= keep) & (keep - r < WINDOW))
    o_w = _dot(p_w, wall[:, KV_WIDTH:ROW_WIDTH])

    for k in range(NSA_KV_HEADS):
        for n_ in range(SEL_TOP):
            blk, cp = block(k, n_)

            @pl.when(blk < n_past)
            def _():
                cp.wait()

    row = lax.broadcasted_iota(jnp.int32, (NSA_HEADS, LANES), 0)
    o_s = jnp.zeros((NSA_HEADS, LANES), F32)
    for k in range(NSA_KV_HEADS):
        keys = kv_ref[k]
        p_s = _masked_softmax_rows(_dot_nt(q8, keys[:, 0:KV_WIDTH]), kpos_ref[0, k:k + 1, :] <= q_pos)
        o_s = jnp.where(lax.div(row, NSA_GQ) == k, _dot(p_s, keys[:, KV_WIDTH:ROW_WIDTH]), o_s)

    g = g_ref[0]
    out = g[:, 0:1] * oc_ref[0] + g[:, 1:2] * o_s + g[:, 2:3] * o_w
    lane_half = lax.div(lax.broadcasted_iota(jnp.int32, (NSA_HEADS, LANES), 1), NSA_HEAD_DIM)
    o_ref[0] = jnp.where(lane_half == lax.div(row, NSA_GQ), out, 0.0).astype(o_ref.dtype)


def decode_attn(idx, page_table, pool, win_cache, sel_new, win_new, q8, gates3, oc, kpos, *, layer, q_pos):
    depth, n_pool = pool.shape[:2]
    b, n_pages = page_table.shape
    keep = win_cache.shape[2]
    blocks = pool.reshape(depth, n_pool * PAGE_BLOCKS, SEL_LEN, ROW_WIDTH)
    win = win_cache.reshape(depth, b, keep, ROW_WIDTH)
    per_b = lambda a: pl.BlockSpec((1,) + a.shape[1:], lambda i, ix, pt: (i, 0, 0))
    idx_flat = idx[:, 0:NSA_KV_HEADS, 0:SEL_TOP].reshape(-1)
    return pl.pallas_call(
        functools.partial(_decode_attn_kernel, layer=layer, n_pages=n_pages, q_pos=q_pos),
        grid_spec=pltpu.PrefetchScalarGridSpec(
            num_scalar_prefetch=2,
            grid=(b,),
            in_specs=[pl.BlockSpec(memory_space=pl.ANY),
                      pl.BlockSpec((1, 1, keep, ROW_WIDTH), lambda i, ix, pt: (layer, i, 0, 0)),
                      per_b(sel_new), per_b(win_new), per_b(q8), per_b(gates3), per_b(oc), per_b(kpos)],
            out_specs=(pl.BlockSpec((1, NSA_HEADS, LANES), lambda i, ix, pt: (i, 0, 0)),
                       pl.BlockSpec((1, keep, ROW_WIDTH), lambda i, ix, pt: (i, 0, 0))),
            scratch_shapes=[pltpu.VMEM((NSA_KV_HEADS, SEL_KEYS, ROW_WIDTH), F32),
                            pltpu.VMEM((keep + 8, ROW_WIDTH), F32),
                            pltpu.SemaphoreType.DMA(())]),
        out_shape=(jax.ShapeDtypeStruct((b, NSA_HEADS, LANES), MXU_DTYPE),
                   jax.ShapeDtypeStruct((b, keep, ROW_WIDTH), F32)),
        compiler_params=_params(("arbitrary",), big=True),
        name="decode_attn",
    )(idx_flat, page_table.reshape(-1), blocks, win, sel_new, win_new, q8, gates3, oc, kpos)


def _merge_ln_kernel(x_ref, ya_ref, yb_ref, yc_ref, wgm_ref, wa_ref, wb_ref, wc_ref, wo_ref, lng_ref, lnb_ref, o_ref):
    x = x_ref[0]
    xb = x.astype(MXU_DTYPE)
    gate = lambda j: _sigmoid(jnp.dot(xb, wgm_ref[:, j * D_MODEL:(j + 1) * D_MODEL], preferred_element_type=F32))
    merged = (gate(0) * _dot(ya_ref[...], wa_ref[...]) + gate(1) * _dot(yb_ref[0], wb_ref[...])
              + gate(2) * _dot(yc_ref[0], wc_ref[...]))
    o_ref[0] = _layer_norm(DN_ALPHA * x + _dot(merged, wo_ref[...]), lng_ref[...], lnb_ref[...])


def merge_ln(x, ya, yb, yc, mw, lng, lnb):
    nb, t, d = x.shape
    tt = min(t, 256)
    row3 = lambda w_: pl.BlockSpec((1, tt, w_), lambda b, i: (b, i, 0))
    full = lambda a: pl.BlockSpec(a.shape, lambda b, i: (0,) * a.ndim)
    ws = (mw['w_gm'], mw['w_a'], mw['w_b'], mw['w_c'], mw['w_o'], lng, lnb)
    return pl.pallas_call(
        _merge_ln_kernel,
        grid=(nb, t // tt),
        in_specs=[row3(d), pl.BlockSpec((tt, S5_WIDTH), lambda b, i: (i, b)), row3(QP_WIDTH), row3(CONV_CH)]
                 + [full(a) for a in ws],
        out_specs=row3(d),
        out_shape=jax.ShapeDtypeStruct((nb, t, d), F32),
        compiler_params=_params(("parallel", "parallel"), big=True),
        name="merge_ln",
    )(x, ya, yb, yc, *ws)


def _layer_weights(w, l):
    bf = lambda a: a.astype(MXU_DTYPE)
    row = lambda a: a.reshape(1, -1).astype(F32)
    w_in = w['w_in'][l]
    w_u, w_q, w_kv, w_gn, w_conv, w_gm = jnp.split(w_in, IN_OFFSETS, axis=-1)
    head_half = (jnp.arange(NSA_HEADS) // NSA_GQ)[None, :, None]

    def pad_heads(a3):
        z = jnp.zeros_like(a3)
        lo = jnp.concatenate([a3, z], axis=-1)
        hi = jnp.concatenate([z, a3], axis=-1)
        return jnp.where(head_half == 0, lo, hi).reshape(a3.shape[0], QP_WIDTH)

    w_qp = pad_heads(w_q.reshape(D_MODEL, NSA_HEADS, NSA_HEAD_DIM))
    w_gpad = jnp.concatenate([w_gn, jnp.zeros((D_MODEL, LANES - w_gn.shape[1]), F32)], axis=1)
    w_proj = bf(jnp.concatenate([w_u, w_qp, w_kv, w_gpad, w_conv], axis=1))
    assert w_proj.shape[1] == C_END
    w_b = pad_heads(w['w_proj_b'][l].T.reshape(D_MODEL, NSA_HEADS, NSA_HEAD_DIM)).T
    conv_w = jnp.concatenate([w['conv_w'][l], jnp.zeros((8 - CONV_K, CONV_CH), F32)], axis=0)
    gu = lambda name: jnp.split(w[name][l], 2, axis=-1)
    return dict(
        ffn1=(bf(gu('ffn1_w_gu')[0]), bf(gu('ffn1_w_gu')[1]), bf(w['ffn1_w_down'][l])),
        ffn2=(bf(gu('ffn2_w_gu')[0]), bf(gu('ffn2_w_gu')[1]), bf(w['ffn2_w_down'][l])),
        ln=[(row(w['ln_g'][l, j]), row(w['ln_b'][l, j])) for j in range(3)],
        w_proj=w_proj, conv_w=conv_w,
        s5=_s5_params(w['s5_lambda_re'][l], w['s5_lambda_im'][l], w['s5_log_dt'][l], w['s5_b_re'][l], w['s5_b_im'][l],
                      w['s5_c_re'][l], w['s5_c_im'][l], w['s5_d'][l], w['s5_w_glu'][l], w['s5_b_glu'][l]),
        cmp=_compress_params(w['nsa_phi_k1'][l], w['nsa_phi_k2'][l], w['nsa_phi_v1'][l], w['nsa_phi_v2'][l],
                             w['nsa_pe_k'][l], w['nsa_pe_v'][l]),
        merge=dict(w_gm=bf(w_gm), w_a=bf(w['w_proj_a'][l]), w_b=bf(w_b), w_c=bf(w['w_proj_c'][l]), w_o=bf(w['w_o'][l])),
    )


def _prompt_trunk(x, layers):
    b, t, d = x.shape
    n_chunks = t // CMP_STRIDE
    tok_tables = _rope_tables(jnp.arange(t))
    cmp_tables = _rope_tables(jnp.arange(n_chunks) * CMP_STRIDE + (CMP_LEN - 1))
    zero_state = jnp.zeros((b, S5_FLAT), F32)
    keep = min(WINDOW, t)
    states = []
    x2 = x.reshape(b * t, d)
    for lw in layers:
        x2 = ffn_ln(x2, *lw['ffn1'], *lw['ln'][0])
        u, qp, cmp_rows, sel_rows, win_rows, selb, winb, gates, yc, conv_state = in_proj(
            x2.reshape(b, t, d), lw['w_proj'], tok_tables, lw['conv_w'], jnp.zeros((8, LANES), F32), decode=False)
        ya, s_re, s_im = s5_branch(u.reshape(t * b, S5_WIDTH), zero_state, zero_state, lw['s5'], nb=b)
        kc, vc = compress(cmp_rows.reshape(b, n_chunks, CHUNK_WIDTH), lw['cmp'], cmp_tables)
        yb = nsa_prompt(qp, gates, kc, vc, selb, winb)
        x3 = merge_ln(x2.reshape(b, t, d), ya.reshape(t, b * S5_WIDTH), yb, yc, lw['merge'], *lw['ln'][1])
        x2 = ffn_ln(x3.reshape(b * t, d), *lw['ffn2'], *lw['ln'][2])
        kv_shape = (2, NSA_KV_HEADS, NSA_HEAD_DIM)
        states.append((cmp_rows.reshape((b, t) + kv_shape), sel_rows.reshape((b, t) + kv_shape),
                       win_rows[:, t - keep:].reshape((b, keep) + kv_shape), conv_state,
                       s_re.reshape(b, S5_GROUPS, S5_STATE), s_im.reshape(b, S5_GROUPS, S5_STATE)))
    return x2.reshape(b, t, d), [jnp.stack(s, axis=0) for s in zip(*states)]


def _sample_trunk(x, layers, cache_cmp, cache_sel, cache_win, cache_conv, s5_re, s5_im, page_table):
    bs, dec_seq, d = x.shape
    assert dec_seq == 1
    n_pages = page_table.shape[1]
    q_pos = n_pages * PAGE_SIZE
    n_blocks = n_pages * PAGE_CHUNKS
    n_sel = q_pos // SEL_LEN + 1
    assert cache_win.shape[2] == min(WINDOW, q_pos) and n_sel >= SEL_TOP
    tok_tables = _rope_tables(jnp.full((bs,), q_pos))
    cmp_tables = _rope_tables(jnp.arange(n_blocks) * CMP_STRIDE + (CMP_LEN - 1))
    cmap = _cmp_to_sel_map(n_blocks, n_blocks, n_sel)
    kv_shape = (2, NSA_KV_HEADS, NSA_HEAD_DIM)
    states = []
    x2 = x.reshape(bs, d)
    for l, lw in enumerate(layers):
        x2 = ffn_ln(x2, *lw['ffn1'], *lw['ln'][0])
        u, qp, cmp_new, sel_new, win_new, _, _, gates, yc, conv_state = in_proj(
            x2[None], lw['w_proj'], tok_tables, lw['conv_w'], cache_conv[l].reshape(bs, (CONV_K - 1) * CONV_CH),
            decode=True)
        ya, s_re, s_im = s5_branch(u, s5_re[l].reshape(bs, S5_FLAT), s5_im[l].reshape(bs, S5_FLAT), lw['s5'], nb=bs)
        ha, hb, hpe = paged_hidden(cache_cmp, page_table, lw['cmp'], layer=l)
        q8 = qp.reshape(bs, NSA_HEADS, LANES)
        per_seq = lambda a: a.reshape(bs, 1, ROW_WIDTH)
        oc, idx, kpos = decode_cmp(ha, hb, hpe, per_seq(cmp_new), q8, lw['cmp'], cmp_tables, cmap,
                                   q_pos=q_pos, n_sel=n_sel)
        g3 = gates[0, :, 0:3 * NSA_HEADS].reshape(bs, NSA_HEADS, 3)
        g3 = jnp.concatenate([g3, jnp.zeros((bs, NSA_HEADS, LANES - 3), F32)], axis=-1)
        yb, win_state = decode_attn(idx, page_table, cache_sel, cache_win, per_seq(sel_new), per_seq(win_new), q8, g3,
                                    oc, kpos, layer=l, q_pos=q_pos)
        x3 = merge_ln(x2[None], ya, yb.reshape(1, bs, QP_WIDTH), yc, lw['merge'], *lw['ln'][1])
        x2 = ffn_ln(x3[0], *lw['ffn2'], *lw['ln'][2])
        states.append((cmp_new.reshape((bs, 1) + kv_shape), sel_new.reshape((bs, 1) + kv_shape),
                       win_state.reshape((bs, -1) + kv_shape), conv_state.reshape(bs, CONV_K - 1, CONV_CH),
                       s_re.reshape(bs, S5_GROUPS, S5_STATE), s_im.reshape(bs, S5_GROUPS, S5_STATE)))
    return x2.reshape(bs, 1, d), [jnp.stack(s, axis=0) for s in zip(*states)]


def kernel(x_prompt, x_sample, cache_cmp_kv, cache_sel_kv, cache_win_kv, cache_conv, state_s5_re, state_s5_im, page_table, ln_g, ln_b, ffn1_w_gu, ffn1_w_down, ffn2_w_gu, ffn2_w_down, w_in, s5_lambda_re, s5_lambda_im, s5_log_dt, s5_b_re, s5_b_im, s5_c_re, s5_c_im, s5_d, s5_w_glu, s5_b_glu, nsa_pe_k, nsa_pe_v, nsa_phi_k1, nsa_phi_k2, nsa_phi_v1, nsa_phi_v2, conv_w, w_proj_a, w_proj_b, w_proj_c, w_o):
    w = dict(ln_g=ln_g, ln_b=ln_b, ffn1_w_gu=ffn1_w_gu, ffn1_w_down=ffn1_w_down, ffn2_w_gu=ffn2_w_gu,
             ffn2_w_down=ffn2_w_down, w_in=w_in, s5_lambda_re=s5_lambda_re, s5_lambda_im=s5_lambda_im,
             s5_log_dt=s5_log_dt, s5_b_re=s5_b_re, s5_b_im=s5_b_im, s5_c_re=s5_c_re, s5_c_im=s5_c_im, s5_d=s5_d,
             s5_w_glu=s5_w_glu, s5_b_glu=s5_b_glu, nsa_pe_k=nsa_pe_k, nsa_pe_v=nsa_pe_v, nsa_phi_k1=nsa_phi_k1,
             nsa_phi_k2=nsa_phi_k2, nsa_phi_v1=nsa_phi_v1, nsa_phi_v2=nsa_phi_v2, conv_w=conv_w,
             w_proj_a=w_proj_a, w_proj_b=w_proj_b, w_proj_c=w_proj_c, w_o=w_o)
    layers = [_layer_weights(w, l) for l in range(DEPTH)]
    y_prompt, (p_cmp, p_sel, p_win, p_conv, p_re, p_im) = _prompt_trunk(x_prompt, layers)
    y_sample, (s_cmp, s_sel, s_win, s_conv, s_re, s_im) = _sample_trunk(
        x_sample, layers, cache_cmp_kv, cache_sel_kv, cache_win_kv, cache_conv, state_s5_re, state_s5_im, page_table)
    return (y_prompt, y_sample, p_cmp, s_cmp, p_sel, s_sel, p_win, s_win, p_conv, s_conv, p_re, s_re, p_im, s_im)
```

```python
import functools
import math

import numpy as np
import jax
import jax.numpy as jnp
from jax import lax
from jax.experimental import pallas as pl
from jax.experimental.pallas import tpu as pltpu

F32 = jnp.float32
MXU_DTYPE = jnp.bfloat16

D_MODEL = 1024
DEPTH = 2
PAGE_SIZE = 128
DN_ALPHA = (2.0 * DEPTH) ** 0.25
LN_EPS = 1e-5
D_FF = 2816
FFN_RES = 0.5
S5_WIDTH = 512
S5_GROUP = 16
S5_GROUPS = S5_WIDTH // S5_GROUP
S5_STATE = 64
S5_FLAT = S5_GROUPS * S5_STATE
NSA_HEADS = 8
NSA_KV_HEADS = 2
NSA_HEAD_DIM = 64
NSA_GQ = NSA_HEADS // NSA_KV_HEADS
CMP_LEN = 32
CMP_STRIDE = 16
CMP_HIDDEN = 128
SEL_LEN = 64
SEL_TOP = 16
WINDOW = 512
ROPE_THETA = 500000.0
ROT_DIM = NSA_HEAD_DIM // 4
ROT_HALF = ROT_DIM // 2
FORCE_SCORE = 1e9
CONV_CH = 512
CONV_K = 3
N_BRANCH = 3
KV_WIDTH = NSA_KV_HEADS * NSA_HEAD_DIM
ROW_WIDTH = 2 * KV_WIDTH
IN_WIDTHS = (S5_WIDTH, NSA_HEADS * NSA_HEAD_DIM, 6 * KV_WIDTH, 3 * NSA_HEADS, 3 * CONV_CH, N_BRANCH * D_MODEL)
IN_OFFSETS = tuple(int(o) for o in np.cumsum(IN_WIDTHS)[:-1])

LANES = 128
V7X_VMEM_BYTES = 64 * 1024 * 1024
VMEM_LIMIT = (V7X_VMEM_BYTES * 7) // 8

QP_WIDTH = NSA_HEADS * LANES
MASK_NEG = -1e30
M_INIT = -1e29

C_U = 0
C_Q = C_U + S5_WIDTH
C_KV = C_Q + QP_WIDTH
C_G = C_KV + 6 * KV_WIDTH
C_CONV = C_G + LANES
C_END = C_CONV + 3 * CONV_CH


def _sigmoid(x):
    return 1.0 / (1.0 + jnp.exp(-x))


def _gelu_tanh(x):
    return 0.5 * x * (1.0 + jnp.tanh(math.sqrt(2.0 / math.pi) * (x + 0.044715 * (x * x * x))))


def _layer_norm(x, g, b):
    mu = jnp.mean(x, axis=-1, keepdims=True)
    xc = x - mu
    var = jnp.mean(xc * xc, axis=-1, keepdims=True)
    return xc * lax.rsqrt(var + LN_EPS) * g + b


def _dot(a, b):
    return jnp.dot(a.astype(MXU_DTYPE), b.astype(MXU_DTYPE), preferred_element_type=F32)


def _dot_nt(a, b):
    return lax.dot_general(a.astype(MXU_DTYPE), b.astype(MXU_DTYPE), (((1,), (1,)), ((), ())),
                           preferred_element_type=F32)


def _rope_lanes(x, cm, sa, sb):
    return x * cm + pltpu.roll(x, LANES - ROT_HALF, 1) * sa + pltpu.roll(x, ROT_HALF, 1) * sb


def _rope_tables(pos):
    inv_freq = ROPE_THETA ** (-jnp.arange(ROT_HALF, dtype=F32) / ROT_HALF)
    ang = pos.astype(F32)[:, None] * inv_freq
    cos, sin = jnp.cos(ang), jnp.sin(ang)
    n = pos.shape[0]
    rest = NSA_HEAD_DIM - ROT_DIM
    cm = jnp.concatenate([cos, cos, jnp.ones((n, rest), F32)], axis=1)
    sa = jnp.concatenate([-sin, jnp.zeros((n, ROT_HALF + rest), F32)], axis=1)
    sb = jnp.concatenate([jnp.zeros((n, ROT_HALF), F32), sin, jnp.zeros((n, rest), F32)], axis=1)
    return tuple(jnp.tile(t, (1, LANES // NSA_HEAD_DIM)) for t in (cm, sa, sb))


def _params(sem, big=False):
    return pltpu.CompilerParams(dimension_semantics=sem, vmem_limit_bytes=VMEM_LIMIT if big else None)


def _ffn_ln_kernel(x_ref, wg_ref, wu_ref, wd_ref, lng_ref, lnb_ref, o_ref, acc_ref):
    f = pl.program_id(1)
    x = x_ref[...]
    xb = x.astype(MXU_DTYPE)
    gate = jnp.dot(xb, wg_ref[...], preferred_element_type=F32)
    up = jnp.dot(xb, wu_ref[...], preferred_element_type=F32)
    part = _dot(gate * _sigmoid(gate) * up, wd_ref[...])

    @pl.when(f == 0)
    def _():
        acc_ref[...] = part

    @pl.when(f != 0)
    def _():
        acc_ref[...] += part

    @pl.when(f == pl.num_programs(1) - 1)
    def _():
        o_ref[...] = _layer_norm(DN_ALPHA * x + FFN_RES * acc_ref[...], lng_ref[...], lnb_ref[...])


def ffn_ln(x, wg, wu, wd, lng, lnb):
    n, d = x.shape
    tm = min(n, 512)
    tf = D_FF // 2
    return pl.pallas_call(
        _ffn_ln_kernel,
        grid=(n // tm, D_FF // tf),
        in_specs=[pl.BlockSpec((tm, d), lambda i, f: (i, 0)),
                  pl.BlockSpec((d, tf), lambda i, f: (0, f)),
                  pl.BlockSpec((d, tf), lambda i, f: (0, f)),
                  pl.BlockSpec((tf, d), lambda i, f: (f, 0)),
                  pl.BlockSpec((1, d), lambda i, f: (0, 0)),
                  pl.BlockSpec((1, d), lambda i, f: (0, 0))],
        out_specs=pl.BlockSpec((tm, d), lambda i, f: (i, 0)),
        out_shape=jax.ShapeDtypeStruct((n, d), F32),
        scratch_shapes=[pltpu.VMEM((tm, d), F32)],
        compiler_params=_params(("parallel", "arbitrary"), big=True),
        name="ffn_ln",
    )(x, wg, wu, wd, lng, lnb)


def _in_proj_kernel(x_ref, w_ref, cm_ref, sa_ref, sb_ref, cw_ref, cprev_ref,
                    u_ref, q_ref, cmp_ref, sel_ref, win_ref, g_ref, yc_ref, cst_ref, *rest, decode):
    if decode:
        (vbuf_ref,) = rest
    else:
        selk_ref, selv_ref, wink_ref, winv_ref, vbuf_ref = rest
    x = x_ref[0]
    rows = x.shape[0]
    z = _dot(x, w_ref[...])
    cm, sa, sb = cm_ref[...], sa_ref[...], sb_ref[...]

    u_ref[...] = z[:, C_U:C_U + S5_WIDTH]
    scale = NSA_HEAD_DIM ** -0.5
    for h in range(NSA_HEADS):
        c = C_Q + h * LANES
        q_ref[0, :, h * LANES:(h + 1) * LANES] = (_rope_lanes(z[:, c:c + LANES], cm, sa, sb) * scale).astype(q_ref.dtype)
    cmp_ref[0] = z[:, C_KV:C_KV + ROW_WIDTH]
    ks = _rope_lanes(z[:, C_KV + 2 * KV_WIDTH:C_KV + 3 * KV_WIDTH], cm, sa, sb)
    vs = z[:, C_KV + 3 * KV_WIDTH:C_KV + 4 * KV_WIDTH]
    kw = _rope_lanes(z[:, C_KV + 4 * KV_WIDTH:C_KV + 5 * KV_WIDTH], cm, sa, sb)
    vw = z[:, C_KV + 5 * KV_WIDTH:C_KV + 6 * KV_WIDTH]
    sel_ref[0, :, 0:KV_WIDTH] = ks
    sel_ref[0, :, KV_WIDTH:ROW_WIDTH] = vs
    win_ref[0, :, 0:KV_WIDTH] = kw
    win_ref[0, :, KV_WIDTH:ROW_WIDTH] = vw
    g_ref[0] = _sigmoid(z[:, C_G:C_G + LANES])
    if not decode:
        lane = lax.broadcasted_iota(jnp.int32, (rows, LANES), 1)
        pos = pl.program_id(1) * rows + lax.broadcasted_iota(jnp.int32, (rows, LANES), 0)
        first = lane < NSA_HEAD_DIM
        tag = jnp.where(lane - NSA_HEAD_DIM == lax.div(pos, SEL_LEN), 1.0, 0.0)
        for k in range(NSA_KV_HEADS):
            ks_k = ks if k == 0 else pltpu.roll(ks, NSA_HEAD_DIM, 1)
            kw_k = kw if k == 0 else pltpu.roll(kw, NSA_HEAD_DIM, 1)
            selk_ref[0, k] = jnp.where(first, ks_k, tag).astype(selk_ref.dtype)
            wink_ref[0, k] = jnp.where(first, kw_k, 0.0).astype(wink_ref.dtype)
        selv_ref[0] = vs.astype(selv_ref.dtype)
        winv_ref[0] = vw.astype(winv_ref.dtype)

    cb = z[:, C_CONV:C_CONV + CONV_CH]
    v = z[:, C_CONV + CONV_CH:C_CONV + 2 * CONV_CH] * z[:, C_CONV + 2 * CONV_CH:C_CONV + 3 * CONV_CH]
    w0, w1, w2 = cw_ref[0:1, :], cw_ref[1:2, :], cw_ref[2:3, :]
    if decode:
        p0 = cprev_ref[:, 0:CONV_CH]
        p1 = cprev_ref[:, CONV_CH:2 * CONV_CH]
        yc_ref[0] = (cb * (w0 * p0 + w1 * p1 + w2 * v)).astype(yc_ref.dtype)
        cst_ref[:, 0:CONV_CH] = p1
        cst_ref[:, CONV_CH:2 * CONV_CH] = v
    else:
        @pl.when(pl.program_id(1) == 0)
        def _():
            vbuf_ref[0:8, :] = jnp.zeros((8, CONV_CH), F32)

        vbuf_ref[8:8 + rows, :] = v
        conv = w0 * vbuf_ref[6:6 + rows, :] + w1 * vbuf_ref[7:7 + rows, :] + w2 * v
        yc_ref[0] = (cb * conv).astype(yc_ref.dtype)
        last = vbuf_ref[rows:rows + 8, :]
        vbuf_ref[0:8, :] = last
        cst_ref[0] = last[6:8, :]


def in_proj(x, w, tables, conv_w, conv_prev, *, decode):
    nb, t, d = x.shape
    tt = t if decode else min(t, 512)
    grid = (nb, t // tt)
    cm, sa, sb = tables
    row3 = lambda w_: pl.BlockSpec((1, tt, w_), lambda b, i: (b, i, 0))
    tab = pl.BlockSpec((tt, LANES), lambda b, i: (i, 0))
    if decode:
        cprev_spec = pl.BlockSpec((t, 2 * CONV_CH), lambda b, i: (0, 0))
        cst_spec = pl.BlockSpec((t, 2 * CONV_CH), lambda b, i: (0, 0))
        cst_shape = jax.ShapeDtypeStruct((t, 2 * CONV_CH), F32)
    else:
        cprev_spec = pl.BlockSpec((8, LANES), lambda b, i: (0, 0))
        cst_spec = pl.BlockSpec((1, CONV_K - 1, CONV_CH), lambda b, i: (b, 0, 0))
        cst_shape = jax.ShapeDtypeStruct((nb, CONV_K - 1, CONV_CH), F32)
    out_shape = (
        jax.ShapeDtypeStruct((t, nb * S5_WIDTH), F32),
        jax.ShapeDtypeStruct((nb, t, QP_WIDTH), MXU_DTYPE),
        jax.ShapeDtypeStruct((nb, t, ROW_WIDTH), F32),
        jax.ShapeDtypeStruct((nb, t, ROW_WIDTH), F32),
        jax.ShapeDtypeStruct((nb, t, ROW_WIDTH), F32),
        jax.ShapeDtypeStruct((nb, t, LANES), F32),
        jax.ShapeDtypeStruct((nb, t, CONV_CH), MXU_DTYPE),
        cst_shape,
    )
    out_specs = (pl.BlockSpec((tt, S5_WIDTH), lambda b, i: (i, b)), row3(QP_WIDTH), row3(ROW_WIDTH), row3(ROW_WIDTH),
                 row3(ROW_WIDTH), row3(LANES), row3(CONV_CH), cst_spec)
    if not decode:
        assert t <= NSA_HEAD_DIM * SEL_LEN
        per_head = (jax.ShapeDtypeStruct((nb, NSA_KV_HEADS, t, LANES), MXU_DTYPE),
                    pl.BlockSpec((1, NSA_KV_HEADS, tt, LANES), lambda b, i: (b, 0, i, 0)))
        packed = (jax.ShapeDtypeStruct((nb, t, KV_WIDTH), MXU_DTYPE), row3(KV_WIDTH))
        extra = (per_head, packed, per_head, packed)
        out_shape += tuple(e[0] for e in extra)
        out_specs += tuple(e[1] for e in extra)
    return pl.pallas_call(
        functools.partial(_in_proj_kernel, decode=decode),
        grid=grid,
        in_specs=[row3(d), pl.BlockSpec((d, C_END), lambda b, i: (0, 0)), tab, tab, tab,
                  pl.BlockSpec((8, CONV_CH), lambda b, i: (0, 0)), cprev_spec],
        out_specs=out_specs,
        out_shape=out_shape,
        scratch_shapes=[pltpu.VMEM((tt + 8, CONV_CH), F32)],
        compiler_params=_params(("parallel", "arbitrary"), big=True),
        name="in_proj",
    )(x, w, cm, sa, sb, conv_w, conv_prev)


S5_COLS = 512


def _s5_kernel(u_ref, s0r_ref, s0i_ref, ar_ref, ai_ref, br_ref, bi_ref, cr_ref, ci_ref, d_ref, wglu_ref, bglu_ref,
               y_ref, sr_ref, si_ref, xr_ref, xi_ref, *, nb, tc):
    @pl.when(pl.program_id(0) == 0)
    def _():
        sr_ref[...] = s0r_ref[...]
        si_ref[...] = s0i_ref[...]

    u = u_ref[...]
    ub = u.astype(MXU_DTYPE)
    xr_ref[...] = jnp.dot(ub, br_ref[...], preferred_element_type=F32)
    xi_ref[...] = jnp.dot(ub, bi_ref[...], preferred_element_type=F32)

    for c in range(S5_FLAT // S5_COLS):
        cols = slice(c * S5_COLS, (c + 1) * S5_COLS)
        ar = jnp.broadcast_to(ar_ref[:, cols], (nb, S5_COLS))
        ai = jnp.broadcast_to(ai_ref[:, cols], (nb, S5_COLS))

        def step(t, carry):
            sr, si = carry
            rows = pl.ds(pl.multiple_of(t * nb, nb), nb)
            nr = ar * sr - ai * si + xr_ref[rows, cols]
            ni = ar * si + ai * sr + xi_ref[rows, cols]
            xr_ref[rows, cols] = nr
            xi_ref[rows, cols] = ni
            return nr, ni

        sr, si = lax.fori_loop(0, tc, step, (sr_ref[:, cols], si_ref[:, cols]), unroll=min(tc, 8))
        sr_ref[:, cols] = sr
        si_ref[:, cols] = si

    y = _dot(xr_ref[...], cr_ref[...]) - _dot(xi_ref[...], ci_ref[...]) + d_ref[...] * u
    y = _gelu_tanh(y)
    y_ref[...] = (y * _sigmoid(_dot(y, wglu_ref[...]) + bglu_ref[...])).astype(y_ref.dtype)


def s5_branch(u, s0r, s0i, p, *, nb):
    n = u.shape[0]
    t = n // nb
    tc = min(t, 128)
    full = lambda a: pl.BlockSpec(a.shape, lambda i: (0,) * a.ndim)
    args = (u, s0r, s0i, p['a_re'], p['a_im'], p['bb_re'], p['bb_im'], p['c_re'], p['c_im'], p['d'], p['w_glu'], p['b_glu'])
    return pl.pallas_call(
        functools.partial(_s5_kernel, nb=nb, tc=tc),
        grid=(t // tc,),
        in_specs=[pl.BlockSpec((tc * nb, S5_WIDTH), lambda i: (i, 0))] + [full(a) for a in args[1:]],
        out_specs=(pl.BlockSpec((tc * nb, S5_WIDTH), lambda i: (i, 0)),
                   pl.BlockSpec((nb, S5_FLAT), lambda i: (0, 0)),
                   pl.BlockSpec((nb, S5_FLAT), lambda i: (0, 0))),
        out_shape=(jax.ShapeDtypeStruct((n, S5_WIDTH), MXU_DTYPE),
                   jax.ShapeDtypeStruct((nb, S5_FLAT), F32),
                   jax.ShapeDtypeStruct((nb, S5_FLAT), F32)),
        scratch_shapes=[pltpu.VMEM((tc * nb, S5_FLAT), F32), pltpu.VMEM((tc * nb, S5_FLAT), F32)],
        compiler_params=_params(("arbitrary",), big=True),
        name="s5",
    )(*args)


def _s5_params(lam_re, lam_im, log_dt, b_re, b_im, c_re, c_im, d_skip, w_glu, b_glu):
    dt = jnp.exp(log_dt.astype(F32))[:, None]
    lr, li = lam_re.astype(F32), lam_im.astype(F32)
    mag = jnp.exp(lr * dt)
    a_re = mag * jnp.cos(li * dt)
    a_im = mag * jnp.sin(li * dt)
    den = lr * lr + li * li
    r_re = ((a_re - 1.0) * lr + a_im * li) / den
    r_im = (a_im * lr - (a_re - 1.0) * li) / den
    bb_re = r_re[..., None] * b_re - r_im[..., None] * b_im
    bb_im = r_re[..., None] * b_im + r_im[..., None] * b_re
    eye = jnp.eye(S5_GROUPS, dtype=F32)
    blk_in = lambda bb: jnp.einsum('gpi,gh->gihp', bb, eye).reshape(S5_WIDTH, S5_FLAT).astype(MXU_DTYPE)
    blk_out = lambda c: jnp.einsum('gop,gh->gpho', c.astype(F32), eye).reshape(S5_FLAT, S5_WIDTH).astype(MXU_DTYPE)
    return dict(a_re=a_re.reshape(1, S5_FLAT), a_im=a_im.reshape(1, S5_FLAT),
                bb_re=blk_in(bb_re), bb_im=blk_in(bb_im), c_re=blk_out(c_re), c_im=blk_out(c_im),
                d=d_skip.reshape(1, S5_WIDTH).astype(F32), w_glu=w_glu.astype(MXU_DTYPE),
                b_glu=b_glu.reshape(1, S5_WIDTH).astype(F32))


CHUNK_WIDTH = CMP_STRIDE * ROW_WIDTH
HID_WIDTH = 4 * CMP_HIDDEN


def _compress_tail(ha, hb_next, hpe, w2_ref, cm, sa, sb):
    kv = _dot(_gelu_tanh(ha + hb_next + hpe), w2_ref[...])
    return _rope_lanes(kv[:, 0:KV_WIDTH], cm, sa, sb), kv[:, KV_WIDTH:ROW_WIDTH]


def _compress_kernel(x_ref, w1a_ref, w1b_ref, pea_ref, peb_ref, w2_ref, cm_ref, sa_ref, sb_ref,
                     kc_ref, vc_ref, hpe_ref, hb_ref):
    x = x_ref[0].astype(MXU_DTYPE)
    n = x.shape[0]
    ha = jnp.dot(x, w1a_ref[...], preferred_element_type=F32)
    hb_ref[0:n, :] = jnp.dot(x, w1b_ref[...], preferred_element_type=F32)
    hb_ref[n:n + 8, :] = jnp.zeros((8, HID_WIDTH), F32)
    hpe = _dot(pea_ref[...], w1a_ref[...]) + _dot(peb_ref[...], w1b_ref[...])
    hpe_ref[...] = hpe
    kc, vc = _compress_tail(ha, hb_ref[1:n + 1, :], hpe[0:1, :], w2_ref, cm_ref[...], sa_ref[...], sb_ref[...])
    first = lax.broadcasted_iota(jnp.int32, kc.shape, 1) < NSA_HEAD_DIM
    kc_ref[0, 0] = jnp.where(first, kc, 0.0).astype(kc_ref.dtype)
    kc_ref[0, 1] = jnp.where(first, pltpu.roll(kc, NSA_HEAD_DIM, 1), 0.0).astype(kc_ref.dtype)
    vc_ref[0] = vc.astype(vc_ref.dtype)


def compress(rows, cw, tables):
    b, n, _ = rows.shape
    cm, sa, sb = tables
    full = lambda a: pl.BlockSpec(a.shape, lambda i: (0,) * a.ndim)
    args = (rows, cw['w1a'], cw['w1b'], cw['pe_a'], cw['pe_b'], cw['w2'], cm, sa, sb)
    return pl.pallas_call(
        _compress_kernel,
        grid=(b,),
        in_specs=[pl.BlockSpec((1, n, CHUNK_WIDTH), lambda i: (i, 0, 0))] + [full(a) for a in args[1:]],
        out_specs=(pl.BlockSpec((1, NSA_KV_HEADS, n, LANES), lambda i: (i, 0, 0, 0)),
                   pl.BlockSpec((1, n, KV_WIDTH), lambda i: (i, 0, 0)),
                   pl.BlockSpec((8, HID_WIDTH), lambda i: (0, 0))),
        out_shape=(jax.ShapeDtypeStruct((b, NSA_KV_HEADS, n, LANES), MXU_DTYPE),
                   jax.ShapeDtypeStruct((b, n, KV_WIDTH), MXU_DTYPE),
                   jax.ShapeDtypeStruct((8, HID_WIDTH), F32)),
        scratch_shapes=[pltpu.VMEM((n + 8, HID_WIDTH), F32)],
        compiler_params=_params(("arbitrary",), big=True),
        name="compress",
    )(*args)


def _compress_params(phi_k1, phi_k2, phi_v1, phi_v2, pe_k, pe_v):
    wk = phi_k1.reshape(CMP_LEN, NSA_HEAD_DIM, CMP_HIDDEN)
    wv = phi_v1.reshape(CMP_LEN, NSA_HEAD_DIM, CMP_HIDDEN)
    w = jnp.stack([wk, wk, wv, wv], axis=1)
    w1 = jnp.einsum('sjdu,jm->sjdmu', w, jnp.eye(4, dtype=F32)).reshape(CMP_LEN * ROW_WIDTH, HID_WIDTH)
    w2 = jnp.einsum('jud,jm->jumd', jnp.stack([phi_k2, phi_k2, phi_v2, phi_v2]), jnp.eye(4, dtype=F32))
    pe = jnp.concatenate([pe_k, pe_k, pe_v, pe_v], axis=1)
    pad8 = lambda r: jnp.concatenate([r, jnp.zeros((7, CHUNK_WIDTH), F32)], axis=0)

    def pair_weights(w3):
        halves = jnp.stack([w3[:CMP_STRIDE], w3[CMP_STRIDE:]]).reshape(2, CMP_STRIDE // 2, 2, NSA_HEAD_DIM, CMP_HIDDEN)
        full = jnp.einsum('pqidu,hg->qihdpgu', halves, jnp.eye(NSA_KV_HEADS, dtype=F32))
        return full.reshape(CMP_STRIDE // 2, 2 * KV_WIDTH, 2 * NSA_KV_HEADS * CMP_HIDDEN).astype(MXU_DTYPE)

    return dict(wk_pair=pair_weights(wk), wv_pair=pair_weights(wv),
                w1a=w1[:CHUNK_WIDTH].astype(MXU_DTYPE), w1b=w1[CHUNK_WIDTH:].astype(MXU_DTYPE),
                w2=w2.reshape(HID_WIDTH, ROW_WIDTH).astype(MXU_DTYPE),
                pe_a=pad8(pe[:CMP_STRIDE].reshape(1, CHUNK_WIDTH)), pe_b=pad8(pe[CMP_STRIDE:].reshape(1, CHUNK_WIDTH)))


def _cmp_to_sel_map(n_cmp, n_rows, n_sel):
    n_cols = -(-n_sel // LANES) * LANES
    c0 = np.arange(n_rows) * CMP_STRIDE
    s0 = np.arange(n_cols) * SEL_LEN
    m = (c0[:, None] < s0[None, :] + SEL_LEN) & (s0[None, :] < c0[:, None] + CMP_LEN)
    m &= (np.arange(n_rows) < n_cmp)[:, None] & (np.arange(n_cols) < n_sel)[None, :]
    return jnp.asarray(m.astype(np.float32))


def _masked_softmax_rows(s, mask):
    s = jnp.where(mask, s, MASK_NEG)
    m = jnp.max(s, axis=-1, keepdims=True)
    e = jnp.where(mask, jnp.exp(s - m), 0.0)
    return e / jnp.maximum(jnp.sum(e, axis=-1, keepdims=True), 1e-30)


def _flash_update(slot, rows, parts, m_ref, l_ref, acc_ref):
    m_prev = m_ref[slot, rows, :]
    m_new = m_prev
    for s, _ in parts:
        m_new = jnp.maximum(m_new, jnp.max(s, axis=-1, keepdims=True))
    alpha = jnp.exp(m_prev - m_new)
    l_new = alpha * l_ref[slot, rows, :]
    acc_new = alpha * acc_ref[slot, rows, :]
    for s, v in parts:
        p = jnp.exp(s - jnp.concatenate([m_new] * (s.shape[1] // LANES), axis=1))
        l_new = l_new + jnp.sum(p, axis=-1, keepdims=True)
        acc_new = acc_new + _dot(p, v)
    l_ref[slot, rows, :] = l_new
    acc_ref[slot, rows, :] = acc_new
    m_ref[slot, rows, :] = m_new


ROW_CHUNK = 512
SEL, WIN = 0, 1


def _block_bias_t(score_ref, n_sel):
    rows, tq = score_ref.shape
    groups = [score_ref[v * 8:(v + 1) * 8, :] for v in range(rows // 8)]
    rank = [jnp.zeros((8, tq), F32) for _ in groups]
    sub = lax.broadcasted_iota(jnp.int32, (8, tq), 0)
    for j in range(n_sel):
        r = jnp.broadcast_to(score_ref[j:j + 1, :], (8, tq))
        for v, s in enumerate(groups):
            if v * 8 > j:
                ahead = r >= s
            elif v * 8 + 7 <= j:
                ahead = r > s
            else:
                ahead = (r > s) | ((r == s) & (sub + v * 8 > j))
            rank[v] = rank[v] + jnp.where(ahead, 1.0, 0.0)
    top = float(min(SEL_TOP, n_sel))
    return jnp.concatenate([jnp.where(rk < top, 0.0, MASK_NEG) for rk in rank], axis=0)


def _nsa_prompt_kernel(q_ref, g_ref, kc_ref, vc_ref, selk_ref, selv_ref, wink_ref, winv_ref, map_ref, o_ref,
                       sc_ref, qa_ref, oc_ref, m_ref, l_ref, acc_ref, *, tq, nc, n_sel):
    qi = pl.program_id(1)
    tk = tq
    nw = WINDOW // tk
    q0 = qi * tq
    qpos = q0 + lax.broadcasted_iota(jnp.int32, (tq, 1), 0)
    ncp = vc_ref.shape[1]
    n_idx = lax.broadcasted_iota(jnp.int32, (1, ncp), 1)
    cmask = (CMP_STRIDE * n_idx + (CMP_LEN - 1) <= qpos) & (n_idx < nc)
    vc = vc_ref[0]
    nsr = sc_ref.shape[0]
    blk_t = lax.broadcasted_iota(jnp.int32, (nsr, tq), 0)
    cur_t = lax.div(q0 + lax.broadcasted_iota(jnp.int32, (nsr, tq), 1), SEL_LEN)
    valid_t = blk_t <= cur_t
    forced_t = valid_t & ((blk_t == 0) | (blk_t >= cur_t - 1))
    gates = g_ref[0]
    lane_half = lax.div(lax.broadcasted_iota(jnp.int32, (tq, LANES), 1), NSA_HEAD_DIM)
    r_idx = lax.broadcasted_iota(jnp.int32, (tq, tk), 0)
    c_idx = lax.broadcasted_iota(jnp.int32, (tq, tk), 1)
    causal_bias = jnp.where(c_idx <= r_idx, 0.0, MASK_NEG)
    oldest_bias = jnp.where(c_idx > r_idx, 0.0, MASK_NEG)

    for k in range(NSA_KV_HEADS):
        heads = [k * NSA_GQ + gq for gq in range(NSA_GQ)]
        qslice = lambda h: q_ref[0, :, h * LANES:(h + 1) * LANES]
        kc = kc_ref[0, k]

        psum = jnp.zeros((tq, ncp), F32)
        for gq, h in enumerate(heads):
            p = _masked_softmax_rows(_dot_nt(qslice(h), kc), cmask)
            psum = psum + p
            oc_ref[gq] = _dot(p, vc)
        imp = jnp.dot(psum, map_ref[...], precision=lax.Precision.HIGHEST, preferred_element_type=F32)
        sc_ref[...] = jnp.where(forced_t, FORCE_SCORE, jnp.where(valid_t, imp.T[0:nsr, :], -jnp.inf))

        bias_t = jnp.concatenate([jnp.zeros((NSA_HEAD_DIM, tq), F32), _block_bias_t(sc_ref, n_sel)]
                                 + ([jnp.zeros((NSA_HEAD_DIM - nsr, tq), F32)] if nsr < NSA_HEAD_DIM else []), axis=0)
        bias = bias_t.T.astype(MXU_DTYPE)
        for gq, h in enumerate(heads):
            qa_ref[SEL, gq * tq:(gq + 1) * tq, :] = qslice(h) + bias
            qa_ref[WIN, gq * tq:(gq + 1) * tq, :] = qslice(h)
        for branch in (SEL, WIN):
            m_ref[branch] = jnp.full((NSA_GQ * tq, LANES), M_INIT, F32)
            l_ref[branch] = jnp.zeros((NSA_GQ * tq, LANES), F32)
            acc_ref[branch] = jnp.zeros((NSA_GQ * tq, LANES), F32)

        def attend(tiles, k_ref, v_ref, branch):
            kv = []
            for j, bias_tile in tiles:
                k0 = pl.multiple_of(j * tk, tk)
                kv.append((k_ref[0, k, pl.ds(k0, tk), :], v_ref[0, pl.ds(k0, tk), :], bias_tile))
            for r0 in range(0, NSA_GQ * tq, ROW_CHUNK):
                rows = slice(r0, r0 + ROW_CHUNK)
                q = qa_ref[branch, rows, :]
                parts = []
                for kj, vj, bias_tile in kv:
                    s = _dot_nt(q, kj)
                    if bias_tile is not None:
                        s = s + jnp.concatenate([bias_tile] * (ROW_CHUNK // tq), axis=0)
                    parts.append((s, vj))
                _flash_update(branch, rows, parts, m_ref, l_ref, acc_ref)

        lax.fori_loop(0, lax.div(qi, 2),
                      lambda jj, c: (attend([(2 * jj, None), (2 * jj + 1, None)], selk_ref, selv_ref, SEL), c)[1], 0)

        @pl.when(lax.rem(qi, 2) == 1)
        def _():
            attend([(qi - 1, None), (qi, causal_bias)], selk_ref, selv_ref, SEL)

        @pl.when(lax.rem(qi, 2) == 0)
        def _():
            attend([(qi, causal_bias)], selk_ref, selv_ref, SEL)

        win_tiles = []
        for i in range(nw + 1):
            j = qi - nw + i
            edge = oldest_bias if i == 0 else causal_bias if i == nw else jnp.zeros((tq, tk), F32)
            win_tiles.append((jnp.maximum(j, 0), edge if i == nw else jnp.where(j >= 0, edge, MASK_NEG)))
        attend(win_tiles, wink_ref, winv_ref, WIN)

        for gq, h in enumerate(heads):
            rows = slice(gq * tq, (gq + 1) * tq)
            o_s = acc_ref[SEL, rows, :] / jnp.maximum(l_ref[SEL, rows, :], 1e-30)
            o_w = acc_ref[WIN, rows, :] / jnp.maximum(l_ref[WIN, rows, :], 1e-30)
            c = 3 * h
            out = gates[:, c:c + 1] * oc_ref[gq] + gates[:, c + 1:c + 2] * o_s + gates[:, c + 2:c + 3] * o_w
            o_ref[0, :, h * LANES:(h + 1) * LANES] = jnp.where(lane_half == k, out, 0.0).astype(o_ref.dtype)


def nsa_prompt(qp, gates, kc, vc, selk, selv, wink, winv):
    b, t, _ = qp.shape
    tq = min(t, 256)
    assert WINDOW % tq == 0 and t % tq == 0
    ncp = vc.shape[1]
    nc = ncp - 1
    n_sel = -(-t // SEL_LEN)
    nsr = -(-n_sel // 8) * 8
    assert nsr <= NSA_HEAD_DIM
    cmap = _cmp_to_sel_map(nc, ncp, n_sel)
    per_b = lambda a: pl.BlockSpec((1,) + a.shape[1:], lambda i, j: (i,) + (0,) * (a.ndim - 1))
    return pl.pallas_call(
        functools.partial(_nsa_prompt_kernel, tq=tq, nc=nc, n_sel=n_sel),
        grid=(b, t // tq),
        in_specs=[pl.BlockSpec((1, tq, QP_WIDTH), lambda i, j: (i, j, 0)),
                  pl.BlockSpec((1, tq, LANES), lambda i, j: (i, j, 0)),
                  per_b(kc), per_b(vc), per_b(selk), per_b(selv), per_b(wink), per_b(winv),
                  pl.BlockSpec(cmap.shape, lambda i, j: (0, 0))],
        out_specs=pl.BlockSpec((1, tq, QP_WIDTH), lambda i, j: (i, j, 0)),
        out_shape=jax.ShapeDtypeStruct((b, t, QP_WIDTH), MXU_DTYPE),
        scratch_shapes=[pltpu.VMEM((nsr, tq), F32),
                        pltpu.VMEM((2, NSA_GQ * tq, LANES), MXU_DTYPE),
                        pltpu.VMEM((NSA_GQ, tq, LANES), F32),
                        pltpu.VMEM((2, NSA_GQ * tq, LANES), F32),
                        pltpu.VMEM((2, NSA_GQ * tq, LANES), F32),
                        pltpu.VMEM((2, NSA_GQ * tq, LANES), F32)],
        compiler_params=_params(("parallel", "parallel"), big=True),
        name="nsa_prompt",
    )(qp, gates, kc, vc, selk, selv, wink, winv, cmap)


PAGE_CHUNKS = PAGE_SIZE // CMP_STRIDE
PAGE_BLOCKS = PAGE_SIZE // SEL_LEN


def _pages_view(pool):
    depth, n_pool, page = pool.shape[:3]
    return pool.transpose(0, 1, 3, 4, 5, 2).reshape(depth, n_pool, 2, KV_WIDTH, page)


def _paged_hidden_kernel(pt_ref, pool_ref, wk_ref, wv_ref, ha_ref, hb_ref, buf_ref, rows_ref, sem,
                         *, layer, pg, n_steps):
    step = pl.program_id(0) * pl.num_programs(1) + pl.program_id(1)
    slot = lax.rem(step, 2)

    def page_copy(s, sl, r):
        return pltpu.make_async_copy(pool_ref.at[layer, pt_ref[s * pg + r]], buf_ref.at[sl, r], sem.at[sl])

    def start_all(s, sl):
        lax.fori_loop(0, pg, lambda r, c: (page_copy(s, sl, r).start(), c)[1], 0)

    @pl.when(step == 0)
    def _():
        start_all(step, slot)

    @pl.when(step + 1 < n_steps)
    def _():
        start_all(step + 1, 1 - slot)

    lax.fori_loop(0, pg, lambda r, c: (page_copy(step, slot, r).wait(), c)[1], 0)

    def to_rows(r, c):
        r0 = pl.multiple_of(r * PAGE_SIZE, PAGE_SIZE)
        for kv in range(2):
            rows_ref[kv, pl.ds(r0, PAGE_SIZE), :] = buf_ref[slot, r, kv].T
        return c

    lax.fori_loop(0, pg, to_rows, 0)

    n = pg * PAGE_CHUNKS
    for kv, w_ref in enumerate((wk_ref, wv_ref)):
        acc = jnp.zeros((n, 2 * NSA_KV_HEADS * CMP_HIDDEN), F32)
        for q in range(CMP_STRIDE // 2):
            x = jnp.concatenate([rows_ref[kv, pl.ds(2 * q + i, n, stride=CMP_STRIDE), :] for i in range(2)], axis=1)
            acc = acc + _dot(x, w_ref[q])
        half = NSA_KV_HEADS * CMP_HIDDEN
        ha_ref[0, :, kv * half:(kv + 1) * half] = acc[:, 0:half]
        hb_ref[0, :, kv * half:(kv + 1) * half] = acc[:, half:2 * half]


def paged_hidden(pool, page_table, cw, *, layer):
    b, n_pages = page_table.shape
    pg = math.gcd(n_pages, 64)
    pages = _pages_view(pool)
    full = lambda a: pl.BlockSpec(a.shape, lambda i, j, pt: (0,) * a.ndim)
    out = jax.ShapeDtypeStruct((b, n_pages * PAGE_CHUNKS, HID_WIDTH), F32)
    out_spec = pl.BlockSpec((1, pg * PAGE_CHUNKS, HID_WIDTH), lambda i, j, pt: (i, j, 0))
    ws = (cw['wk_pair'], cw['wv_pair'])
    return pl.pallas_call(
        functools.partial(_paged_hidden_kernel, layer=layer, pg=pg, n_steps=b * (n_pages // pg)),
        grid_spec=pltpu.PrefetchScalarGridSpec(
            num_scalar_prefetch=1,
            grid=(b, n_pages // pg),
            in_specs=[pl.BlockSpec(memory_space=pl.ANY)] + [full(a) for a in ws],
            out_specs=(out_spec, out_spec),
            scratch_shapes=[pltpu.VMEM((2, pg, 2, KV_WIDTH, PAGE_SIZE), F32),
                            pltpu.VMEM((2, pg * PAGE_SIZE, KV_WIDTH), F32),
                            pltpu.SemaphoreType.DMA((2,))]),
        out_shape=(out, out),
        compiler_params=_params(("arbitrary", "arbitrary"), big=True),
        name="paged_hidden",
    )(page_table.reshape(-1), pages, *ws)


def _own_half_queries(q8):
    q = q8.astype(F32)
    own_first = lax.div(lax.broadcasted_iota(jnp.int32, q.shape, 0), NSA_GQ) == 0
    return jnp.where(own_first, q, pltpu.roll(q, NSA_HEAD_DIM, 1))


SEL_KEYS = SEL_TOP * PAGE_SIZE
NO_KEY = 2 ** 30


def _decode_cmp_kernel(ha_ref, hb_ref, hpe_ref, new_ref, q_ref, w1b0_ref, w2_ref, cm_ref, sa_ref, sb_ref, map_ref,
                       oc_ref, idx_ref, kpos_ref, hbs_ref, *, q_pos, n_sel):
    n = ha_ref.shape[1]
    hbs_ref[0:n, :] = hb_ref[0]
    new8 = jnp.concatenate([new_ref[0], jnp.zeros((7, ROW_WIDTH), F32)], axis=0)
    hbs_ref[n:n + 8, :] = _dot(new8, w1b0_ref[...])
    kc, vc = _compress_tail(ha_ref[0], hbs_ref[1:n + 1, :], hpe_ref[0:1, :], w2_ref,
                            cm_ref[...], sa_ref[...], sb_ref[...])
    q8 = _own_half_queries(q_ref[0])
    n_idx = lax.broadcasted_iota(jnp.int32, (1, n), 1)
    p = _masked_softmax_rows(_dot_nt(q8, kc), CMP_STRIDE * n_idx + (CMP_LEN - 1) <= q_pos)
    oc_ref[0] = _dot(p, vc)

    row = lax.broadcasted_iota(jnp.int32, p.shape, 0)
    psum = jnp.zeros(p.shape, F32)
    for k in range(NSA_KV_HEADS):
        grp = jnp.sum(jnp.where(lax.div(row, NSA_GQ) == k, p, 0.0), axis=0, keepdims=True)
        psum = jnp.where(row == k, grp, psum)
    imp = jnp.dot(psum, map_ref[...], precision=lax.Precision.HIGHEST, preferred_element_type=F32)
    nsp = imp.shape[1]
    blk = lax.broadcasted_iota(jnp.int32, (NSA_HEADS, nsp), 1)
    cur = q_pos // SEL_LEN
    forced = (blk == 0) | (blk >= cur - 1)
    score = jnp.where(blk < n_sel, jnp.where(forced, FORCE_SCORE, imp), -jnp.inf)
    blk_f = blk.astype(F32)
    lane = lax.broadcasted_iota(jnp.int32, (NSA_HEADS, LANES), 1)
    key = lax.broadcasted_iota(jnp.int32, (NSA_HEADS, SEL_KEYS), 1)
    key_slot = lax.div(key, PAGE_SIZE)
    key_off = key - key_slot * PAGE_SIZE
    idx = jnp.zeros((NSA_HEADS, LANES), jnp.int32)
    kpos = jnp.zeros((NSA_HEADS, SEL_KEYS), jnp.int32)
    for i in range(SEL_TOP):
        top = jnp.max(score, axis=1, keepdims=True)
        pick_f = jnp.min(jnp.where(score == top, blk_f, float(nsp)), axis=1, keepdims=True)
        pick = pick_f.astype(jnp.int32)
        idx = jnp.where(lane == i, pick, idx)
        pos = lax.div(pick, PAGE_BLOCKS) * PAGE_SIZE + key_off
        kpos = jnp.where(key_slot == i, jnp.where(lax.div(pos, SEL_LEN) == pick, pos, NO_KEY), kpos)
        score = jnp.where(blk == pick, -jnp.inf, score)
    idx_ref[0] = idx
    kpos_ref[0] = kpos


def decode_cmp(ha, hb, hpe, cmp_new, q8, cw, tables, cmap, *, q_pos, n_sel):
    b, n, _ = ha.shape
    cm, sa, sb = tables
    w1b0 = cw['w1b'][0:ROW_WIDTH]
    per_b = lambda a: pl.BlockSpec((1,) + a.shape[1:], lambda i: (i, 0, 0))
    full = lambda a: pl.BlockSpec(a.shape, lambda i: (0,) * a.ndim)
    out8 = lambda w_, dt: (jax.ShapeDtypeStruct((b, NSA_HEADS, w_), dt), pl.BlockSpec((1, NSA_HEADS, w_), lambda i: (i, 0, 0)))
    outs = (out8(LANES, F32), out8(LANES, jnp.int32), out8(SEL_KEYS, jnp.int32))
    return pl.pallas_call(
        functools.partial(_decode_cmp_kernel, q_pos=q_pos, n_sel=n_sel),
        grid=(b,),
        in_specs=[per_b(ha), per_b(hb), full(hpe), per_b(cmp_new), per_b(q8), full(w1b0), full(cw['w2']),
                  full(cm), full(sa), full(sb), full(cmap)],
        out_specs=tuple(o[1] for o in outs),
        out_shape=tuple(o[0] for o in outs),
        scratch_shapes=[pltpu.VMEM((n + 8, HID_WIDTH), F32)],
        compiler_params=_params(("parallel",), big=True),
        name="decode_cmp",
    )(ha, hb, hpe, cmp_new, q8, w1b0, cw['w2'], cm, sa, sb, cmap)


def _decode_attn_kernel(idx_ref, pt_ref, pool_ref, win_ref, seln_ref, winn_ref, q_ref, g_ref, oc_ref, kpos_ref,
                        o_ref, wst_ref, k_ref, v_ref, sem, *, layer, n_pages, q_pos):
    b = pl.program_id(0)
    n_past = n_pages * PAGE_BLOCKS
    keep = win_ref.shape[4]
    hd = NSA_HEAD_DIM

    def slot(k, n_):
        blk = idx_ref[(b * NSA_KV_HEADS + k) * SEL_TOP + n_]
        page = pt_ref[b * n_pages + jnp.minimum(lax.div(blk, PAGE_BLOCKS), n_pages - 1)]
        lanes = slice(n_ * PAGE_SIZE, (n_ + 1) * PAGE_SIZE)
        dims = slice(k * hd, (k + 1) * hd)
        return blk, lanes, (pltpu.make_async_copy(pool_ref.at[layer, page, 0, dims, :], k_ref.at[k, :, lanes], sem),
                            pltpu.make_async_copy(pool_ref.at[layer, page, 1, dims, :], v_ref.at[k, :, lanes], sem))

    for k in range(NSA_KV_HEADS):
        for n_ in range(SEL_TOP):
            blk, lanes, copies = slot(k, n_)

            @pl.when(blk < n_past)
            def _():
                for cp in copies:
                    cp.start()

            @pl.when(blk >= n_past)
            def _():
                k_ref[k, :, lanes] = jnp.zeros((hd, PAGE_SIZE), F32)
                v_ref[k, :, lanes] = jnp.zeros((hd, PAGE_SIZE), F32)

    row0 = lax.broadcasted_iota(jnp.int32, (LANES, ROW_WIDTH), 0) == 0
    win_new_t = jnp.where(row0, jnp.broadcast_to(winn_ref[0], (LANES, ROW_WIDTH)), 0.0).T
    last_lane = lax.broadcasted_iota(jnp.int32, (KV_WIDTH, keep), 1) == keep - 1
    state = []
    for kv in range(2):
        shifted = pltpu.roll(win_ref[0, 0, kv], keep - 1, 1)
        state.append(jnp.where(last_lane, win_new_t[kv * KV_WIDTH:(kv + 1) * KV_WIDTH, 0:1], shifted))
        wst_ref[0, kv] = state[kv]

    q8 = q_ref[0]
    qk = q8[:, 0:hd]
    row = lax.broadcasted_iota(jnp.int32, (NSA_HEADS, LANES), 0)
    lane_half = lax.div(lax.broadcasted_iota(jnp.int32, (NSA_HEADS, LANES), 1), hd)
    own_half = lane_half == lax.div(row, NSA_GQ)

    def to_half(o, k):
        z = jnp.zeros_like(o)
        return jnp.concatenate([o, z] if k == 0 else [z, o], axis=1)

    o_w = jnp.zeros((NSA_HEADS, LANES), F32)
    for k in range(NSA_KV_HEADS):
        s = _dot(qk, state[0][k * hd:(k + 1) * hd, :])
        e = jnp.exp(s - jnp.max(s, axis=1, keepdims=True))
        p = e / jnp.maximum(jnp.sum(e, axis=1, keepdims=True), 1e-30)
        o_w = jnp.where(lax.div(row, NSA_GQ) == k, to_half(_dot_nt(p, state[1][k * hd:(k + 1) * hd, :]), k), o_w)

    for k in range(NSA_KV_HEADS):
        for n_ in range(SEL_TOP):
            blk, lanes, copies = slot(k, n_)

            @pl.when(blk < n_past)
            def _():
                for cp in copies:
                    cp.wait()

    new_k = seln_ref[0][:, 0:KV_WIDTH]
    new_v = seln_ref[0][:, KV_WIDTH:ROW_WIDTH]
    s_new = jnp.sum(_own_half_queries(q8) * new_k, axis=1, keepdims=True)
    grp = lax.div(lax.broadcasted_iota(jnp.int32, (NSA_HEADS, 1), 0), NSA_GQ)
    o_s = jnp.zeros((NSA_HEADS, LANES), F32)
    e_new = jnp.zeros((NSA_HEADS, 1), F32)
    denom = jnp.ones((NSA_HEADS, 1), F32)
    for k in range(NSA_KV_HEADS):
        mask = kpos_ref[0, k:k + 1, :] < q_pos
        s = jnp.where(mask, _dot(qk, k_ref[k]), MASK_NEG)
        m = jnp.maximum(jnp.max(s, axis=1, keepdims=True), s_new)
        e = jnp.where(mask, jnp.exp(s - m), 0.0)
        en = jnp.exp(s_new - m)
        o_s = jnp.where(lax.div(row, NSA_GQ) == k, to_half(_dot_nt(e, v_ref[k]), k), o_s)
        e_new = jnp.where(grp == k, en, e_new)
        denom = jnp.where(grp == k, jnp.sum(e, axis=1, keepdims=True) + en, denom)
    o_s = (o_s + e_new * jnp.where(own_half, jnp.broadcast_to(new_v, (NSA_HEADS, LANES)), 0.0)) / denom

    g = g_ref[0]
    out = g[:, 0:1] * oc_ref[0] + g[:, 1:2] * o_s + g[:, 2:3] * o_w
    o_ref[0] = jnp.where(own_half, out, 0.0).astype(o_ref.dtype)


def decode_attn(idx, page_table, pool, win_cache, sel_new, win_new, q8, gates3, oc, kpos, *, layer, q_pos):
    depth = pool.shape[0]
    b, n_pages = page_table.shape
    keep = win_cache.shape[2]
    assert keep == WINDOW
    win = win_cache.transpose(0, 1, 3, 4, 5, 2).reshape(depth, b, 2, KV_WIDTH, keep)
    per_b = lambda a: pl.BlockSpec((1,) + a.shape[1:], lambda i, ix, pt: (i, 0, 0))
    idx_flat = idx[:, 0:NSA_KV_HEADS, 0:SEL_TOP].reshape(-1)
    return pl.pallas_call(
        functools.partial(_decode_attn_kernel, layer=layer, n_pages=n_pages, q_pos=q_pos),
        grid_spec=pltpu.PrefetchScalarGridSpec(
            num_scalar_prefetch=2,
            grid=(b,),
            in_specs=[pl.BlockSpec(memory_space=pl.ANY),
                      pl.BlockSpec((1, 1, 2, KV_WIDTH, keep), lambda i, ix, pt: (layer, i, 0, 0, 0)),
                      per_b(sel_new), per_b(win_new), per_b(q8), per_b(gates3), per_b(oc), per_b(kpos)],
            out_specs=(pl.BlockSpec((1, NSA_HEADS, LANES), lambda i, ix, pt: (i, 0, 0)),
                       pl.BlockSpec((1, 2, KV_WIDTH, keep), lambda i, ix, pt: (i, 0, 0, 0))),
            scratch_shapes=[pltpu.VMEM((NSA_KV_HEADS, NSA_HEAD_DIM, SEL_KEYS), F32),
                            pltpu.VMEM((NSA_KV_HEADS, NSA_HEAD_DIM, SEL_KEYS), F32),
                            pltpu.SemaphoreType.DMA(())]),
        out_shape=(jax.ShapeDtypeStruct((b, NSA_HEADS, LANES), MXU_DTYPE),
                   jax.ShapeDtypeStruct((b, 2, KV_WIDTH, keep), F32)),
        compiler_params=_params(("arbitrary",), big=True),
        name="decode_attn",
    )(idx_flat, page_table.reshape(-1), _pages_view(pool), win, sel_new, win_new, q8, gates3, oc, kpos)


def _merge_ln_kernel(x_ref, ya_ref, yb_ref, yc_ref, wgm_ref, wa_ref, wb_ref, wc_ref, wo_ref, lng_ref, lnb_ref, o_ref):
    x = x_ref[0]
    xb = x.astype(MXU_DTYPE)
    gate = lambda j: _sigmoid(jnp.dot(xb, wgm_ref[:, j * D_MODEL:(j + 1) * D_MODEL], preferred_element_type=F32))
    merged = (gate(0) * _dot(ya_ref[...], wa_ref[...]) + gate(1) * _dot(yb_ref[0], wb_ref[...])
              + gate(2) * _dot(yc_ref[0], wc_ref[...]))
    o_ref[0] = _layer_norm(DN_ALPHA * x + _dot(merged, wo_ref[...]), lng_ref[...], lnb_ref[...])


def merge_ln(x, ya, yb, yc, mw, lng, lnb):
    nb, t, d = x.shape
    tt = min(t, 256)
    row3 = lambda w_: pl.BlockSpec((1, tt, w_), lambda b, i: (b, i, 0))
    full = lambda a: pl.BlockSpec(a.shape, lambda b, i: (0,) * a.ndim)
    ws = (mw['w_gm'], mw['w_a'], mw['w_b'], mw['w_c'], mw['w_o'], lng, lnb)
    return pl.pallas_call(
        _merge_ln_kernel,
        grid=(nb, t // tt),
        in_specs=[row3(d), pl.BlockSpec((tt, S5_WIDTH), lambda b, i: (i, b)), row3(QP_WIDTH), row3(CONV_CH)]
                 + [full(a) for a in ws],
        out_specs=row3(d),
        out_shape=jax.ShapeDtypeStruct((nb, t, d), F32),
        compiler_params=_params(("parallel", "parallel"), big=True),
        name="merge_ln",
    )(x, ya, yb, yc, *ws)


def _layer_weights(w, l):
    bf = lambda a: a.astype(MXU_DTYPE)
    row = lambda a: a.reshape(1, -1).astype(F32)
    w_in = w['w_in'][l]
    w_u, w_q, w_kv, w_gn, w_conv, w_gm = jnp.split(w_in, IN_OFFSETS, axis=-1)
    head_half = (jnp.arange(NSA_HEADS) // NSA_GQ)[None, :, None]

    def pad_heads(a3, own_half):
        z = jnp.zeros_like(a3)
        lo = jnp.concatenate([a3, z], axis=-1)
        hi = jnp.concatenate([z, a3], axis=-1)
        return (jnp.where(head_half == 0, lo, hi) if own_half else lo).reshape(a3.shape[0], QP_WIDTH)

    w_qp = pad_heads(w_q.reshape(D_MODEL, NSA_HEADS, NSA_HEAD_DIM), False)
    w_gpad = jnp.concatenate([w_gn, jnp.zeros((D_MODEL, LANES - w_gn.shape[1]), F32)], axis=1)
    w_proj = bf(jnp.concatenate([w_u, w_qp, w_kv, w_gpad, w_conv], axis=1))
    assert w_proj.shape[1] == C_END
    w_b = pad_heads(w['w_proj_b'][l].T.reshape(D_MODEL, NSA_HEADS, NSA_HEAD_DIM), True).T
    conv_w = jnp.concatenate([w['conv_w'][l], jnp.zeros((8 - CONV_K, CONV_CH), F32)], axis=0)
    gu = lambda name: jnp.split(w[name][l], 2, axis=-1)
    return dict(
        ffn1=(bf(gu('ffn1_w_gu')[0]), bf(gu('ffn1_w_gu')[1]), bf(w['ffn1_w_down'][l])),
        ffn2=(bf(gu('ffn2_w_gu')[0]), bf(gu('ffn2_w_gu')[1]), bf(w['ffn2_w_down'][l])),
        ln=[(row(w['ln_g'][l, j]), row(w['ln_b'][l, j])) for j in range(3)],
        w_proj=w_proj, conv_w=conv_w,
        s5=_s5_params(w['s5_lambda_re'][l], w['s5_lambda_im'][l], w['s5_log_dt'][l], w['s5_b_re'][l], w['s5_b_im'][l],
                      w['s5_c_re'][l], w['s5_c_im'][l], w['s5_d'][l], w['s5_w_glu'][l], w['s5_b_glu'][l]),
        cmp=_compress_params(w['nsa_phi_k1'][l], w['nsa_phi_k2'][l], w['nsa_phi_v1'][l], w['nsa_phi_v2'][l],
                             w['nsa_pe_k'][l], w['nsa_pe_v'][l]),
        merge=dict(w_gm=bf(w_gm), w_a=bf(w['w_proj_a'][l]), w_b=bf(w_b), w_c=bf(w['w_proj_c'][l]), w_o=bf(w['w_o'][l])),
    )


def _prompt_trunk(x, layers):
    b, t, d = x.shape
    n_chunks = t // CMP_STRIDE
    tok_tables = _rope_tables(jnp.arange(t))
    cmp_tables = _rope_tables(jnp.arange(n_chunks) * CMP_STRIDE + (CMP_LEN - 1))
    zero_state = jnp.zeros((b, S5_FLAT), F32)
    keep = min(WINDOW, t)
    states, hpes = [], []
    x2 = x.reshape(b * t, d)
    for lw in layers:
        x2 = ffn_ln(x2, *lw['ffn1'], *lw['ln'][0])
        u, qp, cmp_rows, sel_rows, win_rows, gates, yc, conv_state, selk, selv, wink, winv = in_proj(
            x2.reshape(b, t, d), lw['w_proj'], tok_tables, lw['conv_w'], jnp.zeros((8, LANES), F32), decode=False)
        ya, s_re, s_im = s5_branch(u.reshape(t * b, S5_WIDTH), zero_state, zero_state, lw['s5'], nb=b)
        kc, vc, hpe = compress(cmp_rows.reshape(b, n_chunks, CHUNK_WIDTH), lw['cmp'], cmp_tables)
        hpes.append(hpe)
        yb = nsa_prompt(qp, gates, kc, vc, selk, selv, wink, winv)
        x3 = merge_ln(x2.reshape(b, t, d), ya.reshape(t, b * S5_WIDTH), yb, yc, lw['merge'], *lw['ln'][1])
        x2 = ffn_ln(x3.reshape(b * t, d), *lw['ffn2'], *lw['ln'][2])
        kv_shape = (2, NSA_KV_HEADS, NSA_HEAD_DIM)
        states.append((cmp_rows.reshape((b, t) + kv_shape), sel_rows.reshape((b, t) + kv_shape),
                       win_rows[:, t - keep:].reshape((b, keep) + kv_shape), conv_state,
                       s_re.reshape(b, S5_GROUPS, S5_STATE), s_im.reshape(b, S5_GROUPS, S5_STATE)))
    return x2.reshape(b, t, d), [jnp.stack(s, axis=0) for s in zip(*states)], hpes


def _sample_trunk(x, layers, hpes, cache_cmp, cache_sel, cache_win, cache_conv, s5_re, s5_im, page_table):
    bs, dec_seq, d = x.shape
    assert dec_seq == 1
    n_pages = page_table.shape[1]
    q_pos = n_pages * PAGE_SIZE
    n_blocks = n_pages * PAGE_CHUNKS
    n_sel = q_pos // SEL_LEN + 1
    assert cache_win.shape[2] == min(WINDOW, q_pos) and n_sel >= SEL_TOP
    tok_tables = _rope_tables(jnp.full((bs,), q_pos))
    cmp_tables = _rope_tables(jnp.arange(n_blocks) * CMP_STRIDE + (CMP_LEN - 1))
    cmap = _cmp_to_sel_map(n_blocks, n_blocks, n_sel)
    kv_shape = (2, NSA_KV_HEADS, NSA_HEAD_DIM)
    states = []
    x2 = x.reshape(bs, d)
    for l, lw in enumerate(layers):
        x2 = ffn_ln(x2, *lw['ffn1'], *lw['ln'][0])
        u, qp, cmp_new, sel_new, win_new, gates, yc, conv_state = in_proj(
            x2[None], lw['w_proj'], tok_tables, lw['conv_w'], cache_conv[l].reshape(bs, (CONV_K - 1) * CONV_CH),
            decode=True)
        ya, s_re, s_im = s5_branch(u, s5_re[l].reshape(bs, S5_FLAT), s5_im[l].reshape(bs, S5_FLAT), lw['s5'], nb=bs)
        ha, hb = paged_hidden(cache_cmp, page_table, lw['cmp'], layer=l)
        hpe = hpes[l]
        q8 = qp.reshape(bs, NSA_HEADS, LANES)
        per_seq = lambda a: a.reshape(bs, 1, ROW_WIDTH)
        oc, idx, kpos = decode_cmp(ha, hb, hpe, per_seq(cmp_new), q8, lw['cmp'], cmp_tables, cmap,
                                   q_pos=q_pos, n_sel=n_sel)
        g3 = gates[0, :, 0:3 * NSA_HEADS].reshape(bs, NSA_HEADS, 3)
        g3 = jnp.concatenate([g3, jnp.zeros((bs, NSA_HEADS, LANES - 3), F32)], axis=-1)
        yb, win_state = decode_attn(idx, page_table, cache_sel, cache_win, per_seq(sel_new), per_seq(win_new), q8, g3,
                                    oc, kpos, layer=l, q_pos=q_pos)
        x3 = merge_ln(x2[None], ya, yb.reshape(1, bs, QP_WIDTH), yc, lw['merge'], *lw['ln'][1])
        x2 = ffn_ln(x3[0], *lw['ffn2'], *lw['ln'][2])
        states.append((cmp_new.reshape((bs, 1) + kv_shape), sel_new.reshape((bs, 1) + kv_shape),
                       win_state.reshape((bs,) + kv_shape + (-1,)).transpose(0, 4, 1, 2, 3),
                       conv_state.reshape(bs, CONV_K - 1, CONV_CH),
                       s_re.reshape(bs, S5_GROUPS, S5_STATE), s_im.reshape(bs, S5_GROUPS, S5_STATE)))
    return x2.reshape(bs, 1, d), [jnp.stack(s, axis=0) for s in zip(*states)]


def kernel(x_prompt, x_sample, cache_cmp_kv, cache_sel_kv, cache_win_kv, cache_conv, state_s5_re, state_s5_im, page_table, ln_g, ln_b, ffn1_w_gu, ffn1_w_down, ffn2_w_gu, ffn2_w_down, w_in, s5_lambda_re, s5_lambda_im, s5_log_dt, s5_b_re, s5_b_im, s5_c_re, s5_c_im, s5_d, s5_w_glu, s5_b_glu, nsa_pe_k, nsa_pe_v, nsa_phi_k1, nsa_phi_k2, nsa_phi_v1, nsa_phi_v2, conv_w, w_proj_a, w_proj_b, w_proj_c, w_o):
    w = dict(ln_g=ln_g, ln_b=ln_b, ffn1_w_gu=ffn1_w_gu, ffn1_w_down=ffn1_w_down, ffn2_w_gu=ffn2_w_gu,
             ffn2_w_down=ffn2_w_down, w_in=w_in, s5_lambda_re=s5_lambda_re, s5_lambda_im=s5_lambda_im,
             s5_log_dt=s5_log_dt, s5_b_re=s5_b_re, s5_b_im=s5_b_im, s5_c_re=s5_c_re, s5_c_im=s5_c_im, s5_d=s5_d,
             s5_w_glu=s5_w_glu, s5_b_glu=s5_b_glu, nsa_pe_k=nsa_pe_k, nsa_pe_v=nsa_pe_v, nsa_phi_k1=nsa_phi_k1,
             nsa_phi_k2=nsa_phi_k2, nsa_phi_v1=nsa_phi_v1, nsa_phi_v2=nsa_phi_v2, conv_w=conv_w,
             w_proj_a=w_proj_a, w_proj_b=w_proj_b, w_proj_c=w_proj_c, w_o=w_o)
    layers = [_layer_weights(w, l) for l in range(DEPTH)]
    y_prompt, (p_cmp, p_sel, p_win, p_conv, p_re, p_im), hpes = _prompt_trunk(x_prompt, layers)
    y_sample, (s_cmp, s_sel, s_win, s_conv, s_re, s_im) = _sample_trunk(
        x_sample, layers, hpes, cache_cmp_kv, cache_sel_kv, cache_win_kv, cache_conv, state_s5_re, state_s5_im, page_table)
    return (y_prompt, y_sample, p_cmp, s_cmp, p_sel, s_sel, p_win, s_win, p_conv, s_conv, p_re, s_re, p_im, s_im)
```

```python
import functools
import math

import numpy as np
import jax
import jax.numpy as jnp
from jax import lax
from jax.experimental import pallas as pl
from jax.experimental.pallas import tpu as pltpu

F32 = jnp.float32
MXU_DTYPE = jnp.bfloat16

D_MODEL = 1024
DEPTH = 2
PAGE_SIZE = 128
DN_ALPHA = (2.0 * DEPTH) ** 0.25
LN_EPS = 1e-5
D_FF = 2816
FFN_RES = 0.5
S5_WIDTH = 512
S5_GROUP = 16
S5_GROUPS = S5_WIDTH // S5_GROUP
S5_STATE = 64
S5_FLAT = S5_GROUPS * S5_STATE
NSA_HEADS = 8
NSA_KV_HEADS = 2
NSA_HEAD_DIM = 64
NSA_GQ = NSA_HEADS // NSA_KV_HEADS
CMP_LEN = 32
CMP_STRIDE = 16
CMP_HIDDEN = 128
SEL_LEN = 64
SEL_TOP = 16
WINDOW = 512
ROPE_THETA = 500000.0
ROT_DIM = NSA_HEAD_DIM // 4
ROT_HALF = ROT_DIM // 2
FORCE_SCORE = 1e9
CONV_CH = 512
CONV_K = 3
N_BRANCH = 3
KV_WIDTH = NSA_KV_HEADS * NSA_HEAD_DIM
ROW_WIDTH = 2 * KV_WIDTH
IN_WIDTHS = (S5_WIDTH, NSA_HEADS * NSA_HEAD_DIM, 6 * KV_WIDTH, 3 * NSA_HEADS, 3 * CONV_CH, N_BRANCH * D_MODEL)
IN_OFFSETS = tuple(int(o) for o in np.cumsum(IN_WIDTHS)[:-1])

LANES = 128
V7X_VMEM_BYTES = 64 * 1024 * 1024
VMEM_LIMIT = (V7X_VMEM_BYTES * 7) // 8

QP_WIDTH = NSA_HEADS * LANES
MASK_NEG = -1e30
M_INIT = -1e29

C_U = 0
C_Q = C_U + S5_WIDTH
C_KV = C_Q + QP_WIDTH
C_G = C_KV + 6 * KV_WIDTH
C_CONV = C_G + LANES
C_END = C_CONV + 3 * CONV_CH


def _sigmoid(x):
    return 1.0 / (1.0 + jnp.exp(-x))


def _gelu_tanh(x):
    return 0.5 * x * (1.0 + jnp.tanh(math.sqrt(2.0 / math.pi) * (x + 0.044715 * (x * x * x))))


def _layer_norm(x, g, b):
    mu = jnp.mean(x, axis=-1, keepdims=True)
    xc = x - mu
    var = jnp.mean(xc * xc, axis=-1, keepdims=True)
    return xc * lax.rsqrt(var + LN_EPS) * g + b


def _dot(a, b):
    return jnp.dot(a.astype(MXU_DTYPE), b.astype(MXU_DTYPE), preferred_element_type=F32)


def _dot_nt(a, b):
    return lax.dot_general(a.astype(MXU_DTYPE), b.astype(MXU_DTYPE), (((1,), (1,)), ((), ())),
                           preferred_element_type=F32)


def _rope_lanes(x, cm, sa, sb):
    return x * cm + pltpu.roll(x, LANES - ROT_HALF, 1) * sa + pltpu.roll(x, ROT_HALF, 1) * sb


def _rope_tables(pos):
    inv_freq = ROPE_THETA ** (-jnp.arange(ROT_HALF, dtype=F32) / ROT_HALF)
    ang = pos.astype(F32)[:, None] * inv_freq
    cos, sin = jnp.cos(ang), jnp.sin(ang)
    n = pos.shape[0]
    rest = NSA_HEAD_DIM - ROT_DIM
    cm = jnp.concatenate([cos, cos, jnp.ones((n, rest), F32)], axis=1)
    sa = jnp.concatenate([-sin, jnp.zeros((n, ROT_HALF + rest), F32)], axis=1)
    sb = jnp.concatenate([jnp.zeros((n, ROT_HALF), F32), sin, jnp.zeros((n, rest), F32)], axis=1)
    return tuple(jnp.tile(t, (1, LANES // NSA_HEAD_DIM)) for t in (cm, sa, sb))


def _params(sem, big=False):
    return pltpu.CompilerParams(dimension_semantics=sem, vmem_limit_bytes=VMEM_LIMIT if big else None)


FF_CHUNK = D_FF


def _ffn_ln_kernel(x_ref, wg_ref, wu_ref, wd_ref, lng_ref, lnb_ref, o_ref):
    x = x_ref[...]
    xb = x.astype(MXU_DTYPE)
    acc = jnp.zeros(x.shape, F32)
    for c in range(0, D_FF, FF_CHUNK):
        gate = jnp.dot(xb, wg_ref[:, c:c + FF_CHUNK], preferred_element_type=F32)
        up = jnp.dot(xb, wu_ref[:, c:c + FF_CHUNK], preferred_element_type=F32)
        acc = acc + _dot(gate * _sigmoid(gate) * up, wd_ref[c:c + FF_CHUNK, :])
    o_ref[...] = _layer_norm(DN_ALPHA * x + FFN_RES * acc, lng_ref[...], lnb_ref[...])


def ffn_ln(x, wg, wu, wd, lng, lnb):
    n, d = x.shape
    tm = min(n, 512)
    resident = lambda a: pl.BlockSpec(a.shape, lambda i: (0, 0), pipeline_mode=pl.Buffered(1))
    return pl.pallas_call(
        _ffn_ln_kernel,
        grid=(n // tm,),
        in_specs=[pl.BlockSpec((tm, d), lambda i: (i, 0)), resident(wg), resident(wu), resident(wd),
                  resident(lng), resident(lnb)],
        out_specs=pl.BlockSpec((tm, d), lambda i: (i, 0)),
        out_shape=jax.ShapeDtypeStruct((n, d), F32),
        compiler_params=_params(("parallel",), big=True),
        name="ffn_ln",
    )(x, wg, wu, wd, lng, lnb)


def _in_proj_kernel(x_ref, w_ref, cm_ref, sa_ref, sb_ref, cw_ref, cprev_ref,
                    u_ref, q_ref, cmp_ref, sel_ref, win_ref, g_ref, yc_ref, cst_ref, *rest, decode):
    if decode:
        (vbuf_ref,) = rest
    else:
        selk_ref, selv_ref, wink_ref, winv_ref, cmpt_ref, vbuf_ref = rest
    x = x_ref[0]
    rows = x.shape[0]
    z = _dot(x, w_ref[...])
    cm, sa, sb = cm_ref[...], sa_ref[...], sb_ref[...]

    u_ref[...] = z[:, C_U:C_U + S5_WIDTH]
    scale = NSA_HEAD_DIM ** -0.5
    for h in range(NSA_HEADS):
        c = C_Q + h * LANES
        q_ref[0, :, h * LANES:(h + 1) * LANES] = (_rope_lanes(z[:, c:c + LANES], cm, sa, sb) * scale).astype(q_ref.dtype)
    cmp_ref[0] = z[:, C_KV:C_KV + ROW_WIDTH]
    ks = _rope_lanes(z[:, C_KV + 2 * KV_WIDTH:C_KV + 3 * KV_WIDTH], cm, sa, sb)
    vs = z[:, C_KV + 3 * KV_WIDTH:C_KV + 4 * KV_WIDTH]
    kw = _rope_lanes(z[:, C_KV + 4 * KV_WIDTH:C_KV + 5 * KV_WIDTH], cm, sa, sb)
    vw = z[:, C_KV + 5 * KV_WIDTH:C_KV + 6 * KV_WIDTH]
    g_ref[0] = _sigmoid(z[:, C_G:C_G + LANES])
    if decode:
        sel_ref[0, :, 0:KV_WIDTH] = ks
        sel_ref[0, :, KV_WIDTH:ROW_WIDTH] = vs
        win_ref[0, :, 0:KV_WIDTH] = kw
        win_ref[0, :, KV_WIDTH:ROW_WIDTH] = vw
    else:
        for kv, (s_kv, w_kv) in enumerate(((ks, kw), (vs, vw))):
            cmpt_ref[0, kv] = z[:, C_KV + kv * KV_WIDTH:C_KV + (kv + 1) * KV_WIDTH].T
            sel_ref[0, kv] = s_kv.T
            win_ref[0, kv] = w_kv.T
        lane = lax.broadcasted_iota(jnp.int32, (rows, LANES), 1)
        pos = pl.program_id(1) * rows + lax.broadcasted_iota(jnp.int32, (rows, LANES), 0)
        first = lane < NSA_HEAD_DIM
        tag = jnp.where(lane - NSA_HEAD_DIM == lax.div(pos, SEL_LEN), 1.0, 0.0)
        for k in range(NSA_KV_HEADS):
            ks_k = ks if k == 0 else pltpu.roll(ks, NSA_HEAD_DIM, 1)
            kw_k = kw if k == 0 else pltpu.roll(kw, NSA_HEAD_DIM, 1)
            selk_ref[0, k] = jnp.where(first, ks_k, tag).astype(selk_ref.dtype)
            wink_ref[0, k] = jnp.where(first, kw_k, 0.0).astype(wink_ref.dtype)
        selv_ref[0] = vs.astype(selv_ref.dtype)
        winv_ref[0] = vw.astype(winv_ref.dtype)

    cb = z[:, C_CONV:C_CONV + CONV_CH]
    v = z[:, C_CONV + CONV_CH:C_CONV + 2 * CONV_CH] * z[:, C_CONV + 2 * CONV_CH:C_CONV + 3 * CONV_CH]
    w0, w1, w2 = cw_ref[0:1, :], cw_ref[1:2, :], cw_ref[2:3, :]
    if decode:
        p0 = cprev_ref[:, 0:CONV_CH]
        p1 = cprev_ref[:, CONV_CH:2 * CONV_CH]
        yc_ref[0] = (cb * (w0 * p0 + w1 * p1 + w2 * v)).astype(yc_ref.dtype)
        cst_ref[:, 0:CONV_CH] = p1
        cst_ref[:, CONV_CH:2 * CONV_CH] = v
    else:
        @pl.when(pl.program_id(1) == 0)
        def _():
            vbuf_ref[0:8, :] = jnp.zeros((8, CONV_CH), F32)

        vbuf_ref[8:8 + rows, :] = v
        conv = w0 * vbuf_ref[6:6 + rows, :] + w1 * vbuf_ref[7:7 + rows, :] + w2 * v
        yc_ref[0] = (cb * conv).astype(yc_ref.dtype)
        last = vbuf_ref[rows:rows + 8, :]
        vbuf_ref[0:8, :] = last
        cst_ref[0] = last[6:8, :]


def in_proj(x, w, tables, conv_w, conv_prev, *, decode):
    nb, t, d = x.shape
    tt = t if decode else min(t, 512)
    grid = (nb, t // tt)
    cm, sa, sb = tables
    row3 = lambda w_: pl.BlockSpec((1, tt, w_), lambda b, i: (b, i, 0))
    tab = pl.BlockSpec((tt, LANES), lambda b, i: (i, 0))
    if decode:
        cprev_spec = pl.BlockSpec((t, 2 * CONV_CH), lambda b, i: (0, 0))
        cst_spec = pl.BlockSpec((t, 2 * CONV_CH), lambda b, i: (0, 0))
        cst_shape = jax.ShapeDtypeStruct((t, 2 * CONV_CH), F32)
    else:
        cprev_spec = pl.BlockSpec((8, LANES), lambda b, i: (0, 0))
        cst_spec = pl.BlockSpec((1, CONV_K - 1, CONV_CH), lambda b, i: (b, 0, 0))
        cst_shape = jax.ShapeDtypeStruct((nb, CONV_K - 1, CONV_CH), F32)
    slabs = (jax.ShapeDtypeStruct((nb, 2, KV_WIDTH, t), F32),
             pl.BlockSpec((1, 2, KV_WIDTH, tt), lambda b, i: (b, 0, 0, i)))
    rows_out = (jax.ShapeDtypeStruct((nb, t, ROW_WIDTH), F32), row3(ROW_WIDTH))
    kv_out = rows_out if decode else slabs
    out_shape = (
        jax.ShapeDtypeStruct((t, nb * S5_WIDTH), F32),
        jax.ShapeDtypeStruct((nb, t, QP_WIDTH), MXU_DTYPE),
        rows_out[0],
        kv_out[0], kv_out[0],
        jax.ShapeDtypeStruct((nb, t, LANES), F32),
        jax.ShapeDtypeStruct((nb, t, CONV_CH), MXU_DTYPE),
        cst_shape,
    )
    out_specs = (pl.BlockSpec((tt, S5_WIDTH), lambda b, i: (i, b)), row3(QP_WIDTH), rows_out[1], kv_out[1], kv_out[1],
                 row3(LANES), row3(CONV_CH), cst_spec)
    if not decode:
        assert t <= NSA_HEAD_DIM * SEL_LEN
        per_head = (jax.ShapeDtypeStruct((nb, NSA_KV_HEADS, t, LANES), MXU_DTYPE),
                    pl.BlockSpec((1, NSA_KV_HEADS, tt, LANES), lambda b, i: (b, 0, i, 0)))
        packed = (jax.ShapeDtypeStruct((nb, t, KV_WIDTH), MXU_DTYPE), row3(KV_WIDTH))
        extra = (per_head, packed, per_head, packed, slabs)
        out_shape += tuple(e[0] for e in extra)
        out_specs += tuple(e[1] for e in extra)
    return pl.pallas_call(
        functools.partial(_in_proj_kernel, decode=decode),
        grid=grid,
        in_specs=[row3(d), pl.BlockSpec((d, C_END), lambda b, i: (0, 0)), tab, tab, tab,
                  pl.BlockSpec((8, CONV_CH), lambda b, i: (0, 0)), cprev_spec],
        out_specs=out_specs,
        out_shape=out_shape,
        scratch_shapes=[pltpu.VMEM((tt + 8, CONV_CH), F32)],
        compiler_params=_params(("parallel", "arbitrary"), big=True),
        name="in_proj",
    )(x, w, cm, sa, sb, conv_w, conv_prev)


S5_COLS = 512


def _s5_kernel(u_ref, s0r_ref, s0i_ref, ar_ref, ai_ref, br_ref, bi_ref, cr_ref, ci_ref, d_ref, wglu_ref, bglu_ref,
               y_ref, sr_ref, si_ref, xr_ref, xi_ref, *, nb, tc):
    @pl.when(pl.program_id(0) == 0)
    def _():
        sr_ref[...] = s0r_ref[...]
        si_ref[...] = s0i_ref[...]

    u = u_ref[...]
    ub = u.astype(MXU_DTYPE)
    hw, hf = S5_WIDTH // 2, S5_FLAT // 2
    for h in range(2):
        xr_ref[:, h * hf:(h + 1) * hf] = jnp.dot(ub[:, h * hw:(h + 1) * hw], br_ref[h], preferred_element_type=F32)
        xi_ref[:, h * hf:(h + 1) * hf] = jnp.dot(ub[:, h * hw:(h + 1) * hw], bi_ref[h], preferred_element_type=F32)

    for c in range(S5_FLAT // S5_COLS):
        cols = slice(c * S5_COLS, (c + 1) * S5_COLS)
        ar = jnp.broadcast_to(ar_ref[:, cols], (nb, S5_COLS))
        ai = jnp.broadcast_to(ai_ref[:, cols], (nb, S5_COLS))

        def step(t, carry):
            sr, si = carry
            rows = pl.ds(pl.multiple_of(t * nb, nb), nb)
            nr = ar * sr - ai * si + xr_ref[rows, cols]
            ni = ar * si + ai * sr + xi_ref[rows, cols]
            xr_ref[rows, cols] = nr
            xi_ref[rows, cols] = ni
            return nr, ni

        sr, si = lax.fori_loop(0, tc, step, (sr_ref[:, cols], si_ref[:, cols]), unroll=min(tc, 8))
        sr_ref[:, cols] = sr
        si_ref[:, cols] = si

    y = jnp.concatenate([_dot(xr_ref[:, h * hf:(h + 1) * hf], cr_ref[h]) - _dot(xi_ref[:, h * hf:(h + 1) * hf], ci_ref[h])
                         for h in range(2)], axis=1) + d_ref[...] * u
    y = _gelu_tanh(y)
    y_ref[...] = (y * _sigmoid(_dot(y, wglu_ref[...]) + bglu_ref[...])).astype(y_ref.dtype)


def s5_branch(u, s0r, s0i, p, *, nb):
    n = u.shape[0]
    t = n // nb
    tc = min(t, 128)
    full = lambda a: pl.BlockSpec(a.shape, lambda i: (0,) * a.ndim)
    args = (u, s0r, s0i, p['a_re'], p['a_im'], p['bb_re'], p['bb_im'], p['c_re'], p['c_im'], p['d'], p['w_glu'], p['b_glu'])
    return pl.pallas_call(
        functools.partial(_s5_kernel, nb=nb, tc=tc),
        grid=(t // tc,),
        in_specs=[pl.BlockSpec((tc * nb, S5_WIDTH), lambda i: (i, 0))] + [full(a) for a in args[1:]],
        out_specs=(pl.BlockSpec((tc * nb, S5_WIDTH), lambda i: (i, 0)),
                   pl.BlockSpec((nb, S5_FLAT), lambda i: (0, 0)),
                   pl.BlockSpec((nb, S5_FLAT), lambda i: (0, 0))),
        out_shape=(jax.ShapeDtypeStruct((n, S5_WIDTH), MXU_DTYPE),
                   jax.ShapeDtypeStruct((nb, S5_FLAT), F32),
                   jax.ShapeDtypeStruct((nb, S5_FLAT), F32)),
        scratch_shapes=[pltpu.VMEM((tc * nb, S5_FLAT), F32), pltpu.VMEM((tc * nb, S5_FLAT), F32)],
        compiler_params=_params(("arbitrary",), big=True),
        name="s5",
    )(*args)


def _s5_params(lam_re, lam_im, log_dt, b_re, b_im, c_re, c_im, d_skip, w_glu, b_glu):
    dt = jnp.exp(log_dt.astype(F32))[:, None]
    lr, li = lam_re.astype(F32), lam_im.astype(F32)
    mag = jnp.exp(lr * dt)
    a_re = mag * jnp.cos(li * dt)
    a_im = mag * jnp.sin(li * dt)
    den = lr * lr + li * li
    r_re = ((a_re - 1.0) * lr + a_im * li) / den
    r_im = (a_im * lr - (a_re - 1.0) * li) / den
    bb_re = r_re[..., None] * b_re - r_im[..., None] * b_im
    bb_im = r_re[..., None] * b_im + r_im[..., None] * b_re
    eye = jnp.eye(S5_GROUPS, dtype=F32)
    hw, hf = S5_WIDTH // 2, S5_FLAT // 2
    halves = lambda m: jnp.stack([m[:m.shape[0] // 2, :m.shape[1] // 2], m[m.shape[0] // 2:, m.shape[1] // 2:]])
    blk_in = lambda bb: halves(jnp.einsum('gpi,gh->gihp', bb, eye).reshape(S5_WIDTH, S5_FLAT)).astype(MXU_DTYPE)
    blk_out = lambda c: halves(jnp.einsum('gop,gh->gpho', c.astype(F32), eye).reshape(S5_FLAT, S5_WIDTH)).astype(MXU_DTYPE)
    return dict(a_re=a_re.reshape(1, S5_FLAT), a_im=a_im.reshape(1, S5_FLAT),
                bb_re=blk_in(bb_re), bb_im=blk_in(bb_im), c_re=blk_out(c_re), c_im=blk_out(c_im),
                d=d_skip.reshape(1, S5_WIDTH).astype(F32), w_glu=w_glu.astype(MXU_DTYPE),
                b_glu=b_glu.reshape(1, S5_WIDTH).astype(F32))


CHUNK_WIDTH = CMP_STRIDE * ROW_WIDTH
HID_WIDTH = 4 * CMP_HIDDEN


def _compress_tail(ha, hb_next, hpe, w2_ref, cm, sa, sb):
    kv = _dot(_gelu_tanh(ha + hb_next + hpe), w2_ref[...])
    return _rope_lanes(kv[:, 0:KV_WIDTH], cm, sa, sb), kv[:, KV_WIDTH:ROW_WIDTH]


def _compress_kernel(x_ref, w1a_ref, w1b_ref, pea_ref, peb_ref, w2_ref, cm_ref, sa_ref, sb_ref,
                     kc_ref, vc_ref, hpe_ref, hb_ref):
    x = x_ref[0].astype(MXU_DTYPE)
    n = x.shape[0]
    ha = jnp.dot(x, w1a_ref[...], preferred_element_type=F32)
    hb_ref[0:n, :] = jnp.dot(x, w1b_ref[...], preferred_element_type=F32)
    hb_ref[n:n + 8, :] = jnp.zeros((8, HID_WIDTH), F32)
    hpe = _dot(pea_ref[...], w1a_ref[...]) + _dot(peb_ref[...], w1b_ref[...])
    hpe_ref[...] = hpe
    kc, vc = _compress_tail(ha, hb_ref[1:n + 1, :], hpe[0:1, :], w2_ref, cm_ref[...], sa_ref[...], sb_ref[...])
    first = lax.broadcasted_iota(jnp.int32, kc.shape, 1) < NSA_HEAD_DIM
    kc_ref[0, 0] = jnp.where(first, kc, 0.0).astype(kc_ref.dtype)
    kc_ref[0, 1] = jnp.where(first, pltpu.roll(kc, NSA_HEAD_DIM, 1), 0.0).astype(kc_ref.dtype)
    vc_ref[0] = vc.astype(vc_ref.dtype)


def compress(rows, cw, tables):
    b, n, _ = rows.shape
    cm, sa, sb = tables
    full = lambda a: pl.BlockSpec(a.shape, lambda i: (0,) * a.ndim)
    args = (rows, cw['w1a'], cw['w1b'], cw['pe_a'], cw['pe_b'], cw['w2'], cm, sa, sb)
    return pl.pallas_call(
        _compress_kernel,
        grid=(b,),
        in_specs=[pl.BlockSpec((1, n, CHUNK_WIDTH), lambda i: (i, 0, 0))] + [full(a) for a in args[1:]],
        out_specs=(pl.BlockSpec((1, NSA_KV_HEADS, n, LANES), lambda i: (i, 0, 0, 0)),
                   pl.BlockSpec((1, n, KV_WIDTH), lambda i: (i, 0, 0)),
                   pl.BlockSpec((8, HID_WIDTH), lambda i: (0, 0))),
        out_shape=(jax.ShapeDtypeStruct((b, NSA_KV_HEADS, n, LANES), MXU_DTYPE),
                   jax.ShapeDtypeStruct((b, n, KV_WIDTH), MXU_DTYPE),
                   jax.ShapeDtypeStruct((8, HID_WIDTH), F32)),
        scratch_shapes=[pltpu.VMEM((n + 8, HID_WIDTH), F32)],
        compiler_params=_params(("arbitrary",), big=True),
        name="compress",
    )(*args)


def _compress_params(phi_k1, phi_k2, phi_v1, phi_v2, pe_k, pe_v):
    wk = phi_k1.reshape(CMP_LEN, NSA_HEAD_DIM, CMP_HIDDEN)
    wv = phi_v1.reshape(CMP_LEN, NSA_HEAD_DIM, CMP_HIDDEN)
    w = jnp.stack([wk, wk, wv, wv], axis=1)
    w1 = jnp.einsum('sjdu,jm->sjdmu', w, jnp.eye(4, dtype=F32)).reshape(CMP_LEN * ROW_WIDTH, HID_WIDTH)
    w2 = jnp.einsum('jud,jm->jumd', jnp.stack([phi_k2, phi_k2, phi_v2, phi_v2]), jnp.eye(4, dtype=F32))
    pe = jnp.concatenate([pe_k, pe_k, pe_v, pe_v], axis=1)
    pad8 = lambda r: jnp.concatenate([r, jnp.zeros((7, CHUNK_WIDTH), F32)], axis=0)

    def pair_weights(w3):
        halves = jnp.stack([w3[:CMP_STRIDE], w3[CMP_STRIDE:]]).reshape(2, CMP_STRIDE // 2, 2, NSA_HEAD_DIM, CMP_HIDDEN)
        full = jnp.einsum('pqidu,hg->qihdpgu', halves, jnp.eye(NSA_KV_HEADS, dtype=F32))
        return full.reshape(CMP_STRIDE // 2, 2 * KV_WIDTH, 2 * NSA_KV_HEADS * CMP_HIDDEN).astype(MXU_DTYPE)

    return dict(wk_pair=pair_weights(wk), wv_pair=pair_weights(wv),
                w1a=w1[:CHUNK_WIDTH].astype(MXU_DTYPE), w1b=w1[CHUNK_WIDTH:].astype(MXU_DTYPE),
                w2=w2.reshape(HID_WIDTH, ROW_WIDTH).astype(MXU_DTYPE),
                pe_a=pad8(pe[:CMP_STRIDE].reshape(1, CHUNK_WIDTH)), pe_b=pad8(pe[CMP_STRIDE:].reshape(1, CHUNK_WIDTH)))


def _cmp_to_sel_map(n_cmp, n_rows, n_sel):
    n_cols = -(-n_sel // LANES) * LANES
    c0 = np.arange(n_rows) * CMP_STRIDE
    s0 = np.arange(n_cols) * SEL_LEN
    m = (c0[:, None] < s0[None, :] + SEL_LEN) & (s0[None, :] < c0[:, None] + CMP_LEN)
    m &= (np.arange(n_rows) < n_cmp)[:, None] & (np.arange(n_cols) < n_sel)[None, :]
    return jnp.asarray(m.astype(np.float32))


def _masked_softmax_rows(s, mask):
    s = jnp.where(mask, s, MASK_NEG)
    m = jnp.max(s, axis=-1, keepdims=True)
    e = jnp.where(mask, jnp.exp(s - m), 0.0)
    return e / jnp.maximum(jnp.sum(e, axis=-1, keepdims=True), 1e-30)


def _flash_update(slot, rows, parts, m_ref, l_ref, acc_ref):
    m_prev = m_ref[slot, rows, :]
    m_new = m_prev
    for s, _ in parts:
        m_new = jnp.maximum(m_new, jnp.max(s, axis=-1, keepdims=True))
    alpha = jnp.exp(m_prev - m_new)
    l_new = alpha * l_ref[slot, rows, :]
    acc_new = alpha * acc_ref[slot, rows, :]
    for s, v in parts:
        p = jnp.exp(s - jnp.concatenate([m_new] * (s.shape[1] // LANES), axis=1))
        l_new = l_new + jnp.sum(p, axis=-1, keepdims=True)
        acc_new = acc_new + _dot(p, v)
    l_ref[slot, rows, :] = l_new
    acc_ref[slot, rows, :] = acc_new
    m_ref[slot, rows, :] = m_new


SEL, WIN = 0, 1


def _block_bias_t(score_ref, n_sel):
    rows, tq = score_ref.shape
    groups = [score_ref[v * 8:(v + 1) * 8, :] for v in range(rows // 8)]
    rank = [jnp.zeros((8, tq), F32) for _ in groups]
    sub = lax.broadcasted_iota(jnp.int32, (8, tq), 0)
    for j in range(n_sel):
        r = jnp.broadcast_to(score_ref[j:j + 1, :], (8, tq))
        for v, s in enumerate(groups):
            if v * 8 > j:
                ahead = r >= s
            elif v * 8 + 7 <= j:
                ahead = r > s
            else:
                ahead = (r > s) | ((r == s) & (sub + v * 8 > j))
            rank[v] = rank[v] + jnp.where(ahead, 1.0, 0.0)
    top = float(min(SEL_TOP, n_sel))
    return jnp.concatenate([jnp.where(rk < top, 0.0, MASK_NEG) for rk in rank], axis=0)


def _nsa_prompt_kernel(q_ref, g_ref, kc_ref, vc_ref, selk_ref, selv_ref, wink_ref, winv_ref, map_ref, o_ref,
                       sc_ref, qa_ref, oc_ref, m_ref, l_ref, acc_ref, *, tq, nc, n_sel):
    qi = pl.program_id(1)
    tk = tq
    nw = WINDOW // tk
    q0 = qi * tq
    qpos = q0 + lax.broadcasted_iota(jnp.int32, (tq, 1), 0)
    ncp = vc_ref.shape[1]
    n_idx = lax.broadcasted_iota(jnp.int32, (1, ncp), 1)
    cmask = (CMP_STRIDE * n_idx + (CMP_LEN - 1) <= qpos) & (n_idx < nc)
    vc = vc_ref[0]
    nsr = sc_ref.shape[0]
    blk_t = lax.broadcasted_iota(jnp.int32, (nsr, tq), 0)
    cur_t = lax.div(q0 + lax.broadcasted_iota(jnp.int32, (nsr, tq), 1), SEL_LEN)
    valid_t = blk_t <= cur_t
    forced_t = valid_t & ((blk_t == 0) | (blk_t >= cur_t - 1))
    gates = g_ref[0]
    lane_half = lax.div(lax.broadcasted_iota(jnp.int32, (tq, LANES), 1), NSA_HEAD_DIM)
    r_idx = lax.broadcasted_iota(jnp.int32, (tq, tk), 0)
    c_idx = lax.broadcasted_iota(jnp.int32, (tq, tk), 1)
    causal_bias = jnp.where(c_idx <= r_idx, 0.0, MASK_NEG)
    oldest_bias = jnp.where(c_idx > r_idx, 0.0, MASK_NEG)

    for k in range(NSA_KV_HEADS):
        heads = [k * NSA_GQ + gq for gq in range(NSA_GQ)]
        qslice = lambda h: q_ref[0, :, h * LANES:(h + 1) * LANES]
        kc = kc_ref[0, k]

        psum = jnp.zeros((tq, ncp), F32)
        for gq, h in enumerate(heads):
            p = _masked_softmax_rows(_dot_nt(qslice(h), kc), cmask)
            psum = psum + p
            oc_ref[gq] = _dot(p, vc)
        imp = jnp.dot(psum, map_ref[...], precision=lax.Precision.HIGHEST, preferred_element_type=F32)
        sc_ref[...] = jnp.where(forced_t, FORCE_SCORE, jnp.where(valid_t, imp.T[0:nsr, :], -jnp.inf))

        bias_t = jnp.concatenate([jnp.zeros((NSA_HEAD_DIM, tq), F32), _block_bias_t(sc_ref, n_sel)]
                                 + ([jnp.zeros((NSA_HEAD_DIM - nsr, tq), F32)] if nsr < NSA_HEAD_DIM else []), axis=0)
        bias = bias_t.T.astype(MXU_DTYPE)
        for gq, h in enumerate(heads):
            qa_ref[SEL, gq * tq:(gq + 1) * tq, :] = qslice(h) + bias
            qa_ref[WIN, gq * tq:(gq + 1) * tq, :] = qslice(h)
        for branch in (SEL, WIN):
            m_ref[branch] = jnp.full((NSA_GQ * tq, LANES), M_INIT, F32)
            l_ref[branch] = jnp.zeros((NSA_GQ * tq, LANES), F32)
            acc_ref[branch] = jnp.zeros((NSA_GQ * tq, LANES), F32)

        def attend(tiles, k_ref, v_ref, branch):
            kv = []
            for j, bias_tile in tiles:
                k0 = pl.multiple_of(j * tk, tk)
                kv.append((k_ref[0, k, pl.ds(k0, tk), :], v_ref[0, pl.ds(k0, tk), :], bias_tile))
            for r0 in range(0, NSA_GQ * tq, 2 * tq):
                rows = slice(r0, r0 + 2 * tq)
                q = qa_ref[branch, rows, :]
                parts = []
                for kj, vj, bias_tile in kv:
                    s = _dot_nt(q, kj)
                    parts.append((s if bias_tile is None else s + jnp.concatenate([bias_tile] * 2, axis=0), vj))
                _flash_update(branch, rows, parts, m_ref, l_ref, acc_ref)

        lax.fori_loop(0, lax.div(qi, 2),
                      lambda jj, c: (attend([(2 * jj, None), (2 * jj + 1, None)], selk_ref, selv_ref, SEL), c)[1], 0)

        @pl.when(lax.rem(qi, 2) == 1)
        def _():
            attend([(qi - 1, None), (qi, causal_bias)], selk_ref, selv_ref, SEL)

        @pl.when(lax.rem(qi, 2) == 0)
        def _():
            attend([(qi, causal_bias)], selk_ref, selv_ref, SEL)

        win_tiles = []
        for i in range(nw + 1):
            j = qi - nw + i
            edge = oldest_bias if i == 0 else causal_bias if i == nw else jnp.zeros((tq, tk), F32)
            win_tiles.append((jnp.maximum(j, 0), edge if i == nw else jnp.where(j >= 0, edge, MASK_NEG)))
        attend(win_tiles, wink_ref, winv_ref, WIN)

        for gq, h in enumerate(heads):
            rows = slice(gq * tq, (gq + 1) * tq)
            o_s = acc_ref[SEL, rows, :] / jnp.maximum(l_ref[SEL, rows, :], 1e-30)
            o_w = acc_ref[WIN, rows, :] / jnp.maximum(l_ref[WIN, rows, :], 1e-30)
            c = 3 * h
            out = gates[:, c:c + 1] * oc_ref[gq] + gates[:, c + 1:c + 2] * o_s + gates[:, c + 2:c + 3] * o_w
            o_ref[0, :, h * LANES:(h + 1) * LANES] = jnp.where(lane_half == k, out, 0.0).astype(o_ref.dtype)


def nsa_prompt(qp, gates, kc, vc, selk, selv, wink, winv):
    b, t, _ = qp.shape
    tq = min(t, 256)
    assert WINDOW % tq == 0 and t % tq == 0
    ncp = vc.shape[1]
    nc = ncp - 1
    n_sel = -(-t // SEL_LEN)
    nsr = -(-n_sel // 8) * 8
    assert nsr <= NSA_HEAD_DIM
    cmap = _cmp_to_sel_map(nc, ncp, n_sel)
    per_b = lambda a: pl.BlockSpec((1,) + a.shape[1:], lambda i, j: (i,) + (0,) * (a.ndim - 1))
    return pl.pallas_call(
        functools.partial(_nsa_prompt_kernel, tq=tq, nc=nc, n_sel=n_sel),
        grid=(b, t // tq),
        in_specs=[pl.BlockSpec((1, tq, QP_WIDTH), lambda i, j: (i, j, 0)),
                  pl.BlockSpec((1, tq, LANES), lambda i, j: (i, j, 0)),
                  per_b(kc), per_b(vc), per_b(selk), per_b(selv), per_b(wink), per_b(winv),
                  pl.BlockSpec(cmap.shape, lambda i, j: (0, 0))],
        out_specs=pl.BlockSpec((1, tq, QP_WIDTH), lambda i, j: (i, j, 0)),
        out_shape=jax.ShapeDtypeStruct((b, t, QP_WIDTH), MXU_DTYPE),
        scratch_shapes=[pltpu.VMEM((nsr, tq), F32),
                        pltpu.VMEM((2, NSA_GQ * tq, LANES), MXU_DTYPE),
                        pltpu.VMEM((NSA_GQ, tq, LANES), F32),
                        pltpu.VMEM((2, NSA_GQ * tq, LANES), F32),
                        pltpu.VMEM((2, NSA_GQ * tq, LANES), F32),
                        pltpu.VMEM((2, NSA_GQ * tq, LANES), F32)],
        compiler_params=_params(("parallel", "parallel"), big=True),
        name="nsa_prompt",
    )(qp, gates, kc, vc, selk, selv, wink, winv, cmap)


PAGE_CHUNKS = PAGE_SIZE // CMP_STRIDE
PAGE_BLOCKS = PAGE_SIZE // SEL_LEN


def _pages_view(pool):
    depth, n_pool, page = pool.shape[:3]
    return pool.transpose(0, 1, 3, 4, 5, 2).reshape(depth, n_pool, 2, KV_WIDTH, page)


def _paged_hidden_kernel(pt_ref, pool_ref, wk_ref, wv_ref, ha_ref, hb_ref, buf_ref, rows_ref, sem,
                         *, layer, pg, n_steps):
    step = pl.program_id(0) * pl.num_programs(1) + pl.program_id(1)
    slot = lax.rem(step, 2)

    def page_copy(s, sl, r):
        return pltpu.make_async_copy(pool_ref.at[layer, pt_ref[s * pg + r]], buf_ref.at[sl, r], sem.at[sl])

    def start_all(s, sl):
        lax.fori_loop(0, pg, lambda r, c: (page_copy(s, sl, r).start(), c)[1], 0)

    @pl.when(step == 0)
    def _():
        start_all(step, slot)

    @pl.when(step + 1 < n_steps)
    def _():
        start_all(step + 1, 1 - slot)

    lax.fori_loop(0, pg, lambda r, c: (page_copy(step, slot, r).wait(), c)[1], 0)

    def to_rows(r, c):
        r0 = pl.multiple_of(r * PAGE_SIZE, PAGE_SIZE)
        for kv in range(2):
            rows_ref[kv, pl.ds(r0, PAGE_SIZE), :] = buf_ref[slot, r, kv].T
        return c

    lax.fori_loop(0, pg, to_rows, 0, unroll=8)

    n = pg * PAGE_CHUNKS
    for kv, w_ref in enumerate((wk_ref, wv_ref)):
        acc = jnp.zeros((n, 2 * NSA_KV_HEADS * CMP_HIDDEN), F32)
        for q in range(CMP_STRIDE // 2):
            x = jnp.concatenate([rows_ref[kv, pl.ds(2 * q + i, n, stride=CMP_STRIDE), :] for i in range(2)], axis=1)
            acc = acc + _dot(x, w_ref[q])
        half = NSA_KV_HEADS * CMP_HIDDEN
        ha_ref[0, :, kv * half:(kv + 1) * half] = acc[:, 0:half]
        hb_ref[0, :, kv * half:(kv + 1) * half] = acc[:, half:2 * half]


def paged_hidden(pool, page_table, cw, *, layer):
    b, n_pages = page_table.shape
    pg = math.gcd(n_pages, 64)
    pages = _pages_view(pool)
    full = lambda a: pl.BlockSpec(a.shape, lambda i, j, pt: (0,) * a.ndim)
    out = jax.ShapeDtypeStruct((b, n_pages * PAGE_CHUNKS, HID_WIDTH), F32)
    out_spec = pl.BlockSpec((1, pg * PAGE_CHUNKS, HID_WIDTH), lambda i, j, pt: (i, j, 0))
    ws = (cw['wk_pair'], cw['wv_pair'])
    return pl.pallas_call(
        functools.partial(_paged_hidden_kernel, layer=layer, pg=pg, n_steps=b * (n_pages // pg)),
        grid_spec=pltpu.PrefetchScalarGridSpec(
            num_scalar_prefetch=1,
            grid=(b, n_pages // pg),
            in_specs=[pl.BlockSpec(memory_space=pl.ANY)] + [full(a) for a in ws],
            out_specs=(out_spec, out_spec),
            scratch_shapes=[pltpu.VMEM((2, pg, 2, KV_WIDTH, PAGE_SIZE), F32),
                            pltpu.VMEM((2, pg * PAGE_SIZE, KV_WIDTH), F32),
                            pltpu.SemaphoreType.DMA((2,))]),
        out_shape=(out, out),
        compiler_params=_params(("arbitrary", "arbitrary"), big=True),
        name="paged_hidden",
    )(page_table.reshape(-1), pages, *ws)


def _own_half_queries(q8):
    q = q8.astype(F32)
    own_first = lax.div(lax.broadcasted_iota(jnp.int32, q.shape, 0), NSA_GQ) == 0
    return jnp.where(own_first, q, pltpu.roll(q, NSA_HEAD_DIM, 1))


SEL_KEYS = SEL_TOP * PAGE_SIZE
NO_KEY = 2 ** 30


def _decode_cmp_kernel(ha_ref, hb_ref, hpe_ref, new_ref, q_ref, w1b0_ref, w2_ref, cm_ref, sa_ref, sb_ref, map_ref,
                       oc_ref, idx_ref, kpos_ref, hbs_ref, *, q_pos, n_sel):
    n = ha_ref.shape[1]
    hbs_ref[0:n, :] = hb_ref[0]
    new8 = jnp.concatenate([new_ref[0], jnp.zeros((7, ROW_WIDTH), F32)], axis=0)
    hbs_ref[n:n + 8, :] = _dot(new8, w1b0_ref[...])
    kc, vc = _compress_tail(ha_ref[0], hbs_ref[1:n + 1, :], hpe_ref[0:1, :], w2_ref,
                            cm_ref[...], sa_ref[...], sb_ref[...])
    q8 = _own_half_queries(q_ref[0])
    n_idx = lax.broadcasted_iota(jnp.int32, (1, n), 1)
    p = _masked_softmax_rows(_dot_nt(q8, kc), CMP_STRIDE * n_idx + (CMP_LEN - 1) <= q_pos)
    oc_ref[0] = _dot(p, vc)

    row = lax.broadcasted_iota(jnp.int32, p.shape, 0)
    psum = jnp.zeros(p.shape, F32)
    for k in range(NSA_KV_HEADS):
        grp = jnp.sum(jnp.where(lax.div(row, NSA_GQ) == k, p, 0.0), axis=0, keepdims=True)
        psum = jnp.where(row == k, grp, psum)
    imp = jnp.dot(psum, map_ref[...], precision=lax.Precision.HIGHEST, preferred_element_type=F32)
    nsp = imp.shape[1]
    blk = lax.broadcasted_iota(jnp.int32, (NSA_HEADS, nsp), 1)
    cur = q_pos // SEL_LEN
    forced = (blk == 0) | (blk >= cur - 1)
    score = jnp.where(blk < n_sel, jnp.where(forced, FORCE_SCORE, imp), -jnp.inf)
    blk_f = blk.astype(F32)
    lane = lax.broadcasted_iota(jnp.int32, (NSA_HEADS, LANES), 1)
    key = lax.broadcasted_iota(jnp.int32, (NSA_HEADS, SEL_KEYS), 1)
    key_slot = lax.div(key, PAGE_SIZE)
    key_off = key - key_slot * PAGE_SIZE
    idx = jnp.zeros((NSA_HEADS, LANES), jnp.int32)
    kpos = jnp.zeros((NSA_HEADS, SEL_KEYS), jnp.int32)
    for i in range(SEL_TOP):
        top = jnp.max(score, axis=1, keepdims=True)
        pick_f = jnp.min(jnp.where(score == top, blk_f, float(nsp)), axis=1, keepdims=True)
        pick = pick_f.astype(jnp.int32)
        idx = jnp.where(lane == i, pick, idx)
        pos = lax.div(pick, PAGE_BLOCKS) * PAGE_SIZE + key_off
        kpos = jnp.where(key_slot == i, jnp.where(lax.div(pos, SEL_LEN) == pick, pos, NO_KEY), kpos)
        score = jnp.where(blk == pick, -jnp.inf, score)
    idx_ref[0] = idx
    kpos_ref[0] = kpos


def decode_cmp(ha, hb, hpe, cmp_new, q8, cw, tables, cmap, *, q_pos, n_sel):
    b, n, _ = ha.shape
    cm, sa, sb = tables
    w1b0 = cw['w1b'][0:ROW_WIDTH]
    per_b = lambda a: pl.BlockSpec((1,) + a.shape[1:], lambda i: (i, 0, 0))
    full = lambda a: pl.BlockSpec(a.shape, lambda i: (0,) * a.ndim)
    out8 = lambda w_, dt: (jax.ShapeDtypeStruct((b, NSA_HEADS, w_), dt), pl.BlockSpec((1, NSA_HEADS, w_), lambda i: (i, 0, 0)))
    outs = (out8(LANES, F32), out8(LANES, jnp.int32), out8(SEL_KEYS, jnp.int32))
    return pl.pallas_call(
        functools.partial(_decode_cmp_kernel, q_pos=q_pos, n_sel=n_sel),
        grid=(b,),
        in_specs=[per_b(ha), per_b(hb), full(hpe), per_b(cmp_new), per_b(q8), full(w1b0), full(cw['w2']),
                  full(cm), full(sa), full(sb), full(cmap)],
        out_specs=tuple(o[1] for o in outs),
        out_shape=tuple(o[0] for o in outs),
        scratch_shapes=[pltpu.VMEM((n + 8, HID_WIDTH), F32)],
        compiler_params=_params(("parallel",), big=True),
        name="decode_cmp",
    )(ha, hb, hpe, cmp_new, q8, w1b0, cw['w2'], cm, sa, sb, cmap)


def _decode_attn_kernel(idx_ref, pt_ref, pool_ref, win_ref, seln_ref, winn_ref, q_ref, g_ref, oc_ref, kpos_ref,
                        o_ref, wst_ref, k_ref, v_ref, sem, *, layer, n_pages, q_pos):
    b = pl.program_id(0)
    n_past = n_pages * PAGE_BLOCKS
    keep = win_ref.shape[4]
    hd = NSA_HEAD_DIM

    def slot(k, n_):
        blk = idx_ref[(b * NSA_KV_HEADS + k) * SEL_TOP + n_]
        page = pt_ref[b * n_pages + jnp.minimum(lax.div(blk, PAGE_BLOCKS), n_pages - 1)]
        lanes = slice(n_ * PAGE_SIZE, (n_ + 1) * PAGE_SIZE)
        dims = slice(k * hd, (k + 1) * hd)
        return blk, lanes, (pltpu.make_async_copy(pool_ref.at[layer, page, 0, dims, :], k_ref.at[k, :, lanes], sem),
                            pltpu.make_async_copy(pool_ref.at[layer, page, 1, dims, :], v_ref.at[k, :, lanes], sem))

    for k in range(NSA_KV_HEADS):
        for n_ in range(SEL_TOP):
            blk, lanes, copies = slot(k, n_)

            @pl.when(blk < n_past)
            def _():
                for cp in copies:
                    cp.start()

            @pl.when(blk >= n_past)
            def _():
                k_ref[k, :, lanes] = jnp.zeros((hd, PAGE_SIZE), F32)
                v_ref[k, :, lanes] = jnp.zeros((hd, PAGE_SIZE), F32)

    row0 = lax.broadcasted_iota(jnp.int32, (LANES, ROW_WIDTH), 0) == 0
    win_new_t = jnp.where(row0, jnp.broadcast_to(winn_ref[0], (LANES, ROW_WIDTH)), 0.0).T
    last_lane = lax.broadcasted_iota(jnp.int32, (KV_WIDTH, keep), 1) == keep - 1
    state = []
    for kv in range(2):
        shifted = pltpu.roll(win_ref[0, 0, kv], keep - 1, 1)
        state.append(jnp.where(last_lane, win_new_t[kv * KV_WIDTH:(kv + 1) * KV_WIDTH, 0:1], shifted))
        wst_ref[0, kv] = state[kv]

    q8 = q_ref[0]
    qk = q8[:, 0:hd]
    row = lax.broadcasted_iota(jnp.int32, (NSA_HEADS, LANES), 0)
    lane_half = lax.div(lax.broadcasted_iota(jnp.int32, (NSA_HEADS, LANES), 1), hd)
    own_half = lane_half == lax.div(row, NSA_GQ)

    def to_half(o, k):
        z = jnp.zeros_like(o)
        return jnp.concatenate([o, z] if k == 0 else [z, o], axis=1)

    o_w = jnp.zeros((NSA_HEADS, LANES), F32)
    for k in range(NSA_KV_HEADS):
        s = _dot(qk, state[0][k * hd:(k + 1) * hd, :])
        e = jnp.exp(s - jnp.max(s, axis=1, keepdims=True))
        p = e / jnp.maximum(jnp.sum(e, axis=1, keepdims=True), 1e-30)
        o_w = jnp.where(lax.div(row, NSA_GQ) == k, to_half(_dot_nt(p, state[1][k * hd:(k + 1) * hd, :]), k), o_w)

    for k in range(NSA_KV_HEADS):
        for n_ in range(SEL_TOP):
            blk, lanes, copies = slot(k, n_)

            @pl.when(blk < n_past)
            def _():
                for cp in copies:
                    cp.wait()

    new_k = seln_ref[0][:, 0:KV_WIDTH]
    new_v = seln_ref[0][:, KV_WIDTH:ROW_WIDTH]
    s_new = jnp.sum(_own_half_queries(q8) * new_k, axis=1, keepdims=True)
    grp = lax.div(lax.broadcasted_iota(jnp.int32, (NSA_HEADS, 1), 0), NSA_GQ)
    o_s = jnp.zeros((NSA_HEADS, LANES), F32)
    e_new = jnp.zeros((NSA_HEADS, 1), F32)
    denom = jnp.ones((NSA_HEADS, 1), F32)
    for k in range(NSA_KV_HEADS):
        mask = kpos_ref[0, k:k + 1, :] < q_pos
        s = jnp.where(mask, _dot(qk, k_ref[k]), MASK_NEG)
        m = jnp.maximum(jnp.max(s, axis=1, keepdims=True), s_new)
        e = jnp.where(mask, jnp.exp(s - m), 0.0)
        en = jnp.exp(s_new - m)
        o_s = jnp.where(lax.div(row, NSA_GQ) == k, to_half(_dot_nt(e, v_ref[k]), k), o_s)
        e_new = jnp.where(grp == k, en, e_new)
        denom = jnp.where(grp == k, jnp.sum(e, axis=1, keepdims=True) + en, denom)
    o_s = (o_s + e_new * jnp.where(own_half, jnp.broadcast_to(new_v, (NSA_HEADS, LANES)), 0.0)) / denom

    g = g_ref[0]
    out = g[:, 0:1] * oc_ref[0] + g[:, 1:2] * o_s + g[:, 2:3] * o_w
    o_ref[0] = jnp.where(own_half, out, 0.0).astype(o_ref.dtype)


def decode_attn(idx, page_table, pool, win_cache, sel_new, win_new, q8, gates3, oc, kpos, *, layer, q_pos):
    depth = pool.shape[0]
    b, n_pages = page_table.shape
    keep = win_cache.shape[2]
    assert keep == WINDOW
    win = win_cache.transpose(0, 1, 3, 4, 5, 2).reshape(depth, b, 2, KV_WIDTH, keep)
    per_b = lambda a: pl.BlockSpec((1,) + a.shape[1:], lambda i, ix, pt: (i, 0, 0))
    idx_flat = idx[:, 0:NSA_KV_HEADS, 0:SEL_TOP].reshape(-1)
    return pl.pallas_call(
        functools.partial(_decode_attn_kernel, layer=layer, n_pages=n_pages, q_pos=q_pos),
        grid_spec=pltpu.PrefetchScalarGridSpec(
            num_scalar_prefetch=2,
            grid=(b,),
            in_specs=[pl.BlockSpec(memory_space=pl.ANY),
                      pl.BlockSpec((1, 1, 2, KV_WIDTH, keep), lambda i, ix, pt: (layer, i, 0, 0, 0)),
                      per_b(sel_new), per_b(win_new), per_b(q8), per_b(gates3), per_b(oc), per_b(kpos)],
            out_specs=(pl.BlockSpec((1, NSA_HEADS, LANES), lambda i, ix, pt: (i, 0, 0)),
                       pl.BlockSpec((1, 2, KV_WIDTH, keep), lambda i, ix, pt: (i, 0, 0, 0))),
            scratch_shapes=[pltpu.VMEM((NSA_KV_HEADS, NSA_HEAD_DIM, SEL_KEYS), F32),
                            pltpu.VMEM((NSA_KV_HEADS, NSA_HEAD_DIM, SEL_KEYS), F32),
                            pltpu.SemaphoreType.DMA(())]),
        out_shape=(jax.ShapeDtypeStruct((b, NSA_HEADS, LANES), MXU_DTYPE),
                   jax.ShapeDtypeStruct((b, 2, KV_WIDTH, keep), F32)),
        compiler_params=_params(("arbitrary",), big=True),
        name="decode_attn",
    )(idx_flat, page_table.reshape(-1), _pages_view(pool), win, sel_new, win_new, q8, gates3, oc, kpos)


def _merge_ln_kernel(x_ref, ya_ref, yb_ref, yc_ref, wgm_ref, wa_ref, wb_ref, wc_ref, wo_ref, lng_ref, lnb_ref, o_ref):
    x = x_ref[0]
    xb = x.astype(MXU_DTYPE)
    gate = lambda j: _sigmoid(jnp.dot(xb, wgm_ref[:, j * D_MODEL:(j + 1) * D_MODEL], preferred_element_type=F32))
    merged = (gate(0) * _dot(ya_ref[...], wa_ref[...]) + gate(1) * _dot(yb_ref[0], wb_ref[...])
              + gate(2) * _dot(yc_ref[0], wc_ref[...]))
    o_ref[0] = _layer_norm(DN_ALPHA * x + _dot(merged, wo_ref[...]), lng_ref[...], lnb_ref[...])


def merge_ln(x, ya, yb, yc, mw, lng, lnb):
    nb, t, d = x.shape
    tt = min(t, 256)
    row3 = lambda w_: pl.BlockSpec((1, tt, w_), lambda b, i: (b, i, 0))
    full = lambda a: pl.BlockSpec(a.shape, lambda b, i: (0,) * a.ndim)
    ws = (mw['w_gm'], mw['w_a'], mw['w_b'], mw['w_c'], mw['w_o'], lng, lnb)
    return pl.pallas_call(
        _merge_ln_kernel,
        grid=(nb, t // tt),
        in_specs=[row3(d), pl.BlockSpec((tt, S5_WIDTH), lambda b, i: (i, b)), row3(QP_WIDTH), row3(CONV_CH)]
                 + [full(a) for a in ws],
        out_specs=row3(d),
        out_shape=jax.ShapeDtypeStruct((nb, t, d), F32),
        compiler_params=_params(("parallel", "parallel"), big=True),
        name="merge_ln",
    )(x, ya, yb, yc, *ws)


def _layer_weights(w, l):
    bf = lambda a: a.astype(MXU_DTYPE)
    row = lambda a: a.reshape(1, -1).astype(F32)
    w_in = w['w_in'][l]
    w_u, w_q, w_kv, w_gn, w_conv, w_gm = jnp.split(w_in, IN_OFFSETS, axis=-1)
    head_half = (jnp.arange(NSA_HEADS) // NSA_GQ)[None, :, None]

    def pad_heads(a3, own_half):
        z = jnp.zeros_like(a3)
        lo = jnp.concatenate([a3, z], axis=-1)
        hi = jnp.concatenate([z, a3], axis=-1)
        return (jnp.where(head_half == 0, lo, hi) if own_half else lo).reshape(a3.shape[0], QP_WIDTH)

    w_qp = pad_heads(w_q.reshape(D_MODEL, NSA_HEADS, NSA_HEAD_DIM), False)
    w_gpad = jnp.concatenate([w_gn, jnp.zeros((D_MODEL, LANES - w_gn.shape[1]), F32)], axis=1)
    w_proj = bf(jnp.concatenate([w_u, w_qp, w_kv, w_gpad, w_conv], axis=1))
    assert w_proj.shape[1] == C_END
    w_b = pad_heads(w['w_proj_b'][l].T.reshape(D_MODEL, NSA_HEADS, NSA_HEAD_DIM), True).T
    conv_w = jnp.concatenate([w['conv_w'][l], jnp.zeros((8 - CONV_K, CONV_CH), F32)], axis=0)
    gu = lambda name: jnp.split(w[name][l], 2, axis=-1)
    return dict(
        ffn1=(bf(gu('ffn1_w_gu')[0]), bf(gu('ffn1_w_gu')[1]), bf(w['ffn1_w_down'][l])),
        ffn2=(bf(gu('ffn2_w_gu')[0]), bf(gu('ffn2_w_gu')[1]), bf(w['ffn2_w_down'][l])),
        ln=[(row(w['ln_g'][l, j]), row(w['ln_b'][l, j])) for j in range(3)],
        w_proj=w_proj, conv_w=conv_w,
        s5=_s5_params(w['s5_lambda_re'][l], w['s5_lambda_im'][l], w['s5_log_dt'][l], w['s5_b_re'][l], w['s5_b_im'][l],
                      w['s5_c_re'][l], w['s5_c_im'][l], w['s5_d'][l], w['s5_w_glu'][l], w['s5_b_glu'][l]),
        cmp=_compress_params(w['nsa_phi_k1'][l], w['nsa_phi_k2'][l], w['nsa_phi_v1'][l], w['nsa_phi_v2'][l],
                             w['nsa_pe_k'][l], w['nsa_pe_v'][l]),
        merge=dict(w_gm=bf(w_gm), w_a=bf(w['w_proj_a'][l]), w_b=bf(w_b), w_c=bf(w['w_proj_c'][l]), w_o=bf(w['w_o'][l])),
    )


def _prompt_trunk(x, layers):
    b, t, d = x.shape
    n_chunks = t // CMP_STRIDE
    tok_tables = _rope_tables(jnp.arange(t))
    cmp_tables = _rope_tables(jnp.arange(n_chunks) * CMP_STRIDE + (CMP_LEN - 1))
    zero_state = jnp.zeros((b, S5_FLAT), F32)
    keep = min(WINDOW, t)
    states, hpes = [], []
    x2 = x.reshape(b * t, d)
    for lw in layers:
        x2 = ffn_ln(x2, *lw['ffn1'], *lw['ln'][0])
        u, qp, cmp_rows, sel_t, win_t, gates, yc, conv_state, selk, selv, wink, winv, cmp_t = in_proj(
            x2.reshape(b, t, d), lw['w_proj'], tok_tables, lw['conv_w'], jnp.zeros((8, LANES), F32), decode=False)
        ya, s_re, s_im = s5_branch(u.reshape(t * b, S5_WIDTH), zero_state, zero_state, lw['s5'], nb=b)
        kc, vc, hpe = compress(cmp_rows.reshape(b, n_chunks, CHUNK_WIDTH), lw['cmp'], cmp_tables)
        hpes.append(hpe)
        yb = nsa_prompt(qp, gates, kc, vc, selk, selv, wink, winv)
        x3 = merge_ln(x2.reshape(b, t, d), ya.reshape(t, b * S5_WIDTH), yb, yc, lw['merge'], *lw['ln'][1])
        x2 = ffn_ln(x3.reshape(b * t, d), *lw['ffn2'], *lw['ln'][2])
        rows = lambda a: a.reshape(b, 2, NSA_KV_HEADS, NSA_HEAD_DIM, -1).transpose(0, 4, 1, 2, 3)
        states.append((rows(cmp_t), rows(sel_t), rows(win_t[..., t - keep:]), conv_state,
                       s_re.reshape(b, S5_GROUPS, S5_STATE), s_im.reshape(b, S5_GROUPS, S5_STATE)))
    return x2.reshape(b, t, d), [jnp.stack(s, axis=0) for s in zip(*states)], hpes


def _sample_trunk(x, layers, hpes, cache_cmp, cache_sel, cache_win, cache_conv, s5_re, s5_im, page_table):
    bs, dec_seq, d = x.shape
    assert dec_seq == 1
    n_pages = page_table.shape[1]
    q_pos = n_pages * PAGE_SIZE
    n_blocks = n_pages * PAGE_CHUNKS
    n_sel = q_pos // SEL_LEN + 1
    assert cache_win.shape[2] == min(WINDOW, q_pos) and n_sel >= SEL_TOP
    tok_tables = _rope_tables(jnp.full((bs,), q_pos))
    cmp_tables = _rope_tables(jnp.arange(n_blocks) * CMP_STRIDE + (CMP_LEN - 1))
    cmap = _cmp_to_sel_map(n_blocks, n_blocks, n_sel)
    kv_shape = (2, NSA_KV_HEADS, NSA_HEAD_DIM)
    states = []
    x2 = x.reshape(bs, d)
    for l, lw in enumerate(layers):
        x2 = ffn_ln(x2, *lw['ffn1'], *lw['ln'][0])
        u, qp, cmp_new, sel_new, win_new, gates, yc, conv_state = in_proj(
            x2[None], lw['w_proj'], tok_tables, lw['conv_w'], cache_conv[l].reshape(bs, (CONV_K - 1) * CONV_CH),
            decode=True)
        ya, s_re, s_im = s5_branch(u, s5_re[l].reshape(bs, S5_FLAT), s5_im[l].reshape(bs, S5_FLAT), lw['s5'], nb=bs)
        ha, hb = paged_hidden(cache_cmp, page_table, lw['cmp'], layer=l)
        hpe = hpes[l]
        q8 = qp.reshape(bs, NSA_HEADS, LANES)
        per_seq = lambda a: a.reshape(bs, 1, ROW_WIDTH)
        oc, idx, kpos = decode_cmp(ha, hb, hpe, per_seq(cmp_new), q8, lw['cmp'], cmp_tables, cmap,
                                   q_pos=q_pos, n_sel=n_sel)
        g3 = gates[0, :, 0:3 * NSA_HEADS].reshape(bs, NSA_HEADS, 3)
        g3 = jnp.concatenate([g3, jnp.zeros((bs, NSA_HEADS, LANES - 3), F32)], axis=-1)
        yb, win_state = decode_attn(idx, page_table, cache_sel, cache_win, per_seq(sel_new), per_seq(win_new), q8, g3,
                                    oc, kpos, layer=l, q_pos=q_pos)
        x3 = merge_ln(x2[None], ya, yb.reshape(1, bs, QP_WIDTH), yc, lw['merge'], *lw['ln'][1])
        x2 = ffn_ln(x3[0], *lw['ffn2'], *lw['ln'][2])
        states.append((cmp_new.reshape((bs, 1) + kv_shape), sel_new.reshape((bs, 1) + kv_shape),
                       win_state.reshape((bs,) + kv_shape + (-1,)).transpose(0, 4, 1, 2, 3),
                       conv_state.reshape(bs, CONV_K - 1, CONV_CH),
                       s_re.reshape(bs, S5_GROUPS, S5_STATE), s_im.reshape(bs, S5_GROUPS, S5_STATE)))
    return x2.reshape(bs, 1, d), [jnp.stack(s, axis=0) for s in zip(*states)]


def kernel(x_prompt, x_sample, cache_cmp_kv, cache_sel_kv, cache_win_kv, cache_conv, state_s5_re, state_s5_im, page_table, ln_g, ln_b, ffn1_w_gu, ffn1_w_down, ffn2_w_gu, ffn2_w_down, w_in, s5_lambda_re, s5_lambda_im, s5_log_dt, s5_b_re, s5_b_im, s5_c_re, s5_c_im, s5_d, s5_w_glu, s5_b_glu, nsa_pe_k, nsa_pe_v, nsa_phi_k1, nsa_phi_k2, nsa_phi_v1, nsa_phi_v2, conv_w, w_proj_a, w_proj_b, w_proj_c, w_o):
    w = dict(ln_g=ln_g, ln_b=ln_b, ffn1_w_gu=ffn1_w_gu, ffn1_w_down=ffn1_w_down, ffn2_w_gu=ffn2_w_gu,
             ffn2_w_down=ffn2_w_down, w_in=w_in, s5_lambda_re=s5_lambda_re, s5_lambda_im=s5_lambda_im,
             s5_log_dt=s5_log_dt, s5_b_re=s5_b_re, s5_b_im=s5_b_im, s5_c_re=s5_c_re, s5_c_im=s5_c_im, s5_d=s5_d,
             s5_w_glu=s5_w_glu, s5_b_glu=s5_b_glu, nsa_pe_k=nsa_pe_k, nsa_pe_v=nsa_pe_v, nsa_phi_k1=nsa_phi_k1,
             nsa_phi_k2=nsa_phi_k2, nsa_phi_v1=nsa_phi_v1, nsa_phi_v2=nsa_phi_v2, conv_w=conv_w,
             w_proj_a=w_proj_a, w_proj_b=w_proj_b, w_proj_c=w_proj_c, w_o=w_o)
    layers = [_layer_weights(w, l) for l in range(DEPTH)]
    y_prompt, (p_cmp, p_sel, p_win, p_conv, p_re, p_im), hpes = _prompt_trunk(x_prompt, layers)
    y_sample, (s_cmp, s_sel, s_win, s_conv, s_re, s_im) = _sample_trunk(
        x_sample, layers, hpes, cache_cmp_kv, cache_sel_kv, cache_win_kv, cache_conv, state_s5_re, state_s5_im, page_table)
    return (y_prompt, y_sample, p_cmp, s_cmp, p_sel, s_sel, p_win, s_win, p_conv, s_conv, p_re, s_re, p_im, s_im)
```

```python
import functools
import math

import numpy as np
import jax
import jax.numpy as jnp
from jax import lax
from jax.experimental import pallas as pl
from jax.experimental.pallas import tpu as pltpu

F32 = jnp.float32
MXU_DTYPE = jnp.bfloat16

D_MODEL = 1024
DEPTH = 2
PAGE_SIZE = 128
DN_ALPHA = (2.0 * DEPTH) ** 0.25
LN_EPS = 1e-5
D_FF = 2816
FFN_RES = 0.5
S5_WIDTH = 512
S5_GROUP = 16
S5_GROUPS = S5_WIDTH // S5_GROUP
S5_STATE = 64
S5_FLAT = S5_GROUPS * S5_STATE
NSA_HEADS = 8
NSA_KV_HEADS = 2
NSA_HEAD_DIM = 64
NSA_GQ = NSA_HEADS // NSA_KV_HEADS
CMP_LEN = 32
CMP_STRIDE = 16
CMP_HIDDEN = 128
SEL_LEN = 64
SEL_TOP = 16
WINDOW = 512
ROPE_THETA = 500000.0
ROT_DIM = NSA_HEAD_DIM // 4
ROT_HALF = ROT_DIM // 2
FORCE_SCORE = 1e9
CONV_CH = 512
CONV_K = 3
N_BRANCH = 3
KV_WIDTH = NSA_KV_HEADS * NSA_HEAD_DIM
ROW_WIDTH = 2 * KV_WIDTH
IN_WIDTHS = (S5_WIDTH, NSA_HEADS * NSA_HEAD_DIM, 6 * KV_WIDTH, 3 * NSA_HEADS, 3 * CONV_CH, N_BRANCH * D_MODEL)
IN_OFFSETS = tuple(int(o) for o in np.cumsum(IN_WIDTHS)[:-1])

LANES = 128
V7X_VMEM_BYTES = 64 * 1024 * 1024
VMEM_LIMIT = (V7X_VMEM_BYTES * 7) // 8

QP_WIDTH = NSA_HEADS * LANES
MASK_NEG = -1e30
M_INIT = -1e29

C_U = 0
C_Q = C_U + S5_WIDTH
C_KV = C_Q + QP_WIDTH
C_G = C_KV + 6 * KV_WIDTH
C_CONV = C_G + LANES
C_END = C_CONV + 3 * CONV_CH


def _sigmoid(x):
    return 1.0 / (1.0 + jnp.exp(-x))


def _gelu_tanh(x):
    return 0.5 * x * (1.0 + jnp.tanh(math.sqrt(2.0 / math.pi) * (x + 0.044715 * (x * x * x))))


def _layer_norm(x, g, b):
    mu = jnp.mean(x, axis=-1, keepdims=True)
    xc = x - mu
    var = jnp.mean(xc * xc, axis=-1, keepdims=True)
    return xc * lax.rsqrt(var + LN_EPS) * g + b


def _dot(a, b):
    return jnp.dot(a.astype(MXU_DTYPE), b.astype(MXU_DTYPE), preferred_element_type=F32)


def _dot_nt(a, b):
    return lax.dot_general(a.astype(MXU_DTYPE), b.astype(MXU_DTYPE), (((1,), (1,)), ((), ())),
                           preferred_element_type=F32)


def _rope_lanes(x, cm, sa, sb):
    return x * cm + pltpu.roll(x, LANES - ROT_HALF, 1) * sa + pltpu.roll(x, ROT_HALF, 1) * sb


def _rope_tables(pos):
    inv_freq = ROPE_THETA ** (-jnp.arange(ROT_HALF, dtype=F32) / ROT_HALF)
    ang = pos.astype(F32)[:, None] * inv_freq
    cos, sin = jnp.cos(ang), jnp.sin(ang)
    n = pos.shape[0]
    rest = NSA_HEAD_DIM - ROT_DIM
    cm = jnp.concatenate([cos, cos, jnp.ones((n, rest), F32)], axis=1)
    sa = jnp.concatenate([-sin, jnp.zeros((n, ROT_HALF + rest), F32)], axis=1)
    sb = jnp.concatenate([jnp.zeros((n, ROT_HALF), F32), sin, jnp.zeros((n, rest), F32)], axis=1)
    return tuple(jnp.tile(t, (1, LANES // NSA_HEAD_DIM)) for t in (cm, sa, sb))


def _params(sem, big=False):
    return pltpu.CompilerParams(dimension_semantics=sem, vmem_limit_bytes=VMEM_LIMIT if big else None)


FF_CHUNK = D_FF


def _ffn_ln_kernel(x_ref, wg_ref, wu_ref, wd_ref, lng_ref, lnb_ref, o_ref):
    x = x_ref[...]
    xb = x.astype(MXU_DTYPE)
    acc = jnp.zeros(x.shape, F32)
    for c in range(0, D_FF, FF_CHUNK):
        gate = jnp.dot(xb, wg_ref[:, c:c + FF_CHUNK], preferred_element_type=F32)
        up = jnp.dot(xb, wu_ref[:, c:c + FF_CHUNK], preferred_element_type=F32)
        acc = acc + _dot(gate * _sigmoid(gate) * up, wd_ref[c:c + FF_CHUNK, :])
    o_ref[...] = _layer_norm(DN_ALPHA * x + FFN_RES * acc, lng_ref[...], lnb_ref[...])


def ffn_ln(x, wg, wu, wd, lng, lnb):
    n, d = x.shape
    tm = min(n, 512)
    resident = lambda a: pl.BlockSpec(a.shape, lambda i: (0, 0), pipeline_mode=pl.Buffered(1))
    return pl.pallas_call(
        _ffn_ln_kernel,
        grid=(n // tm,),
        in_specs=[pl.BlockSpec((tm, d), lambda i: (i, 0)), resident(wg), resident(wu), resident(wd),
                  resident(lng), resident(lnb)],
        out_specs=pl.BlockSpec((tm, d), lambda i: (i, 0)),
        out_shape=jax.ShapeDtypeStruct((n, d), F32),
        compiler_params=_params(("parallel",), big=True),
        name="ffn_ln",
    )(x, wg, wu, wd, lng, lnb)


def _in_proj_kernel(x_ref, w_ref, cm_ref, sa_ref, sb_ref, cw_ref, cprev_ref,
                    u_ref, q_ref, cmp_ref, sel_ref, win_ref, g_ref, yc_ref, cst_ref, *rest, decode):
    if decode:
        (vbuf_ref,) = rest
    else:
        selk_ref, selv_ref, wink_ref, winv_ref, cmpt_ref, vbuf_ref = rest
    x = x_ref[0]
    rows = x.shape[0]
    z = _dot(x, w_ref[...])
    cm, sa, sb = cm_ref[...], sa_ref[...], sb_ref[...]

    u_ref[...] = z[:, C_U:C_U + S5_WIDTH]
    scale = NSA_HEAD_DIM ** -0.5
    for h in range(NSA_HEADS):
        c = C_Q + h * LANES
        q_ref[0, :, h * LANES:(h + 1) * LANES] = (_rope_lanes(z[:, c:c + LANES], cm, sa, sb) * scale).astype(q_ref.dtype)
    cmp_ref[0] = z[:, C_KV:C_KV + ROW_WIDTH]
    ks = _rope_lanes(z[:, C_KV + 2 * KV_WIDTH:C_KV + 3 * KV_WIDTH], cm, sa, sb)
    vs = z[:, C_KV + 3 * KV_WIDTH:C_KV + 4 * KV_WIDTH]
    kw = _rope_lanes(z[:, C_KV + 4 * KV_WIDTH:C_KV + 5 * KV_WIDTH], cm, sa, sb)
    vw = z[:, C_KV + 5 * KV_WIDTH:C_KV + 6 * KV_WIDTH]
    g_ref[0] = _sigmoid(z[:, C_G:C_G + LANES])
    if decode:
        sel_ref[0, :, 0:KV_WIDTH] = ks
        sel_ref[0, :, KV_WIDTH:ROW_WIDTH] = vs
        win_ref[0, :, 0:KV_WIDTH] = kw
        win_ref[0, :, KV_WIDTH:ROW_WIDTH] = vw
    else:
        for kv, (s_kv, w_kv) in enumerate(((ks, kw), (vs, vw))):
            cmpt_ref[0, kv] = z[:, C_KV + kv * KV_WIDTH:C_KV + (kv + 1) * KV_WIDTH].T
            sel_ref[0, kv] = s_kv.T
            win_ref[0, kv] = w_kv.T
        lane = lax.broadcasted_iota(jnp.int32, (rows, LANES), 1)
        pos = pl.program_id(1) * rows + lax.broadcasted_iota(jnp.int32, (rows, LANES), 0)
        first = lane < NSA_HEAD_DIM
        tag = jnp.where(lane - NSA_HEAD_DIM == lax.div(pos, SEL_LEN), 1.0, 0.0)
        for k in range(NSA_KV_HEADS):
            ks_k = ks if k == 0 else pltpu.roll(ks, NSA_HEAD_DIM, 1)
            kw_k = kw if k == 0 else pltpu.roll(kw, NSA_HEAD_DIM, 1)
            selk_ref[0, k] = jnp.where(first, ks_k, tag).astype(selk_ref.dtype)
            wink_ref[0, k] = jnp.where(first, kw_k, 0.0).astype(wink_ref.dtype)
        selv_ref[0] = vs.astype(selv_ref.dtype)
        winv_ref[0] = vw.astype(winv_ref.dtype)

    cb = z[:, C_CONV:C_CONV + CONV_CH]
    v = z[:, C_CONV + CONV_CH:C_CONV + 2 * CONV_CH] * z[:, C_CONV + 2 * CONV_CH:C_CONV + 3 * CONV_CH]
    w0, w1, w2 = cw_ref[0:1, :], cw_ref[1:2, :], cw_ref[2:3, :]
    if decode:
        p0 = cprev_ref[:, 0:CONV_CH]
        p1 = cprev_ref[:, CONV_CH:2 * CONV_CH]
        yc_ref[0] = (cb * (w0 * p0 + w1 * p1 + w2 * v)).astype(yc_ref.dtype)
        cst_ref[:, 0:CONV_CH] = p1
        cst_ref[:, CONV_CH:2 * CONV_CH] = v
    else:
        @pl.when(pl.program_id(1) == 0)
        def _():
            vbuf_ref[0:8, :] = jnp.zeros((8, CONV_CH), F32)

        vbuf_ref[8:8 + rows, :] = v
        conv = w0 * vbuf_ref[6:6 + rows, :] + w1 * vbuf_ref[7:7 + rows, :] + w2 * v
        yc_ref[0] = (cb * conv).astype(yc_ref.dtype)
        last = vbuf_ref[rows:rows + 8, :]
        vbuf_ref[0:8, :] = last
        cst_ref[0] = last[6:8, :]


def in_proj(x, w, tables, conv_w, conv_prev, *, decode):
    nb, t, d = x.shape
    tt = t if decode else min(t, 512)
    grid = (nb, t // tt)
    cm, sa, sb = tables
    row3 = lambda w_: pl.BlockSpec((1, tt, w_), lambda b, i: (b, i, 0))
    tab = pl.BlockSpec((tt, LANES), lambda b, i: (i, 0))
    if decode:
        cprev_spec = pl.BlockSpec((t, 2 * CONV_CH), lambda b, i: (0, 0))
        cst_spec = pl.BlockSpec((t, 2 * CONV_CH), lambda b, i: (0, 0))
        cst_shape = jax.ShapeDtypeStruct((t, 2 * CONV_CH), F32)
    else:
        cprev_spec = pl.BlockSpec((8, LANES), lambda b, i: (0, 0))
        cst_spec = pl.BlockSpec((1, CONV_K - 1, CONV_CH), lambda b, i: (b, 0, 0))
        cst_shape = jax.ShapeDtypeStruct((nb, CONV_K - 1, CONV_CH), F32)
    slabs = (jax.ShapeDtypeStruct((nb, 2, KV_WIDTH, t), F32),
             pl.BlockSpec((1, 2, KV_WIDTH, tt), lambda b, i: (b, 0, 0, i)))
    rows_out = (jax.ShapeDtypeStruct((nb, t, ROW_WIDTH), F32), row3(ROW_WIDTH))
    kv_out = rows_out if decode else slabs
    out_shape = (
        jax.ShapeDtypeStruct((t, nb * S5_WIDTH), F32),
        jax.ShapeDtypeStruct((nb, t, QP_WIDTH), MXU_DTYPE),
        rows_out[0],
        kv_out[0], kv_out[0],
        jax.ShapeDtypeStruct((nb, t, LANES), F32),
        jax.ShapeDtypeStruct((nb, t, CONV_CH), MXU_DTYPE),
        cst_shape,
    )
    out_specs = (pl.BlockSpec((tt, S5_WIDTH), lambda b, i: (i, b)), row3(QP_WIDTH), rows_out[1], kv_out[1], kv_out[1],
                 row3(LANES), row3(CONV_CH), cst_spec)
    if not decode:
        assert t <= NSA_HEAD_DIM * SEL_LEN
        per_head = (jax.ShapeDtypeStruct((nb, NSA_KV_HEADS, t, LANES), MXU_DTYPE),
                    pl.BlockSpec((1, NSA_KV_HEADS, tt, LANES), lambda b, i: (b, 0, i, 0)))
        packed = (jax.ShapeDtypeStruct((nb, t, KV_WIDTH), MXU_DTYPE), row3(KV_WIDTH))
        extra = (per_head, packed, per_head, packed, slabs)
        out_shape += tuple(e[0] for e in extra)
        out_specs += tuple(e[1] for e in extra)
    return pl.pallas_call(
        functools.partial(_in_proj_kernel, decode=decode),
        grid=grid,
        in_specs=[row3(d), pl.BlockSpec((d, C_END), lambda b, i: (0, 0)), tab, tab, tab,
                  pl.BlockSpec((8, CONV_CH), lambda b, i: (0, 0)), cprev_spec],
        out_specs=out_specs,
        out_shape=out_shape,
        scratch_shapes=[pltpu.VMEM((tt + 8, CONV_CH), F32)],
        compiler_params=_params(("parallel", "arbitrary"), big=True),
        name="in_proj",
    )(x, w, cm, sa, sb, conv_w, conv_prev)


S5_COLS = 512


def _s5_kernel(u_ref, s0r_ref, s0i_ref, ar_ref, ai_ref, br_ref, bi_ref, cr_ref, ci_ref, d_ref, wglu_ref, bglu_ref,
               y_ref, sr_ref, si_ref, xr_ref, xi_ref, *, nb, tc):
    @pl.when(pl.program_id(0) == 0)
    def _():
        sr_ref[...] = s0r_ref[...]
        si_ref[...] = s0i_ref[...]

    u = u_ref[...]
    ub = u.astype(MXU_DTYPE)
    hw, hf = S5_WIDTH // 2, S5_FLAT // 2
    for h in range(2):
        xr_ref[:, h * hf:(h + 1) * hf] = jnp.dot(ub[:, h * hw:(h + 1) * hw], br_ref[h], preferred_element_type=F32)
        xi_ref[:, h * hf:(h + 1) * hf] = jnp.dot(ub[:, h * hw:(h + 1) * hw], bi_ref[h], preferred_element_type=F32)

    for c in range(S5_FLAT // S5_COLS):
        cols = slice(c * S5_COLS, (c + 1) * S5_COLS)
        ar = jnp.broadcast_to(ar_ref[:, cols], (nb, S5_COLS))
        ai = jnp.broadcast_to(ai_ref[:, cols], (nb, S5_COLS))

        def step(t, carry):
            sr, si = carry
            rows = pl.ds(pl.multiple_of(t * nb, nb), nb)
            nr = ar * sr - ai * si + xr_ref[rows, cols]
            ni = ar * si + ai * sr + xi_ref[rows, cols]
            xr_ref[rows, cols] = nr
            xi_ref[rows, cols] = ni
            return nr, ni

        sr, si = lax.fori_loop(0, tc, step, (sr_ref[:, cols], si_ref[:, cols]), unroll=min(tc, 8))
        sr_ref[:, cols] = sr
        si_ref[:, cols] = si

    y = jnp.concatenate([_dot(xr_ref[:, h * hf:(h + 1) * hf], cr_ref[h]) - _dot(xi_ref[:, h * hf:(h + 1) * hf], ci_ref[h])
                         for h in range(2)], axis=1) + d_ref[...] * u
    y = _gelu_tanh(y)
    y_ref[...] = (y * _sigmoid(_dot(y, wglu_ref[...]) + bglu_ref[...])).astype(y_ref.dtype)


def s5_branch(u, s0r, s0i, p, *, nb):
    n = u.shape[0]
    t = n // nb
    tc = min(t, 128)
    full = lambda a: pl.BlockSpec(a.shape, lambda i: (0,) * a.ndim)
    args = (u, s0r, s0i, p['a_re'], p['a_im'], p['bb_re'], p['bb_im'], p['c_re'], p['c_im'], p['d'], p['w_glu'], p['b_glu'])
    return pl.pallas_call(
        functools.partial(_s5_kernel, nb=nb, tc=tc),
        grid=(t // tc,),
        in_specs=[pl.BlockSpec((tc * nb, S5_WIDTH), lambda i: (i, 0))] + [full(a) for a in args[1:]],
        out_specs=(pl.BlockSpec((tc * nb, S5_WIDTH), lambda i: (i, 0)),
                   pl.BlockSpec((nb, S5_FLAT), lambda i: (0, 0)),
                   pl.BlockSpec((nb, S5_FLAT), lambda i: (0, 0))),
        out_shape=(jax.ShapeDtypeStruct((n, S5_WIDTH), MXU_DTYPE),
                   jax.ShapeDtypeStruct((nb, S5_FLAT), F32),
                   jax.ShapeDtypeStruct((nb, S5_FLAT), F32)),
        scratch_shapes=[pltpu.VMEM((tc * nb, S5_FLAT), F32), pltpu.VMEM((tc * nb, S5_FLAT), F32)],
        compiler_params=_params(("arbitrary",), big=True),
        name="s5",
    )(*args)


def _s5_params(lam_re, lam_im, log_dt, b_re, b_im, c_re, c_im, d_skip, w_glu, b_glu):
    dt = jnp.exp(log_dt.astype(F32))[:, None]
    lr, li = lam_re.astype(F32), lam_im.astype(F32)
    mag = jnp.exp(lr * dt)
    a_re = mag * jnp.cos(li * dt)
    a_im = mag * jnp.sin(li * dt)
    den = lr * lr + li * li
    r_re = ((a_re - 1.0) * lr + a_im * li) / den
    r_im = (a_im * lr - (a_re - 1.0) * li) / den
    bb_re = r_re[..., None] * b_re - r_im[..., None] * b_im
    bb_im = r_re[..., None] * b_im + r_im[..., None] * b_re
    eye = jnp.eye(S5_GROUPS, dtype=F32)
    hw, hf = S5_WIDTH // 2, S5_FLAT // 2
    halves = lambda m: jnp.stack([m[:m.shape[0] // 2, :m.shape[1] // 2], m[m.shape[0] // 2:, m.shape[1] // 2:]])
    blk_in = lambda bb: halves(jnp.einsum('gpi,gh->gihp', bb, eye).reshape(S5_WIDTH, S5_FLAT)).astype(MXU_DTYPE)
    blk_out = lambda c: halves(jnp.einsum('gop,gh->gpho', c.astype(F32), eye).reshape(S5_FLAT, S5_WIDTH)).astype(MXU_DTYPE)
    return dict(a_re=a_re.reshape(1, S5_FLAT), a_im=a_im.reshape(1, S5_FLAT),
                bb_re=blk_in(bb_re), bb_im=blk_in(bb_im), c_re=blk_out(c_re), c_im=blk_out(c_im),
                d=d_skip.reshape(1, S5_WIDTH).astype(F32), w_glu=w_glu.astype(MXU_DTYPE),
                b_glu=b_glu.reshape(1, S5_WIDTH).astype(F32))


CHUNK_WIDTH = CMP_STRIDE * ROW_WIDTH
HID_WIDTH = 4 * CMP_HIDDEN


def _compress_tail(ha, hb_next, hpe, w2_ref, cm, sa, sb):
    kv = _dot(_gelu_tanh(ha + hb_next + hpe), w2_ref[...])
    return _rope_lanes(kv[:, 0:KV_WIDTH], cm, sa, sb), kv[:, KV_WIDTH:ROW_WIDTH]


def _compress_kernel(x_ref, w1a_ref, w1b_ref, pea_ref, peb_ref, w2_ref, cm_ref, sa_ref, sb_ref,
                     kc_ref, vc_ref, hpe_ref, hb_ref):
    x = x_ref[0].astype(MXU_DTYPE)
    n = x.shape[0]
    ha = jnp.dot(x, w1a_ref[...], preferred_element_type=F32)
    hb_ref[0:n, :] = jnp.dot(x, w1b_ref[...], preferred_element_type=F32)
    hb_ref[n:n + 8, :] = jnp.zeros((8, HID_WIDTH), F32)
    hpe = _dot(pea_ref[...], w1a_ref[...]) + _dot(peb_ref[...], w1b_ref[...])
    hpe_ref[...] = hpe
    kc, vc = _compress_tail(ha, hb_ref[1:n + 1, :], hpe[0:1, :], w2_ref, cm_ref[...], sa_ref[...], sb_ref[...])
    first = lax.broadcasted_iota(jnp.int32, kc.shape, 1) < NSA_HEAD_DIM
    kc_ref[0, 0] = jnp.where(first, kc, 0.0).astype(kc_ref.dtype)
    kc_ref[0, 1] = jnp.where(first, pltpu.roll(kc, NSA_HEAD_DIM, 1), 0.0).astype(kc_ref.dtype)
    vc_ref[0] = vc.astype(vc_ref.dtype)


def compress(rows, cw, tables):
    b, n, _ = rows.shape
    cm, sa, sb = tables
    full = lambda a: pl.BlockSpec(a.shape, lambda i: (0,) * a.ndim)
    args = (rows, cw['w1a'], cw['w1b'], cw['pe_a'], cw['pe_b'], cw['w2'], cm, sa, sb)
    return pl.pallas_call(
        _compress_kernel,
        grid=(b,),
        in_specs=[pl.BlockSpec((1, n, CHUNK_WIDTH), lambda i: (i, 0, 0))] + [full(a) for a in args[1:]],
        out_specs=(pl.BlockSpec((1, NSA_KV_HEADS, n, LANES), lambda i: (i, 0, 0, 0)),
                   pl.BlockSpec((1, n, KV_WIDTH), lambda i: (i, 0, 0)),
                   pl.BlockSpec((8, HID_WIDTH), lambda i: (0, 0))),
        out_shape=(jax.ShapeDtypeStruct((b, NSA_KV_HEADS, n, LANES), MXU_DTYPE),
                   jax.ShapeDtypeStruct((b, n, KV_WIDTH), MXU_DTYPE),
                   jax.ShapeDtypeStruct((8, HID_WIDTH), F32)),
        scratch_shapes=[pltpu.VMEM((n + 8, HID_WIDTH), F32)],
        compiler_params=_params(("arbitrary",), big=True),
        name="compress",
    )(*args)


def _compress_params(phi_k1, phi_k2, phi_v1, phi_v2, pe_k, pe_v):
    wk = phi_k1.reshape(CMP_LEN, NSA_HEAD_DIM, CMP_HIDDEN)
    wv = phi_v1.reshape(CMP_LEN, NSA_HEAD_DIM, CMP_HIDDEN)
    w = jnp.stack([wk, wk, wv, wv], axis=1)
    w1 = jnp.einsum('sjdu,jm->sjdmu', w, jnp.eye(4, dtype=F32)).reshape(CMP_LEN * ROW_WIDTH, HID_WIDTH)
    w2 = jnp.einsum('jud,jm->jumd', jnp.stack([phi_k2, phi_k2, phi_v2, phi_v2]), jnp.eye(4, dtype=F32))
    pe = jnp.concatenate([pe_k, pe_k, pe_v, pe_v], axis=1)
    pad8 = lambda r: jnp.concatenate([r, jnp.zeros((7, CHUNK_WIDTH), F32)], axis=0)

    def pair_weights(w3):
        halves = jnp.stack([w3[:CMP_STRIDE], w3[CMP_STRIDE:]]).reshape(2, CMP_STRIDE // 2, 2, NSA_HEAD_DIM, CMP_HIDDEN)
        full = jnp.einsum('pqidu,hg->qihdpgu', halves, jnp.eye(NSA_KV_HEADS, dtype=F32))
        return full.reshape(CMP_STRIDE // 2, 2 * KV_WIDTH, 2 * NSA_KV_HEADS * CMP_HIDDEN).astype(MXU_DTYPE)

    return dict(wk_pair=pair_weights(wk), wv_pair=pair_weights(wv),
                w1a=w1[:CHUNK_WIDTH].astype(MXU_DTYPE), w1b=w1[CHUNK_WIDTH:].astype(MXU_DTYPE),
                w2=w2.reshape(HID_WIDTH, ROW_WIDTH).astype(MXU_DTYPE),
                pe_a=pad8(pe[:CMP_STRIDE].reshape(1, CHUNK_WIDTH)), pe_b=pad8(pe[CMP_STRIDE:].reshape(1, CHUNK_WIDTH)))


def _cmp_to_sel_map(n_cmp, n_rows, n_sel):
    n_cols = -(-n_sel // LANES) * LANES
    c0 = np.arange(n_rows) * CMP_STRIDE
    s0 = np.arange(n_cols) * SEL_LEN
    m = (c0[:, None] < s0[None, :] + SEL_LEN) & (s0[None, :] < c0[:, None] + CMP_LEN)
    m &= (np.arange(n_rows) < n_cmp)[:, None] & (np.arange(n_cols) < n_sel)[None, :]
    return jnp.asarray(m.astype(np.float32))


def _masked_softmax_rows(s, mask):
    s = jnp.where(mask, s, MASK_NEG)
    m = jnp.max(s, axis=-1, keepdims=True)
    e = jnp.where(mask, jnp.exp(s - m), 0.0)
    return e / jnp.maximum(jnp.sum(e, axis=-1, keepdims=True), 1e-30)


def _flash_update(slot, rows, parts, m_ref, l_ref, acc_ref):
    m_prev = m_ref[slot, rows, :]
    s_max = functools.reduce(jnp.maximum, [s for s, _ in parts])
    m_new = jnp.maximum(m_prev, jnp.max(s_max, axis=-1, keepdims=True))
    alpha = jnp.exp(m_prev - m_new)
    acc_new = alpha * acc_ref[slot, rows, :]
    p_sum = None
    for s, v in parts:
        p = jnp.exp(s - jnp.concatenate([m_new] * (s.shape[1] // LANES), axis=1))
        p_sum = p if p_sum is None else p_sum + p
        acc_new = acc_new + _dot(p, v)
    l_ref[slot, rows, :] = alpha * l_ref[slot, rows, :] + jnp.sum(p_sum, axis=-1, keepdims=True)
    acc_ref[slot, rows, :] = acc_new
    m_ref[slot, rows, :] = m_new


SEL, WIN = 0, 1


def _block_bias_t(score_ref, n_sel):
    rows, tq = score_ref.shape
    groups = [score_ref[v * 8:(v + 1) * 8, :] for v in range(rows // 8)]
    rank = [jnp.zeros((8, tq), F32) for _ in groups]
    sub = lax.broadcasted_iota(jnp.int32, (8, tq), 0)
    for j in range(n_sel):
        r = jnp.broadcast_to(score_ref[j:j + 1, :], (8, tq))
        for v, s in enumerate(groups):
            if v * 8 > j:
                ahead = r >= s
            elif v * 8 + 7 <= j:
                ahead = r > s
            else:
                ahead = (r > s) | ((r == s) & (sub + v * 8 > j))
            rank[v] = rank[v] + jnp.where(ahead, 1.0, 0.0)
    top = float(min(SEL_TOP, n_sel))
    return jnp.concatenate([jnp.where(rk < top, 0.0, MASK_NEG) for rk in rank], axis=0)


def _nsa_prompt_kernel(q_ref, g_ref, kc_ref, vc_ref, selk_ref, selv_ref, wink_ref, winv_ref, map_ref, o_ref,
                       sc_ref, qa_ref, oc_ref, m_ref, l_ref, acc_ref, *, tq, nc, n_sel):
    qi = pl.program_id(1)
    tk = tq
    nw = WINDOW // tk
    q0 = qi * tq
    qpos = q0 + lax.broadcasted_iota(jnp.int32, (tq, 1), 0)
    ncp = vc_ref.shape[1]
    n_idx = lax.broadcasted_iota(jnp.int32, (1, ncp), 1)
    cmask = (CMP_STRIDE * n_idx + (CMP_LEN - 1) <= qpos) & (n_idx < nc)
    vc = vc_ref[0]
    nsr = sc_ref.shape[0]
    blk_t = lax.broadcasted_iota(jnp.int32, (nsr, tq), 0)
    cur_t = lax.div(q0 + lax.broadcasted_iota(jnp.int32, (nsr, tq), 1), SEL_LEN)
    valid_t = blk_t <= cur_t
    forced_t = valid_t & ((blk_t == 0) | (blk_t >= cur_t - 1))
    gates = g_ref[0]
    lane_half = lax.div(lax.broadcasted_iota(jnp.int32, (tq, LANES), 1), NSA_HEAD_DIM)
    r_idx = lax.broadcasted_iota(jnp.int32, (tq, tk), 0)
    c_idx = lax.broadcasted_iota(jnp.int32, (tq, tk), 1)
    causal_bias = jnp.where(c_idx <= r_idx, 0.0, MASK_NEG)
    oldest_bias = jnp.where(c_idx > r_idx, 0.0, MASK_NEG)

    qslice = lambda h: q_ref[0, :, h * LANES:(h + 1) * LANES]
    group_rows = lambda gq: slice(gq * tq, (gq + 1) * tq)
    slot = lambda k, branch: k * 2 + branch

    for k in range(NSA_KV_HEADS):
        kc = kc_ref[0, k]
        psum = jnp.zeros((tq, ncp), F32)
        for gq in range(NSA_GQ):
            p = _masked_softmax_rows(_dot_nt(qslice(k * NSA_GQ + gq), kc), cmask)
            psum = psum + p
            oc_ref[k * NSA_GQ + gq] = _dot(p, vc)
        imp = jnp.dot(psum, map_ref[...], precision=lax.Precision.HIGHEST, preferred_element_type=F32)
        sc_ref[...] = jnp.where(forced_t, FORCE_SCORE, jnp.where(valid_t, imp.T[0:nsr, :], -jnp.inf))

        bias_t = jnp.concatenate([jnp.zeros((NSA_HEAD_DIM, tq), F32), _block_bias_t(sc_ref, n_sel)]
                                 + ([jnp.zeros((NSA_HEAD_DIM - nsr, tq), F32)] if nsr < NSA_HEAD_DIM else []), axis=0)
        bias = bias_t.T.astype(MXU_DTYPE)
        for gq in range(NSA_GQ):
            qa_ref[slot(k, SEL), group_rows(gq), :] = qslice(k * NSA_GQ + gq) + bias
            qa_ref[slot(k, WIN), group_rows(gq), :] = qslice(k * NSA_GQ + gq)
        for branch in (SEL, WIN):
            m_ref[slot(k, branch)] = jnp.full((NSA_GQ * tq, LANES), M_INIT, F32)
            l_ref[slot(k, branch)] = jnp.zeros((NSA_GQ * tq, LANES), F32)
            acc_ref[slot(k, branch)] = jnp.zeros((NSA_GQ * tq, LANES), F32)

    def attend(tiles, k_ref, v_ref, branch):
        for r0 in range(0, NSA_GQ * tq, 2 * tq):
            rows = slice(r0, r0 + 2 * tq)
            for k in range(NSA_KV_HEADS):
                q = qa_ref[slot(k, branch), rows, :]
                parts = []
                for j, bias_tile in tiles:
                    k0 = pl.multiple_of(j * tk, tk)
                    s = _dot_nt(q, k_ref[0, k, pl.ds(k0, tk), :])
                    if bias_tile is not None:
                        s = s + jnp.concatenate([bias_tile] * 2, axis=0)
                    parts.append((s, v_ref[0, pl.ds(k0, tk), :]))
                _flash_update(slot(k, branch), rows, parts, m_ref, l_ref, acc_ref)

    lax.fori_loop(0, lax.div(qi, 2),
                  lambda jj, c: (attend([(2 * jj, None), (2 * jj + 1, None)], selk_ref, selv_ref, SEL), c)[1], 0)

    @pl.when(lax.rem(qi, 2) == 1)
    def _():
        attend([(qi - 1, None), (qi, causal_bias)], selk_ref, selv_ref, SEL)

    @pl.when(lax.rem(qi, 2) == 0)
    def _():
        attend([(qi, causal_bias)], selk_ref, selv_ref, SEL)

    win_tiles = []
    for i in range(nw + 1):
        j = qi - nw + i
        edge = oldest_bias if i == 0 else causal_bias if i == nw else jnp.zeros((tq, tk), F32)
        win_tiles.append((jnp.maximum(j, 0), edge if i == nw else jnp.where(j >= 0, edge, MASK_NEG)))
    attend(win_tiles, wink_ref, winv_ref, WIN)

    for h in range(NSA_HEADS):
        k, gq = divmod(h, NSA_GQ)
        rows = group_rows(gq)
        o_s = acc_ref[slot(k, SEL), rows, :] / jnp.maximum(l_ref[slot(k, SEL), rows, :], 1e-30)
        o_w = acc_ref[slot(k, WIN), rows, :] / jnp.maximum(l_ref[slot(k, WIN), rows, :], 1e-30)
        c = 3 * h
        out = gates[:, c:c + 1] * oc_ref[h] + gates[:, c + 1:c + 2] * o_s + gates[:, c + 2:c + 3] * o_w
        o_ref[0, :, h * LANES:(h + 1) * LANES] = jnp.where(lane_half == k, out, 0.0).astype(o_ref.dtype)


def nsa_prompt(qp, gates, kc, vc, selk, selv, wink, winv):
    b, t, _ = qp.shape
    tq = min(t, 256)
    assert WINDOW % tq == 0 and t % tq == 0
    ncp = vc.shape[1]
    nc = ncp - 1
    n_sel = -(-t // SEL_LEN)
    nsr = -(-n_sel // 8) * 8
    assert nsr <= NSA_HEAD_DIM
    cmap = _cmp_to_sel_map(nc, ncp, n_sel)
    per_b = lambda a: pl.BlockSpec((1,) + a.shape[1:], lambda i, j: (i,) + (0,) * (a.ndim - 1))
    return pl.pallas_call(
        functools.partial(_nsa_prompt_kernel, tq=tq, nc=nc, n_sel=n_sel),
        grid=(b, t // tq),
        in_specs=[pl.BlockSpec((1, tq, QP_WIDTH), lambda i, j: (i, j, 0)),
                  pl.BlockSpec((1, tq, LANES), lambda i, j: (i, j, 0)),
                  per_b(kc), per_b(vc), per_b(selk), per_b(selv), per_b(wink), per_b(winv),
                  pl.BlockSpec(cmap.shape, lambda i, j: (0, 0))],
        out_specs=pl.BlockSpec((1, tq, QP_WIDTH), lambda i, j: (i, j, 0)),
        out_shape=jax.ShapeDtypeStruct((b, t, QP_WIDTH), MXU_DTYPE),
        scratch_shapes=[pltpu.VMEM((nsr, tq), F32),
                        pltpu.VMEM((2 * NSA_KV_HEADS, NSA_GQ * tq, LANES), MXU_DTYPE),
                        pltpu.VMEM((NSA_HEADS, tq, LANES), F32),
                        pltpu.VMEM((2 * NSA_KV_HEADS, NSA_GQ * tq, LANES), F32),
                        pltpu.VMEM((2 * NSA_KV_HEADS, NSA_GQ * tq, LANES), F32),
                        pltpu.VMEM((2 * NSA_KV_HEADS, NSA_GQ * tq, LANES), F32)],
        compiler_params=_params(("parallel", "parallel"), big=True),
        name="nsa_prompt",
    )(qp, gates, kc, vc, selk, selv, wink, winv, cmap)


PAGE_CHUNKS = PAGE_SIZE // CMP_STRIDE
PAGE_BLOCKS = PAGE_SIZE // SEL_LEN


def _pages_view(pool):
    depth, n_pool, page = pool.shape[:3]
    return pool.transpose(0, 1, 3, 4, 5, 2).reshape(depth, n_pool, 2, KV_WIDTH, page)


def _paged_hidden_kernel(pt_ref, pool_ref, wk_ref, wv_ref, ha_ref, hb_ref, buf_ref, rows_ref, sem,
                         *, layer, pg, n_steps):
    step = pl.program_id(0) * pl.num_programs(1) + pl.program_id(1)
    slot = lax.rem(step, 2)

    def page_copy(s, sl, r):
        return pltpu.make_async_copy(pool_ref.at[layer, pt_ref[s * pg + r]], buf_ref.at[sl, r], sem.at[sl])

    def start_all(s, sl):
        lax.fori_loop(0, pg, lambda r, c: (page_copy(s, sl, r).start(), c)[1], 0)

    @pl.when(step == 0)
    def _():
        start_all(step, slot)

    @pl.when(step + 1 < n_steps)
    def _():
        start_all(step + 1, 1 - slot)

    lax.fori_loop(0, pg, lambda r, c: (page_copy(step, slot, r).wait(), c)[1], 0)

    def to_rows(r, c):
        r0 = pl.multiple_of(r * PAGE_SIZE, PAGE_SIZE)
        for kv in range(2):
            rows_ref[kv, pl.ds(r0, PAGE_SIZE), :] = buf_ref[slot, r, kv].T
        return c

    lax.fori_loop(0, pg, to_rows, 0, unroll=8)

    n = pg * PAGE_CHUNKS
    for kv, w_ref in enumerate((wk_ref, wv_ref)):
        acc = jnp.zeros((n, 2 * NSA_KV_HEADS * CMP_HIDDEN), F32)
        for q in range(CMP_STRIDE // 2):
            x = jnp.concatenate([rows_ref[kv, pl.ds(2 * q + i, n, stride=CMP_STRIDE), :] for i in range(2)], axis=1)
            acc = acc + _dot(x, w_ref[q])
        half = NSA_KV_HEADS * CMP_HIDDEN
        ha_ref[0, :, kv * half:(kv + 1) * half] = acc[:, 0:half]
        hb_ref[0, :, kv * half:(kv + 1) * half] = acc[:, half:2 * half]


def paged_hidden(pool, page_table, cw, *, layer):
    b, n_pages = page_table.shape
    pg = math.gcd(n_pages, 64)
    pages = _pages_view(pool)
    full = lambda a: pl.BlockSpec(a.shape, lambda i, j, pt: (0,) * a.ndim)
    out = jax.ShapeDtypeStruct((b, n_pages * PAGE_CHUNKS, HID_WIDTH), F32)
    out_spec = pl.BlockSpec((1, pg * PAGE_CHUNKS, HID_WIDTH), lambda i, j, pt: (i, j, 0))
    ws = (cw['wk_pair'], cw['wv_pair'])
    return pl.pallas_call(
        functools.partial(_paged_hidden_kernel, layer=layer, pg=pg, n_steps=b * (n_pages // pg)),
        grid_spec=pltpu.PrefetchScalarGridSpec(
            num_scalar_prefetch=1,
            grid=(b, n_pages // pg),
            in_specs=[pl.BlockSpec(memory_space=pl.ANY)] + [full(a) for a in ws],
            out_specs=(out_spec, out_spec),
            scratch_shapes=[pltpu.VMEM((2, pg, 2, KV_WIDTH, PAGE_SIZE), F32),
                            pltpu.VMEM((2, pg * PAGE_SIZE, KV_WIDTH), F32),
                            pltpu.SemaphoreType.DMA((2,))]),
        out_shape=(out, out),
        compiler_params=_params(("arbitrary", "arbitrary"), big=True),
        name="paged_hidden",
    )(page_table.reshape(-1), pages, *ws)


def _own_half_queries(q8):
    q = q8.astype(F32)
    own_first = lax.div(lax.broadcasted_iota(jnp.int32, q.shape, 0), NSA_GQ) == 0
    return jnp.where(own_first, q, pltpu.roll(q, NSA_HEAD_DIM, 1))


SEL_KEYS = SEL_TOP * PAGE_SIZE
NO_KEY = 2 ** 30


def _decode_cmp_kernel(ha_ref, hb_ref, hpe_ref, new_ref, q_ref, w1b0_ref, w2_ref, cm_ref, sa_ref, sb_ref, map_ref,
                       oc_ref, idx_ref, kpos_ref, hbs_ref, *, q_pos, n_sel):
    n = ha_ref.shape[1]
    hbs_ref[0:n, :] = hb_ref[0]
    new8 = jnp.concatenate([new_ref[0], jnp.zeros((7, ROW_WIDTH), F32)], axis=0)
    hbs_ref[n:n + 8, :] = _dot(new8, w1b0_ref[...])
    kc, vc = _compress_tail(ha_ref[0], hbs_ref[1:n + 1, :], hpe_ref[0:1, :], w2_ref,
                            cm_ref[...], sa_ref[...], sb_ref[...])
    q8 = _own_half_queries(q_ref[0])
    n_idx = lax.broadcasted_iota(jnp.int32, (1, n), 1)
    p = _masked_softmax_rows(_dot_nt(q8, kc), CMP_STRIDE * n_idx + (CMP_LEN - 1) <= q_pos)
    oc_ref[0] = _dot(p, vc)

    row = lax.broadcasted_iota(jnp.int32, p.shape, 0)
    psum = jnp.zeros(p.shape, F32)
    for k in range(NSA_KV_HEADS):
        grp = jnp.sum(jnp.where(lax.div(row, NSA_GQ) == k, p, 0.0), axis=0, keepdims=True)
        psum = jnp.where(row == k, grp, psum)
    imp = jnp.dot(psum, map_ref[...], precision=lax.Precision.HIGHEST, preferred_element_type=F32)
    nsp = imp.shape[1]
    blk = lax.broadcasted_iota(jnp.int32, (NSA_HEADS, nsp), 1)
    cur = q_pos // SEL_LEN
    forced = (blk == 0) | (blk >= cur - 1)
    score = jnp.where(blk < n_sel, jnp.where(forced, FORCE_SCORE, imp), -jnp.inf)
    blk_f = blk.astype(F32)
    lane = lax.broadcasted_iota(jnp.int32, (NSA_HEADS, LANES), 1)
    key = lax.broadcasted_iota(jnp.int32, (NSA_HEADS, SEL_KEYS), 1)
    key_slot = lax.div(key, PAGE_SIZE)
    key_off = key - key_slot * PAGE_SIZE
    idx = jnp.zeros((NSA_HEADS, LANES), jnp.int32)
    kpos = jnp.zeros((NSA_HEADS, SEL_KEYS), jnp.int32)
    for i in range(SEL_TOP):
        top = jnp.max(score, axis=1, keepdims=True)
        pick_f = jnp.min(jnp.where(score == top, blk_f, float(nsp)), axis=1, keepdims=True)
        pick = pick_f.astype(jnp.int32)
        idx = jnp.where(lane == i, pick, idx)
        pos = lax.div(pick, PAGE_BLOCKS) * PAGE_SIZE + key_off
        kpos = jnp.where(key_slot == i, jnp.where(lax.div(pos, SEL_LEN) == pick, pos, NO_KEY), kpos)
        score = jnp.where(blk == pick, -jnp.inf, score)
    idx_ref[0] = idx
    kpos_ref[0] = kpos


def decode_cmp(ha, hb, hpe, cmp_new, q8, cw, tables, cmap, *, q_pos, n_sel):
    b, n, _ = ha.shape
    cm, sa, sb = tables
    w1b0 = cw['w1b'][0:ROW_WIDTH]
    per_b = lambda a: pl.BlockSpec((1,) + a.shape[1:], lambda i: (i, 0, 0))
    full = lambda a: pl.BlockSpec(a.shape, lambda i: (0,) * a.ndim)
    out8 = lambda w_, dt: (jax.ShapeDtypeStruct((b, NSA_HEADS, w_), dt), pl.BlockSpec((1, NSA_HEADS, w_), lambda i: (i, 0, 0)))
    outs = (out8(LANES, F32), out8(LANES, jnp.int32), out8(SEL_KEYS, jnp.int32))
    return pl.pallas_call(
        functools.partial(_decode_cmp_kernel, q_pos=q_pos, n_sel=n_sel),
        grid=(b,),
        in_specs=[per_b(ha), per_b(hb), full(hpe), per_b(cmp_new), per_b(q8), full(w1b0), full(cw['w2']),
                  full(cm), full(sa), full(sb), full(cmap)],
        out_specs=tuple(o[1] for o in outs),
        out_shape=tuple(o[0] for o in outs),
        scratch_shapes=[pltpu.VMEM((n + 8, HID_WIDTH), F32)],
        compiler_params=_params(("parallel",), big=True),
        name="decode_cmp",
    )(ha, hb, hpe, cmp_new, q8, w1b0, cw['w2'], cm, sa, sb, cmap)


def _decode_attn_kernel(idx_ref, pt_ref, pool_ref, win_ref, seln_ref, winn_ref, q_ref, g_ref, oc_ref, kpos_ref,
                        o_ref, wst_ref, k_ref, v_ref, sem, *, layer, n_pages, q_pos):
    b = pl.program_id(0)
    n_past = n_pages * PAGE_BLOCKS
    keep = win_ref.shape[4]
    hd = NSA_HEAD_DIM

    def slot(k, n_):
        blk = idx_ref[(b * NSA_KV_HEADS + k) * SEL_TOP + n_]
        page = pt_ref[b * n_pages + jnp.minimum(lax.div(blk, PAGE_BLOCKS), n_pages - 1)]
        lanes = slice(n_ * PAGE_SIZE, (n_ + 1) * PAGE_SIZE)
        dims = slice(k * hd, (k + 1) * hd)
        return blk, lanes, (pltpu.make_async_copy(pool_ref.at[layer, page, 0, dims, :], k_ref.at[k, :, lanes], sem),
                            pltpu.make_async_copy(pool_ref.at[layer, page, 1, dims, :], v_ref.at[k, :, lanes], sem))

    for k in range(NSA_KV_HEADS):
        for n_ in range(SEL_TOP):
            blk, lanes, copies = slot(k, n_)

            @pl.when(blk < n_past)
            def _():
                for cp in copies:
                    cp.start()

            @pl.when(blk >= n_past)
            def _():
                k_ref[k, :, lanes] = jnp.zeros((hd, PAGE_SIZE), F32)
                v_ref[k, :, lanes] = jnp.zeros((hd, PAGE_SIZE), F32)

    row0 = lax.broadcasted_iota(jnp.int32, (LANES, ROW_WIDTH), 0) == 0
    win_new_t = jnp.where(row0, jnp.broadcast_to(winn_ref[0], (LANES, ROW_WIDTH)), 0.0).T
    last_lane = lax.broadcasted_iota(jnp.int32, (KV_WIDTH, keep), 1) == keep - 1
    state = []
    for kv in range(2):
        shifted = pltpu.roll(win_ref[0, 0, kv], keep - 1, 1)
        state.append(jnp.where(last_lane, win_new_t[kv * KV_WIDTH:(kv + 1) * KV_WIDTH, 0:1], shifted))
        wst_ref[0, kv] = state[kv]

    q8 = q_ref[0]
    qk = q8[:, 0:hd]
    row = lax.broadcasted_iota(jnp.int32, (NSA_HEADS, LANES), 0)
    lane_half = lax.div(lax.broadcasted_iota(jnp.int32, (NSA_HEADS, LANES), 1), hd)
    own_half = lane_half == lax.div(row, NSA_GQ)

    def to_half(o, k):
        z = jnp.zeros_like(o)
        return jnp.concatenate([o, z] if k == 0 else [z, o], axis=1)

    o_w = jnp.zeros((NSA_HEADS, LANES), F32)
    for k in range(NSA_KV_HEADS):
        s = _dot(qk, state[0][k * hd:(k + 1) * hd, :])
        e = jnp.exp(s - jnp.max(s, axis=1, keepdims=True))
        p = e / jnp.maximum(jnp.sum(e, axis=1, keepdims=True), 1e-30)
        o_w = jnp.where(lax.div(row, NSA_GQ) == k, to_half(_dot_nt(p, state[1][k * hd:(k + 1) * hd, :]), k), o_w)

    for k in range(NSA_KV_HEADS):
        for n_ in range(SEL_TOP):
            blk, lanes, copies = slot(k, n_)

            @pl.when(blk < n_past)
            def _():
                for cp in copies:
                    cp.wait()

    new_k = seln_ref[0][:, 0:KV_WIDTH]
    new_v = seln_ref[0][:, KV_WIDTH:ROW_WIDTH]
    s_new = jnp.sum(_own_half_queries(q8) * new_k, axis=1, keepdims=True)
    grp = lax.div(lax.broadcasted_iota(jnp.int32, (NSA_HEADS, 1), 0), NSA_GQ)
    o_s = jnp.zeros((NSA_HEADS, LANES), F32)
    e_new = jnp.zeros((NSA_HEADS, 1), F32)
    denom = jnp.ones((NSA_HEADS, 1), F32)
    for k in range(NSA_KV_HEADS):
        mask = kpos_ref[0, k:k + 1, :] < q_pos
        s = jnp.where(mask, _dot(qk, k_ref[k]), MASK_NEG)
        m = jnp.maximum(jnp.max(s, axis=1, keepdims=True), s_new)
        e = jnp.where(mask, jnp.exp(s - m), 0.0)
        en = jnp.exp(s_new - m)
        o_s = jnp.where(lax.div(row, NSA_GQ) == k, to_half(_dot_nt(e, v_ref[k]), k), o_s)
        e_new = jnp.where(grp == k, en, e_new)
        denom = jnp.where(grp == k, jnp.sum(e, axis=1, keepdims=True) + en, denom)
    o_s = (o_s + e_new * jnp.where(own_half, jnp.broadcast_to(new_v, (NSA_HEADS, LANES)), 0.0)) / denom

    g = g_ref[0]
    out = g[:, 0:1] * oc_ref[0] + g[:, 1:2] * o_s + g[:, 2:3] * o_w
    o_ref[0] = jnp.where(own_half, out, 0.0).astype(o_ref.dtype)


def decode_attn(idx, page_table, pool, win_cache, sel_new, win_new, q8, gates3, oc, kpos, *, layer, q_pos):
    depth = pool.shape[0]
    b, n_pages = page_table.shape
    keep = win_cache.shape[2]
    assert keep == WINDOW
    win = win_cache.transpose(0, 1, 3, 4, 5, 2).reshape(depth, b, 2, KV_WIDTH, keep)
    per_b = lambda a: pl.BlockSpec((1,) + a.shape[1:], lambda i, ix, pt: (i, 0, 0))
    idx_flat = idx[:, 0:NSA_KV_HEADS, 0:SEL_TOP].reshape(-1)
    return pl.pallas_call(
        functools.partial(_decode_attn_kernel, layer=layer, n_pages=n_pages, q_pos=q_pos),
        grid_spec=pltpu.PrefetchScalarGridSpec(
            num_scalar_prefetch=2,
            grid=(b,),
            in_specs=[pl.BlockSpec(memory_space=pl.ANY),
                      pl.BlockSpec((1, 1, 2, KV_WIDTH, keep), lambda i, ix, pt: (layer, i, 0, 0, 0)),
                      per_b(sel_new), per_b(win_new), per_b(q8), per_b(gates3), per_b(oc), per_b(kpos)],
            out_specs=(pl.BlockSpec((1, NSA_HEADS, LANES), lambda i, ix, pt: (i, 0, 0)),
                       pl.BlockSpec((1, 2, KV_WIDTH, keep), lambda i, ix, pt: (i, 0, 0, 0))),
            scratch_shapes=[pltpu.VMEM((NSA_KV_HEADS, NSA_HEAD_DIM, SEL_KEYS), F32),
                            pltpu.VMEM((NSA_KV_HEADS, NSA_HEAD_DIM, SEL_KEYS), F32),
                            pltpu.SemaphoreType.DMA(())]),
        out_shape=(jax.ShapeDtypeStruct((b, NSA_HEADS, LANES), MXU_DTYPE),
                   jax.ShapeDtypeStruct((b, 2, KV_WIDTH, keep), F32)),
        compiler_params=_params(("arbitrary",), big=True),
        name="decode_attn",
    )(idx_flat, page_table.reshape(-1), _pages_view(pool), win, sel_new, win_new, q8, gates3, oc, kpos)


def _merge_ln_kernel(x_ref, ya_ref, yb_ref, yc_ref, wgm_ref, wa_ref, wb_ref, wc_ref, wo_ref, lng_ref, lnb_ref, o_ref):
    x = x_ref[0]
    xb = x.astype(MXU_DTYPE)
    gate = lambda j: _sigmoid(jnp.dot(xb, wgm_ref[:, j * D_MODEL:(j + 1) * D_MODEL], preferred_element_type=F32))
    merged = (gate(0) * _dot(ya_ref[...], wa_ref[...]) + gate(1) * _dot(yb_ref[0], wb_ref[...])
              + gate(2) * _dot(yc_ref[0], wc_ref[...]))
    o_ref[0] = _layer_norm(DN_ALPHA * x + _dot(merged, wo_ref[...]), lng_ref[...], lnb_ref[...])


def merge_ln(x, ya, yb, yc, mw, lng, lnb):
    nb, t, d = x.shape
    tt = min(t, 256)
    row3 = lambda w_: pl.BlockSpec((1, tt, w_), lambda b, i: (b, i, 0))
    full = lambda a: pl.BlockSpec(a.shape, lambda b, i: (0,) * a.ndim)
    ws = (mw['w_gm'], mw['w_a'], mw['w_b'], mw['w_c'], mw['w_o'], lng, lnb)
    return pl.pallas_call(
        _merge_ln_kernel,
        grid=(nb, t // tt),
        in_specs=[row3(d), pl.BlockSpec((tt, S5_WIDTH), lambda b, i: (i, b)), row3(QP_WIDTH), row3(CONV_CH)]
                 + [full(a) for a in ws],
        out_specs=row3(d),
        out_shape=jax.ShapeDtypeStruct((nb, t, d), F32),
        compiler_params=_params(("parallel", "parallel"), big=True),
        name="merge_ln",
    )(x, ya, yb, yc, *ws)


def _layer_weights(w, l):
    bf = lambda a: a.astype(MXU_DTYPE)
    row = lambda a: a.reshape(1, -1).astype(F32)
    w_in = w['w_in'][l]
    w_u, w_q, w_kv, w_gn, w_conv, w_gm = jnp.split(w_in, IN_OFFSETS, axis=-1)
    head_half = (jnp.arange(NSA_HEADS) // NSA_GQ)[None, :, None]

    def pad_heads(a3, own_half):
        z = jnp.zeros_like(a3)
        lo = jnp.concatenate([a3, z], axis=-1)
        hi = jnp.concatenate([z, a3], axis=-1)
        return (jnp.where(head_half == 0, lo, hi) if own_half else lo).reshape(a3.shape[0], QP_WIDTH)

    w_qp = pad_heads(w_q.reshape(D_MODEL, NSA_HEADS, NSA_HEAD_DIM), False)
    w_gpad = jnp.concatenate([w_gn, jnp.zeros((D_MODEL, LANES - w_gn.shape[1]), F32)], axis=1)
    w_proj = bf(jnp.concatenate([w_u, w_qp, w_kv, w_gpad, w_conv], axis=1))
    assert w_proj.shape[1] == C_END
    w_b = pad_heads(w['w_proj_b'][l].T.reshape(D_MODEL, NSA_HEADS, NSA_HEAD_DIM), True).T
    conv_w = jnp.concatenate([w['conv_w'][l], jnp.zeros((8 - CONV_K, CONV_CH), F32)], axis=0)
    gu = lambda name: jnp.split(w[name][l], 2, axis=-1)
    return dict(
        ffn1=(bf(gu('ffn1_w_gu')[0]), bf(gu('ffn1_w_gu')[1]), bf(w['ffn1_w_down'][l])),
        ffn2=(bf(gu('ffn2_w_gu')[0]), bf(gu('ffn2_w_gu')[1]), bf(w['ffn2_w_down'][l])),
        ln=[(row(w['ln_g'][l, j]), row(w['ln_b'][l, j])) for j in range(3)],
        w_proj=w_proj, conv_w=conv_w,
        s5=_s5_params(w['s5_lambda_re'][l], w['s5_lambda_im'][l], w['s5_log_dt'][l], w['s5_b_re'][l], w['s5_b_im'][l],
                      w['s5_c_re'][l], w['s5_c_im'][l], w['s5_d'][l], w['s5_w_glu'][l], w['s5_b_glu'][l]),
        cmp=_compress_params(w['nsa_phi_k1'][l], w['nsa_phi_k2'][l], w['nsa_phi_v1'][l], w['nsa_phi_v2'][l],
                             w['nsa_pe_k'][l], w['nsa_pe_v'][l]),
        merge=dict(w_gm=bf(w_gm), w_a=bf(w['w_proj_a'][l]), w_b=bf(w_b), w_c=bf(w['w_proj_c'][l]), w_o=bf(w['w_o'][l])),
    )


def _prompt_trunk(x, layers):
    b, t, d = x.shape
    n_chunks = t // CMP_STRIDE
    tok_tables = _rope_tables(jnp.arange(t))
    cmp_tables = _rope_tables(jnp.arange(n_chunks) * CMP_STRIDE + (CMP_LEN - 1))
    zero_state = jnp.zeros((b, S5_FLAT), F32)
    keep = min(WINDOW, t)
    states, hpes = [], []
    x2 = x.reshape(b * t, d)
    for lw in layers:
        x2 = ffn_ln(x2, *lw['ffn1'], *lw['ln'][0])
        u, qp, cmp_rows, sel_t, win_t, gates, yc, conv_state, selk, selv, wink, winv, cmp_t = in_proj(
            x2.reshape(b, t, d), lw['w_proj'], tok_tables, lw['conv_w'], jnp.zeros((8, LANES), F32), decode=False)
        ya, s_re, s_im = s5_branch(u.reshape(t * b, S5_WIDTH), zero_state, zero_state, lw['s5'], nb=b)
        kc, vc, hpe = compress(cmp_rows.reshape(b, n_chunks, CHUNK_WIDTH), lw['cmp'], cmp_tables)
        hpes.append(hpe)
        yb = nsa_prompt(qp, gates, kc, vc, selk, selv, wink, winv)
        x3 = merge_ln(x2.reshape(b, t, d), ya.reshape(t, b * S5_WIDTH), yb, yc, lw['merge'], *lw['ln'][1])
        x2 = ffn_ln(x3.reshape(b * t, d), *lw['ffn2'], *lw['ln'][2])
        rows = lambda a: a.reshape(b, 2, NSA_KV_HEADS, NSA_HEAD_DIM, -1).transpose(0, 4, 1, 2, 3)
        states.append((rows(cmp_t), rows(sel_t), rows(win_t[..., t - keep:]), conv_state,
                       s_re.reshape(b, S5_GROUPS, S5_STATE), s_im.reshape(b, S5_GROUPS, S5_STATE)))
    return x2.reshape(b, t, d), [jnp.stack(s, axis=0) for s in zip(*states)], hpes


def _sample_trunk(x, layers, hpes, cache_cmp, cache_sel, cache_win, cache_conv, s5_re, s5_im, page_table):
    bs, dec_seq, d = x.shape
    assert dec_seq == 1
    n_pages = page_table.shape[1]
    q_pos = n_pages * PAGE_SIZE
    n_blocks = n_pages * PAGE_CHUNKS
    n_sel = q_pos // SEL_LEN + 1
    assert cache_win.shape[2] == min(WINDOW, q_pos) and n_sel >= SEL_TOP
    tok_tables = _rope_tables(jnp.full((bs,), q_pos))
    cmp_tables = _rope_tables(jnp.arange(n_blocks) * CMP_STRIDE + (CMP_LEN - 1))
    cmap = _cmp_to_sel_map(n_blocks, n_blocks, n_sel)
    kv_shape = (2, NSA_KV_HEADS, NSA_HEAD_DIM)
    states = []
    x2 = x.reshape(bs, d)
    for l, lw in enumerate(layers):
        x2 = ffn_ln(x2, *lw['ffn1'], *lw['ln'][0])
        u, qp, cmp_new, sel_new, win_new, gates, yc, conv_state = in_proj(
            x2[None], lw['w_proj'], tok_tables, lw['conv_w'], cache_conv[l].reshape(bs, (CONV_K - 1) * CONV_CH),
            decode=True)
        ya, s_re, s_im = s5_branch(u, s5_re[l].reshape(bs, S5_FLAT), s5_im[l].reshape(bs, S5_FLAT), lw['s5'], nb=bs)
        ha, hb = paged_hidden(cache_cmp, page_table, lw['cmp'], layer=l)
        hpe = hpes[l]
        q8 = qp.reshape(bs, NSA_HEADS, LANES)
        per_seq = lambda a: a.reshape(bs, 1, ROW_WIDTH)
        oc, idx, kpos = decode_cmp(ha, hb, hpe, per_seq(cmp_new), q8, lw['cmp'], cmp_tables, cmap,
                                   q_pos=q_pos, n_sel=n_sel)
        g3 = gates[0, :, 0:3 * NSA_HEADS].reshape(bs, NSA_HEADS, 3)
        g3 = jnp.concatenate([g3, jnp.zeros((bs, NSA_HEADS, LANES - 3), F32)], axis=-1)
        yb, win_state = decode_attn(idx, page_table, cache_sel, cache_win, per_seq(sel_new), per_seq(win_new), q8, g3,
                                    oc, kpos, layer=l, q_pos=q_pos)
        x3 = merge_ln(x2[None], ya, yb.reshape(1, bs, QP_WIDTH), yc, lw['merge'], *lw['ln'][1])
        x2 = ffn_ln(x3[0], *lw['ffn2'], *lw['ln'][2])
        states.append((cmp_new.reshape((bs, 1) + kv_shape), sel_new.reshape((bs, 1) + kv_shape),
                       win_state.reshape((bs,) + kv_shape + (-1,)).transpose(0, 4, 1, 2, 3),
                       conv_state.reshape(bs, CONV_K - 1, CONV_CH),
                       s_re.reshape(bs, S5_GROUPS, S5_STATE), s_im.reshape(bs, S5_GROUPS, S5_STATE)))
    return x2.reshape(bs, 1, d), [jnp.stack(s, axis=0) for s in zip(*states)]


def kernel(x_prompt, x_sample, cache_cmp_kv, cache_sel_kv, cache_win_kv, cache_conv, state_s5_re, state_s5_im, page_table, ln_g, ln_b, ffn1_w_gu, ffn1_w_down, ffn2_w_gu, ffn2_w_down, w_in, s5_lambda_re, s5_lambda_im, s5_log_dt, s5_b_re, s5_b_im, s5_c_re, s5_c_im, s5_d, s5_w_glu, s5_b_glu, nsa_pe_k, nsa_pe_v, nsa_phi_k1, nsa_phi_k2, nsa_phi_v1, nsa_phi_v2, conv_w, w_proj_a, w_proj_b, w_proj_c, w_o):
    w = dict(ln_g=ln_g, ln_b=ln_b, ffn1_w_gu=ffn1_w_gu, ffn1_w_down=ffn1_w_down, ffn2_w_gu=ffn2_w_gu,
             ffn2_w_down=ffn2_w_down, w_in=w_in, s5_lambda_re=s5_lambda_re, s5_lambda_im=s5_lambda_im,
             s5_log_dt=s5_log_dt, s5_b_re=s5_b_re, s5_b_im=s5_b_im, s5_c_re=s5_c_re, s5_c_im=s5_c_im, s5_d=s5_d,
             s5_w_glu=s5_w_glu, s5_b_glu=s5_b_glu, nsa_pe_k=nsa_pe_k, nsa_pe_v=nsa_pe_v, nsa_phi_k1=nsa_phi_k1,
             nsa_phi_k2=nsa_phi_k2, nsa_phi_v1=nsa_phi_v1, nsa_phi_v2=nsa_phi_v2, conv_w=conv_w,
             w_proj_a=w_proj_a, w_proj_b=w_proj_b, w_proj_c=w_proj_c, w_o=w_o)
    layers = [_layer_weights(w, l) for l in range(DEPTH)]
    y_prompt, (p_cmp, p_sel, p_win, p_conv, p_re, p_im), hpes = _prompt_trunk(x_prompt, layers)
    y_sample, (s_cmp, s_sel, s_win, s_conv, s_re, s_im) = _sample_trunk(
        x_sample, layers, hpes, cache_cmp_kv, cache_sel_kv, cache_win_kv, cache_conv, state_s5_re, state_s5_im, page_table)
    return (y_prompt, y_sample, p_cmp, s_cmp, p_sel, s_sel, p_win, s_win, p_conv, s_conv, p_re, s_re, p_im, s_im)
```

```python
import functools
import math

import numpy as np
import jax
import jax.numpy as jnp
from jax import lax
from jax.experimental import pallas as pl
from jax.experimental.pallas import tpu as pltpu

F32 = jnp.float32
MXU_DTYPE = jnp.bfloat16

D_MODEL = 1024
DEPTH = 2
PAGE_SIZE = 128
DN_ALPHA = (2.0 * DEPTH) ** 0.25
LN_EPS = 1e-5
D_FF = 2816
FFN_RES = 0.5
S5_WIDTH = 512
S5_GROUP = 16
S5_GROUPS = S5_WIDTH // S5_GROUP
S5_STATE = 64
S5_FLAT = S5_GROUPS * S5_STATE
NSA_HEADS = 8
NSA_KV_HEADS = 2
NSA_HEAD_DIM = 64
NSA_GQ = NSA_HEADS // NSA_KV_HEADS
CMP_LEN = 32
CMP_STRIDE = 16
CMP_HIDDEN = 128
SEL_LEN = 64
SEL_TOP = 16
WINDOW = 512
ROPE_THETA = 500000.0
ROT_DIM = NSA_HEAD_DIM // 4
ROT_HALF = ROT_DIM // 2
FORCE_SCORE = 1e9
CONV_CH = 512
CONV_K = 3
N_BRANCH = 3
KV_WIDTH = NSA_KV_HEADS * NSA_HEAD_DIM
ROW_WIDTH = 2 * KV_WIDTH
IN_WIDTHS = (S5_WIDTH, NSA_HEADS * NSA_HEAD_DIM, 6 * KV_WIDTH, 3 * NSA_HEADS, 3 * CONV_CH, N_BRANCH * D_MODEL)
IN_OFFSETS = tuple(int(o) for o in np.cumsum(IN_WIDTHS)[:-1])

LANES = 128
V7X_VMEM_BYTES = 64 * 1024 * 1024
VMEM_LIMIT = (V7X_VMEM_BYTES * 7) // 8

QP_WIDTH = NSA_HEADS * LANES
MASK_NEG = -1e30
M_INIT = -1e29

C_U = 0
C_Q = C_U + S5_WIDTH
C_KV = C_Q + QP_WIDTH
C_G = C_KV + 6 * KV_WIDTH
C_CONV = C_G + LANES
C_END = C_CONV + 3 * CONV_CH


def _sigmoid(x):
    return 1.0 / (1.0 + jnp.exp(-x))


def _gelu_tanh(x):
    return 0.5 * x * (1.0 + jnp.tanh(math.sqrt(2.0 / math.pi) * (x + 0.044715 * (x * x * x))))


def _layer_norm(x, g, b):
    mu = jnp.mean(x, axis=-1, keepdims=True)
    xc = x - mu
    var = jnp.mean(xc * xc, axis=-1, keepdims=True)
    return xc * lax.rsqrt(var + LN_EPS) * g + b


def _dot(a, b):
    return jnp.dot(a.astype(MXU_DTYPE), b.astype(MXU_DTYPE), preferred_element_type=F32)


def _dot_nt(a, b):
    return lax.dot_general(a.astype(MXU_DTYPE), b.astype(MXU_DTYPE), (((1,), (1,)), ((), ())),
                           preferred_element_type=F32)


def _rope_lanes(x, cm, sa, sb):
    return x * cm + pltpu.roll(x, LANES - ROT_HALF, 1) * sa + pltpu.roll(x, ROT_HALF, 1) * sb


def _rope_tables(pos):
    inv_freq = ROPE_THETA ** (-jnp.arange(ROT_HALF, dtype=F32) / ROT_HALF)
    ang = pos.astype(F32)[:, None] * inv_freq
    cos, sin = jnp.cos(ang), jnp.sin(ang)
    n = pos.shape[0]
    rest = NSA_HEAD_DIM - ROT_DIM
    cm = jnp.concatenate([cos, cos, jnp.ones((n, rest), F32)], axis=1)
    sa = jnp.concatenate([-sin, jnp.zeros((n, ROT_HALF + rest), F32)], axis=1)
    sb = jnp.concatenate([jnp.zeros((n, ROT_HALF), F32), sin, jnp.zeros((n, rest), F32)], axis=1)
    return tuple(jnp.tile(t, (1, LANES // NSA_HEAD_DIM)) for t in (cm, sa, sb))


def _params(sem, big=False):
    return pltpu.CompilerParams(dimension_semantics=sem, vmem_limit_bytes=VMEM_LIMIT if big else None)


FF_CHUNK = D_FF


def _ffn_ln_kernel(x_ref, wg_ref, wu_ref, wd_ref, lng_ref, lnb_ref, o_ref):
    x = x_ref[...]
    xb = x.astype(MXU_DTYPE)
    acc = jnp.zeros(x.shape, F32)
    for c in range(0, D_FF, FF_CHUNK):
        gate = jnp.dot(xb, wg_ref[:, c:c + FF_CHUNK], preferred_element_type=F32)
        up = jnp.dot(xb, wu_ref[:, c:c + FF_CHUNK], preferred_element_type=F32)
        acc = acc + _dot(gate * _sigmoid(gate) * up, wd_ref[c:c + FF_CHUNK, :])
    o_ref[...] = _layer_norm(DN_ALPHA * x + FFN_RES * acc, lng_ref[...], lnb_ref[...])


def ffn_ln(x, wg, wu, wd, lng, lnb):
    n, d = x.shape
    tm = min(n, 512)
    resident = lambda a: pl.BlockSpec(a.shape, lambda i: (0, 0), pipeline_mode=pl.Buffered(1))
    return pl.pallas_call(
        _ffn_ln_kernel,
        grid=(n // tm,),
        in_specs=[pl.BlockSpec((tm, d), lambda i: (i, 0)), resident(wg), resident(wu), resident(wd),
                  resident(lng), resident(lnb)],
        out_specs=pl.BlockSpec((tm, d), lambda i: (i, 0)),
        out_shape=jax.ShapeDtypeStruct((n, d), F32),
        compiler_params=_params(("parallel",), big=True),
        name="ffn_ln",
    )(x, wg, wu, wd, lng, lnb)


def _in_proj_kernel(x_ref, w_ref, cm_ref, sa_ref, sb_ref, cw_ref, cprev_ref,
                    u_ref, q_ref, cmp_ref, sel_ref, win_ref, g_ref, yc_ref, cst_ref, *rest, decode):
    if decode:
        (vbuf_ref,) = rest
    else:
        selk_ref, selv_ref, wink_ref, winv_ref, cmpt_ref, vbuf_ref = rest
    x = x_ref[0]
    rows = x.shape[0]
    z = _dot(x, w_ref[...])
    cm, sa, sb = cm_ref[...], sa_ref[...], sb_ref[...]

    u_ref[...] = z[:, C_U:C_U + S5_WIDTH]
    scale = NSA_HEAD_DIM ** -0.5
    for h in range(NSA_HEADS):
        c = C_Q + h * LANES
        q_ref[0, :, h * LANES:(h + 1) * LANES] = (_rope_lanes(z[:, c:c + LANES], cm, sa, sb) * scale).astype(q_ref.dtype)
    cmp_ref[0] = z[:, C_KV:C_KV + ROW_WIDTH]
    ks = _rope_lanes(z[:, C_KV + 2 * KV_WIDTH:C_KV + 3 * KV_WIDTH], cm, sa, sb)
    vs = z[:, C_KV + 3 * KV_WIDTH:C_KV + 4 * KV_WIDTH]
    kw = _rope_lanes(z[:, C_KV + 4 * KV_WIDTH:C_KV + 5 * KV_WIDTH], cm, sa, sb)
    vw = z[:, C_KV + 5 * KV_WIDTH:C_KV + 6 * KV_WIDTH]
    g_ref[0] = _sigmoid(z[:, C_G:C_G + LANES])
    if decode:
        sel_ref[0, :, 0:KV_WIDTH] = ks
        sel_ref[0, :, KV_WIDTH:ROW_WIDTH] = vs
        win_ref[0, :, 0:KV_WIDTH] = kw
        win_ref[0, :, KV_WIDTH:ROW_WIDTH] = vw
    else:
        for kv, (s_kv, w_kv) in enumerate(((ks, kw), (vs, vw))):
            cmpt_ref[0, kv] = z[:, C_KV + kv * KV_WIDTH:C_KV + (kv + 1) * KV_WIDTH].T
            sel_ref[0, kv] = s_kv.T
            win_ref[0, kv] = w_kv.T
        lane = lax.broadcasted_iota(jnp.int32, (rows, LANES), 1)
        pos = pl.program_id(1) * rows + lax.broadcasted_iota(jnp.int32, (rows, LANES), 0)
        first = lane < NSA_HEAD_DIM
        tag = jnp.where(lane - NSA_HEAD_DIM == lax.div(pos, SEL_LEN), 1.0, 0.0)
        for k in range(NSA_KV_HEADS):
            ks_k = ks if k == 0 else pltpu.roll(ks, NSA_HEAD_DIM, 1)
            kw_k = kw if k == 0 else pltpu.roll(kw, NSA_HEAD_DIM, 1)
            selk_ref[0, k] = jnp.where(first, ks_k, tag).astype(selk_ref.dtype)
            wink_ref[0, k] = jnp.where(first, kw_k, 0.0).astype(wink_ref.dtype)
        selv_ref[0] = vs.astype(selv_ref.dtype)
        winv_ref[0] = vw.astype(winv_ref.dtype)

    cb = z[:, C_CONV:C_CONV + CONV_CH]
    v = z[:, C_CONV + CONV_CH:C_CONV + 2 * CONV_CH] * z[:, C_CONV + 2 * CONV_CH:C_CONV + 3 * CONV_CH]
    w0, w1, w2 = cw_ref[0:1, :], cw_ref[1:2, :], cw_ref[2:3, :]
    if decode:
        p0 = cprev_ref[:, 0:CONV_CH]
        p1 = cprev_ref[:, CONV_CH:2 * CONV_CH]
        yc_ref[0] = (cb * (w0 * p0 + w1 * p1 + w2 * v)).astype(yc_ref.dtype)
        cst_ref[:, 0:CONV_CH] = p1
        cst_ref[:, CONV_CH:2 * CONV_CH] = v
    else:
        @pl.when(pl.program_id(1) == 0)
        def _():
            vbuf_ref[0:8, :] = jnp.zeros((8, CONV_CH), F32)

        vbuf_ref[8:8 + rows, :] = v
        conv = w0 * vbuf_ref[6:6 + rows, :] + w1 * vbuf_ref[7:7 + rows, :] + w2 * v
        yc_ref[0] = (cb * conv).astype(yc_ref.dtype)
        last = vbuf_ref[rows:rows + 8, :]
        vbuf_ref[0:8, :] = last
        cst_ref[0] = last[6:8, :]


def in_proj(x, w, tables, conv_w, conv_prev, *, decode):
    nb, t, d = x.shape
    tt = t if decode else min(t, 512)
    grid = (nb, t // tt)
    cm, sa, sb = tables
    row3 = lambda w_: pl.BlockSpec((1, tt, w_), lambda b, i: (b, i, 0))
    tab = pl.BlockSpec((tt, LANES), lambda b, i: (i, 0))
    if decode:
        cprev_spec = pl.BlockSpec((t, 2 * CONV_CH), lambda b, i: (0, 0))
        cst_spec = pl.BlockSpec((t, 2 * CONV_CH), lambda b, i: (0, 0))
        cst_shape = jax.ShapeDtypeStruct((t, 2 * CONV_CH), F32)
    else:
        cprev_spec = pl.BlockSpec((8, LANES), lambda b, i: (0, 0))
        cst_spec = pl.BlockSpec((1, CONV_K - 1, CONV_CH), lambda b, i: (b, 0, 0))
        cst_shape = jax.ShapeDtypeStruct((nb, CONV_K - 1, CONV_CH), F32)
    slabs = (jax.ShapeDtypeStruct((nb, 2, KV_WIDTH, t), F32),
             pl.BlockSpec((1, 2, KV_WIDTH, tt), lambda b, i: (b, 0, 0, i)))
    rows_out = (jax.ShapeDtypeStruct((nb, t, ROW_WIDTH), F32), row3(ROW_WIDTH))
    kv_out = rows_out if decode else slabs
    out_shape = (
        jax.ShapeDtypeStruct((t, nb * S5_WIDTH), F32),
        jax.ShapeDtypeStruct((nb, t, QP_WIDTH), MXU_DTYPE),
        rows_out[0],
        kv_out[0], kv_out[0],
        jax.ShapeDtypeStruct((nb, t, LANES), F32),
        jax.ShapeDtypeStruct((nb, t, CONV_CH), MXU_DTYPE),
        cst_shape,
    )
    out_specs = (pl.BlockSpec((tt, S5_WIDTH), lambda b, i: (i, b)), row3(QP_WIDTH), rows_out[1], kv_out[1], kv_out[1],
                 row3(LANES), row3(CONV_CH), cst_spec)
    if not decode:
        assert t <= NSA_HEAD_DIM * SEL_LEN
        per_head = (jax.ShapeDtypeStruct((nb, NSA_KV_HEADS, t, LANES), MXU_DTYPE),
                    pl.BlockSpec((1, NSA_KV_HEADS, tt, LANES), lambda b, i: (b, 0, i, 0)))
        packed = (jax.ShapeDtypeStruct((nb, t, KV_WIDTH), MXU_DTYPE), row3(KV_WIDTH))
        extra = (per_head, packed, per_head, packed, slabs)
        out_shape += tuple(e[0] for e in extra)
        out_specs += tuple(e[1] for e in extra)
    return pl.pallas_call(
        functools.partial(_in_proj_kernel, decode=decode),
        grid=grid,
        in_specs=[row3(d), pl.BlockSpec((d, C_END), lambda b, i: (0, 0)), tab, tab, tab,
                  pl.BlockSpec((8, CONV_CH), lambda b, i: (0, 0)), cprev_spec],
        out_specs=out_specs,
        out_shape=out_shape,
        scratch_shapes=[pltpu.VMEM((tt + 8, CONV_CH), F32)],
        compiler_params=_params(("parallel", "arbitrary"), big=True),
        name="in_proj",
    )(x, w, cm, sa, sb, conv_w, conv_prev)


S5_COLS = 512


def _s5_kernel(u_ref, s0r_ref, s0i_ref, ar_ref, ai_ref, br_ref, bi_ref, cr_ref, ci_ref, d_ref, wglu_ref, bglu_ref,
               y_ref, sr_ref, si_ref, xr_ref, xi_ref, *shuffle, nb, tc, wide):
    @pl.when(pl.program_id(0) == 0)
    def _():
        sr_ref[...] = s0r_ref[...]
        si_ref[...] = s0i_ref[...]

    slabs = S5_WIDTH // LANES
    if wide:
        uio_ref, yio_ref = shuffle
        for b in range(nb):
            for c in range(slabs):
                lanes = slice(b * S5_WIDTH + c * LANES, b * S5_WIDTH + (c + 1) * LANES)
                uio_ref[c, pl.ds(b, tc, stride=nb), :] = u_ref[:, lanes]
        u = jnp.concatenate([uio_ref[c] for c in range(slabs)], axis=1)
    else:
        u = u_ref[...]
    ub = u.astype(MXU_DTYPE)
    hw, hf = S5_WIDTH // 2, S5_FLAT // 2
    for h in range(2):
        xr_ref[:, h * hf:(h + 1) * hf] = jnp.dot(ub[:, h * hw:(h + 1) * hw], br_ref[h], preferred_element_type=F32)
        xi_ref[:, h * hf:(h + 1) * hf] = jnp.dot(ub[:, h * hw:(h + 1) * hw], bi_ref[h], preferred_element_type=F32)

    for c in range(S5_FLAT // S5_COLS):
        cols = slice(c * S5_COLS, (c + 1) * S5_COLS)
        ar = jnp.broadcast_to(ar_ref[:, cols], (nb, S5_COLS))
        ai = jnp.broadcast_to(ai_ref[:, cols], (nb, S5_COLS))

        def step(t, carry):
            sr, si = carry
            rows = pl.ds(pl.multiple_of(t * nb, nb), nb)
            nr = ar * sr - ai * si + xr_ref[rows, cols]
            ni = ar * si + ai * sr + xi_ref[rows, cols]
            xr_ref[rows, cols] = nr
            xi_ref[rows, cols] = ni
            return nr, ni

        sr, si = lax.fori_loop(0, tc, step, (sr_ref[:, cols], si_ref[:, cols]), unroll=min(tc, 8))
        sr_ref[:, cols] = sr
        si_ref[:, cols] = si

    y = jnp.concatenate([_dot(xr_ref[:, h * hf:(h + 1) * hf], cr_ref[h]) - _dot(xi_ref[:, h * hf:(h + 1) * hf], ci_ref[h])
                         for h in range(2)], axis=1) + d_ref[...] * u
    y = _gelu_tanh(y)
    y = y * _sigmoid(_dot(y, wglu_ref[...]) + bglu_ref[...])
    if wide:
        for c in range(slabs):
            yio_ref[c] = y[:, c * LANES:(c + 1) * LANES]
        for b in range(nb):
            for c in range(slabs):
                lanes = slice(b * S5_WIDTH + c * LANES, b * S5_WIDTH + (c + 1) * LANES)
                y_ref[:, lanes] = yio_ref[c, pl.ds(b, tc, stride=nb), :].astype(y_ref.dtype)
    else:
        y_ref[...] = y.astype(y_ref.dtype)


def s5_branch(u, s0r, s0i, p, *, nb, wide):
    if wide:
        t = u.shape[0]
        tc = min(t, 128)
        block = (tc, nb * S5_WIDTH)
        shuffle = [pltpu.VMEM((S5_WIDTH // LANES, tc * nb, LANES), F32)] * 2
    else:
        t = u.shape[0] // nb
        tc = min(t, 128)
        block = (tc * nb, S5_WIDTH)
        shuffle = []
    full = lambda a: pl.BlockSpec(a.shape, lambda i: (0,) * a.ndim)
    args = (u, s0r, s0i, p['a_re'], p['a_im'], p['bb_re'], p['bb_im'], p['c_re'], p['c_im'], p['d'], p['w_glu'], p['b_glu'])
    return pl.pallas_call(
        functools.partial(_s5_kernel, nb=nb, tc=tc, wide=wide),
        grid=(t // tc,),
        in_specs=[pl.BlockSpec(block, lambda i: (i, 0))] + [full(a) for a in args[1:]],
        out_specs=(pl.BlockSpec(block, lambda i: (i, 0)),
                   pl.BlockSpec((nb, S5_FLAT), lambda i: (0, 0)),
                   pl.BlockSpec((nb, S5_FLAT), lambda i: (0, 0))),
        out_shape=(jax.ShapeDtypeStruct(u.shape, MXU_DTYPE),
                   jax.ShapeDtypeStruct((nb, S5_FLAT), F32),
                   jax.ShapeDtypeStruct((nb, S5_FLAT), F32)),
        scratch_shapes=[pltpu.VMEM((tc * nb, S5_FLAT), F32), pltpu.VMEM((tc * nb, S5_FLAT), F32)] + shuffle,
        compiler_params=_params(("arbitrary",), big=True),
        name="s5",
    )(*args)


def _s5_params(lam_re, lam_im, log_dt, b_re, b_im, c_re, c_im, d_skip, w_glu, b_glu):
    dt = jnp.exp(log_dt.astype(F32))[:, None]
    lr, li = lam_re.astype(F32), lam_im.astype(F32)
    mag = jnp.exp(lr * dt)
    a_re = mag * jnp.cos(li * dt)
    a_im = mag * jnp.sin(li * dt)
    den = lr * lr + li * li
    r_re = ((a_re - 1.0) * lr + a_im * li) / den
    r_im = (a_im * lr - (a_re - 1.0) * li) / den
    bb_re = r_re[..., None] * b_re - r_im[..., None] * b_im
    bb_im = r_re[..., None] * b_im + r_im[..., None] * b_re
    eye = jnp.eye(S5_GROUPS, dtype=F32)
    hw, hf = S5_WIDTH // 2, S5_FLAT // 2
    halves = lambda m: jnp.stack([m[:m.shape[0] // 2, :m.shape[1] // 2], m[m.shape[0] // 2:, m.shape[1] // 2:]])
    blk_in = lambda bb: halves(jnp.einsum('gpi,gh->gihp', bb, eye).reshape(S5_WIDTH, S5_FLAT)).astype(MXU_DTYPE)
    blk_out = lambda c: halves(jnp.einsum('gop,gh->gpho', c.astype(F32), eye).reshape(S5_FLAT, S5_WIDTH)).astype(MXU_DTYPE)
    return dict(a_re=a_re.reshape(1, S5_FLAT), a_im=a_im.reshape(1, S5_FLAT),
                bb_re=blk_in(bb_re), bb_im=blk_in(bb_im), c_re=blk_out(c_re), c_im=blk_out(c_im),
                d=d_skip.reshape(1, S5_WIDTH).astype(F32), w_glu=w_glu.astype(MXU_DTYPE),
                b_glu=b_glu.reshape(1, S5_WIDTH).astype(F32))


CHUNK_WIDTH = CMP_STRIDE * ROW_WIDTH
HID_WIDTH = 4 * CMP_HIDDEN


def _compress_tail(ha, hb_next, hpe, w2_ref, cm, sa, sb):
    kv = _dot(_gelu_tanh(ha + hb_next + hpe), w2_ref[...])
    return _rope_lanes(kv[:, 0:KV_WIDTH], cm, sa, sb), kv[:, KV_WIDTH:ROW_WIDTH]


def _compress_kernel(x_ref, w1a_ref, w1b_ref, pea_ref, peb_ref, w2_ref, cm_ref, sa_ref, sb_ref,
                     kc_ref, vc_ref, hpe_ref, hb_ref):
    x = x_ref[0].astype(MXU_DTYPE)
    n = x.shape[0]
    ha = jnp.dot(x, w1a_ref[...], preferred_element_type=F32)
    hb_ref[0:n, :] = jnp.dot(x, w1b_ref[...], preferred_element_type=F32)
    hb_ref[n:n + 8, :] = jnp.zeros((8, HID_WIDTH), F32)
    hpe = _dot(pea_ref[...], w1a_ref[...]) + _dot(peb_ref[...], w1b_ref[...])
    hpe_ref[...] = hpe
    kc, vc = _compress_tail(ha, hb_ref[1:n + 1, :], hpe[0:1, :], w2_ref, cm_ref[...], sa_ref[...], sb_ref[...])
    first = lax.broadcasted_iota(jnp.int32, kc.shape, 1) < NSA_HEAD_DIM
    kc_ref[0, 0] = jnp.where(first, kc, 0.0).astype(kc_ref.dtype)
    kc_ref[0, 1] = jnp.where(first, pltpu.roll(kc, NSA_HEAD_DIM, 1), 0.0).astype(kc_ref.dtype)
    vc_ref[0] = vc.astype(vc_ref.dtype)


def compress(rows, cw, tables):
    b, n, _ = rows.shape
    cm, sa, sb = tables
    full = lambda a: pl.BlockSpec(a.shape, lambda i: (0,) * a.ndim)
    args = (rows, cw['w1a'], cw['w1b'], cw['pe_a'], cw['pe_b'], cw['w2'], cm, sa, sb)
    return pl.pallas_call(
        _compress_kernel,
        grid=(b,),
        in_specs=[pl.BlockSpec((1, n, CHUNK_WIDTH), lambda i: (i, 0, 0))] + [full(a) for a in args[1:]],
        out_specs=(pl.BlockSpec((1, NSA_KV_HEADS, n, LANES), lambda i: (i, 0, 0, 0)),
                   pl.BlockSpec((1, n, KV_WIDTH), lambda i: (i, 0, 0)),
                   pl.BlockSpec((8, HID_WIDTH), lambda i: (0, 0))),
        out_shape=(jax.ShapeDtypeStruct((b, NSA_KV_HEADS, n, LANES), MXU_DTYPE),
                   jax.ShapeDtypeStruct((b, n, KV_WIDTH), MXU_DTYPE),
                   jax.ShapeDtypeStruct((8, HID_WIDTH), F32)),
        scratch_shapes=[pltpu.VMEM((n + 8, HID_WIDTH), F32)],
        compiler_params=_params(("arbitrary",), big=True),
        name="compress",
    )(*args)


def _compress_params(phi_k1, phi_k2, phi_v1, phi_v2, pe_k, pe_v):
    wk = phi_k1.reshape(CMP_LEN, NSA_HEAD_DIM, CMP_HIDDEN)
    wv = phi_v1.reshape(CMP_LEN, NSA_HEAD_DIM, CMP_HIDDEN)
    w = jnp.stack([wk, wk, wv, wv], axis=1)
    w1 = jnp.einsum('sjdu,jm->sjdmu', w, jnp.eye(4, dtype=F32)).reshape(CMP_LEN * ROW_WIDTH, HID_WIDTH)
    w2 = jnp.einsum('jud,jm->jumd', jnp.stack([phi_k2, phi_k2, phi_v2, phi_v2]), jnp.eye(4, dtype=F32))
    pe = jnp.concatenate([pe_k, pe_k, pe_v, pe_v], axis=1)
    pad8 = lambda r: jnp.concatenate([r, jnp.zeros((7, CHUNK_WIDTH), F32)], axis=0)

    def pair_weights(w3):
        halves = jnp.stack([w3[:CMP_STRIDE], w3[CMP_STRIDE:]]).reshape(2, CMP_STRIDE // 2, 2, NSA_HEAD_DIM, CMP_HIDDEN)
        full = jnp.einsum('pqidu,hg->qihdpgu', halves, jnp.eye(NSA_KV_HEADS, dtype=F32))
        return full.reshape(CMP_STRIDE // 2, 2 * KV_WIDTH, 2 * NSA_KV_HEADS * CMP_HIDDEN).astype(MXU_DTYPE)

    return dict(wk_pair=pair_weights(wk), wv_pair=pair_weights(wv),
                w1a=w1[:CHUNK_WIDTH].astype(MXU_DTYPE), w1b=w1[CHUNK_WIDTH:].astype(MXU_DTYPE),
                w2=w2.reshape(HID_WIDTH, ROW_WIDTH).astype(MXU_DTYPE),
                pe_a=pad8(pe[:CMP_STRIDE].reshape(1, CHUNK_WIDTH)), pe_b=pad8(pe[CMP_STRIDE:].reshape(1, CHUNK_WIDTH)))


def _cmp_to_sel_map(n_cmp, n_rows, n_sel):
    n_cols = -(-n_sel // LANES) * LANES
    c0 = np.arange(n_rows) * CMP_STRIDE
    s0 = np.arange(n_cols) * SEL_LEN
    m = (c0[:, None] < s0[None, :] + SEL_LEN) & (s0[None, :] < c0[:, None] + CMP_LEN)
    m &= (np.arange(n_rows) < n_cmp)[:, None] & (np.arange(n_cols) < n_sel)[None, :]
    return jnp.asarray(m.astype(np.float32))


def _masked_softmax_rows(s, mask):
    s = jnp.where(mask, s, MASK_NEG)
    m = jnp.max(s, axis=-1, keepdims=True)
    e = jnp.where(mask, jnp.exp(s - m), 0.0)
    return e / jnp.maximum(jnp.sum(e, axis=-1, keepdims=True), 1e-30)


def _biased_softmax_rows(s):
    m = jnp.maximum(jnp.max(s, axis=-1, keepdims=True), M_INIT)
    e = jnp.exp(s - m)
    return e / jnp.maximum(jnp.sum(e, axis=-1, keepdims=True), 1e-30)


def _flash_update(slot, rows, parts, m_ref, l_ref, acc_ref):
    m_prev = m_ref[slot, rows, :]
    s_max = functools.reduce(jnp.maximum, [s for s, _ in parts])
    m_new = jnp.maximum(m_prev, jnp.max(s_max, axis=-1, keepdims=True))
    alpha = jnp.exp(m_prev - m_new)
    acc_new = alpha * acc_ref[slot, rows, :]
    p_sum = None
    for s, v in parts:
        p = jnp.exp(s - jnp.concatenate([m_new] * (s.shape[1] // LANES), axis=1))
        p_sum = p if p_sum is None else p_sum + p
        acc_new = acc_new + _dot(p, v)
    l_ref[slot, rows, :] = alpha * l_ref[slot, rows, :] + jnp.sum(p_sum, axis=-1, keepdims=True)
    acc_ref[slot, rows, :] = acc_new
    m_ref[slot, rows, :] = m_new


SEL, WIN = 0, 1


def _block_bias_t(score_ref, rank_ref, n_sel, last_valid):
    rows, tq = score_ref.shape
    n_groups = rows // 8
    rank_ref[...] = jnp.zeros((rows, tq), F32)
    sub = lax.broadcasted_iota(jnp.int32, (8, tq), 0)
    for c in range(-(-n_sel // 8)):
        @pl.when(c * 8 <= last_valid)
        def _():
            groups = [score_ref[v * 8:(v + 1) * 8, :] for v in range(n_groups)]
            rank = [rank_ref[v * 8:(v + 1) * 8, :] for v in range(n_groups)]
            for j in range(c * 8, min(c * 8 + 8, n_sel)):
                r = jnp.broadcast_to(score_ref[j:j + 1, :], (8, tq))
                for v, s in enumerate(groups):
                    if v * 8 > j:
                        ahead = r >= s
                    elif v * 8 + 7 <= j:
                        ahead = r > s
                    else:
                        ahead = (r > s) | ((r == s) & (sub + v * 8 > j))
                    rank[v] = rank[v] + jnp.where(ahead, 1.0, 0.0)
            for v in range(n_groups):
                rank_ref[v * 8:(v + 1) * 8, :] = rank[v]
    return jnp.where(rank_ref[...] < float(min(SEL_TOP, n_sel)), 0.0, MASK_NEG)


def _nsa_prompt_kernel(q_ref, g_ref, kc_ref, vc_ref, selk_ref, selv_ref, wink_ref, winv_ref, map_ref, o_ref,
                       sc_ref, rank_ref, qa_ref, oc_ref, m_ref, l_ref, acc_ref, *, tq, nc, n_sel):
    qi = pl.program_id(1)
    tk = tq
    nw = WINDOW // tk
    q0 = qi * tq
    qpos = q0 + lax.broadcasted_iota(jnp.int32, (tq, 1), 0)
    ncp = vc_ref.shape[1]
    n_idx = lax.broadcasted_iota(jnp.int32, (1, ncp), 1)
    cbias = jnp.where((CMP_STRIDE * n_idx + (CMP_LEN - 1) <= qpos) & (n_idx < nc), 0.0, MASK_NEG)
    last_valid = lax.div(q0 + tq - 1, SEL_LEN)
    vc = vc_ref[0]
    nsr = sc_ref.shape[0]
    blk_t = lax.broadcasted_iota(jnp.int32, (nsr, tq), 0)
    cur_t = lax.div(q0 + lax.broadcasted_iota(jnp.int32, (nsr, tq), 1), SEL_LEN)
    valid_t = blk_t <= cur_t
    forced_t = valid_t & ((blk_t == 0) | (blk_t >= cur_t - 1))
    gates = g_ref[0]
    lane_half = lax.div(lax.broadcasted_iota(jnp.int32, (tq, LANES), 1), NSA_HEAD_DIM)
    r_idx = lax.broadcasted_iota(jnp.int32, (tq, tk), 0)
    c_idx = lax.broadcasted_iota(jnp.int32, (tq, tk), 1)
    causal_bias = jnp.where(c_idx <= r_idx, 0.0, MASK_NEG)
    oldest_bias = jnp.where(c_idx > r_idx, 0.0, MASK_NEG)

    qslice = lambda h: q_ref[0, :, h * LANES:(h + 1) * LANES]
    group_rows = lambda gq: slice(gq * tq, (gq + 1) * tq)
    slot = lambda k, branch: k * 2 + branch

    for k in range(NSA_KV_HEADS):
        kc = kc_ref[0, k]
        psum = jnp.zeros((tq, ncp), F32)
        for gq in range(NSA_GQ):
            p = _biased_softmax_rows(_dot_nt(qslice(k * NSA_GQ + gq), kc) + cbias)
            psum = psum + p
            oc_ref[k * NSA_GQ + gq] = _dot(p, vc)
        psum_hi = psum.astype(MXU_DTYPE)
        psum_lo = psum - psum_hi.astype(F32)
        imp = _dot(psum_hi, map_ref[...]) + _dot(psum_lo, map_ref[...])
        sc_ref[...] = jnp.where(forced_t, FORCE_SCORE, jnp.where(valid_t, imp.T[0:nsr, :], -jnp.inf))

        bias_t = jnp.concatenate([jnp.zeros((NSA_HEAD_DIM, tq), F32), _block_bias_t(sc_ref, rank_ref, n_sel, last_valid)]
                                 + ([jnp.zeros((NSA_HEAD_DIM - nsr, tq), F32)] if nsr < NSA_HEAD_DIM else []), axis=0)
        bias = bias_t.T.astype(MXU_DTYPE)
        for gq in range(NSA_GQ):
            qa_ref[slot(k, SEL), group_rows(gq), :] = qslice(k * NSA_GQ + gq) + bias
            qa_ref[slot(k, WIN), group_rows(gq), :] = qslice(k * NSA_GQ + gq)
        for branch in (SEL, WIN):
            m_ref[slot(k, branch)] = jnp.full((NSA_GQ * tq, LANES), M_INIT, F32)
            l_ref[slot(k, branch)] = jnp.zeros((NSA_GQ * tq, LANES), F32)
            acc_ref[slot(k, branch)] = jnp.zeros((NSA_GQ * tq, LANES), F32)

    def attend(tiles, k_ref, v_ref, branch):
        for r0 in range(0, NSA_GQ * tq, 2 * tq):
            rows = slice(r0, r0 + 2 * tq)
            for k in range(NSA_KV_HEADS):
                q = qa_ref[slot(k, branch), rows, :]
                parts = []
                for j, bias_tile in tiles:
                    k0 = pl.multiple_of(j * tk, tk)
                    s = _dot_nt(q, k_ref[0, k, pl.ds(k0, tk), :])
                    if bias_tile is not None:
                        s = s + jnp.concatenate([bias_tile] * 2, axis=0)
                    parts.append((s, v_ref[0, pl.ds(k0, tk), :]))
                _flash_update(slot(k, branch), rows, parts, m_ref, l_ref, acc_ref)

    lax.fori_loop(0, lax.div(qi, 2),
                  lambda jj, c: (attend([(2 * jj, None), (2 * jj + 1, None)], selk_ref, selv_ref, SEL), c)[1], 0)

    @pl.when(lax.rem(qi, 2) == 1)
    def _():
        attend([(qi - 1, None), (qi, causal_bias)], selk_ref, selv_ref, SEL)

    @pl.when(lax.rem(qi, 2) == 0)
    def _():
        attend([(qi, causal_bias)], selk_ref, selv_ref, SEL)

    win_tiles = []
    for i in range(nw + 1):
        j = qi - nw + i
        edge = oldest_bias if i == 0 else causal_bias if i == nw else jnp.zeros((tq, tk), F32)
        win_tiles.append((jnp.maximum(j, 0), edge if i == nw else jnp.where(j >= 0, edge, MASK_NEG)))
    attend(win_tiles, wink_ref, winv_ref, WIN)

    for h in range(NSA_HEADS):
        k, gq = divmod(h, NSA_GQ)
        rows = group_rows(gq)
        o_s = acc_ref[slot(k, SEL), rows, :] / jnp.maximum(l_ref[slot(k, SEL), rows, :], 1e-30)
        o_w = acc_ref[slot(k, WIN), rows, :] / jnp.maximum(l_ref[slot(k, WIN), rows, :], 1e-30)
        c = 3 * h
        out = gates[:, c:c + 1] * oc_ref[h] + gates[:, c + 1:c + 2] * o_s + gates[:, c + 2:c + 3] * o_w
        o_ref[0, :, h * LANES:(h + 1) * LANES] = jnp.where(lane_half == k, out, 0.0).astype(o_ref.dtype)


def nsa_prompt(qp, gates, kc, vc, selk, selv, wink, winv):
    b, t, _ = qp.shape
    tq = min(t, 256)
    assert WINDOW % tq == 0 and t % tq == 0
    ncp = vc.shape[1]
    nc = ncp - 1
    n_sel = -(-t // SEL_LEN)
    nsr = -(-n_sel // 8) * 8
    assert nsr <= NSA_HEAD_DIM
    cmap = _cmp_to_sel_map(nc, ncp, n_sel).astype(MXU_DTYPE)
    per_b = lambda a: pl.BlockSpec((1,) + a.shape[1:], lambda i, j: (i,) + (0,) * (a.ndim - 1))
    return pl.pallas_call(
        functools.partial(_nsa_prompt_kernel, tq=tq, nc=nc, n_sel=n_sel),
        grid=(b, t // tq),
        in_specs=[pl.BlockSpec((1, tq, QP_WIDTH), lambda i, j: (i, j, 0)),
                  pl.BlockSpec((1, tq, LANES), lambda i, j: (i, j, 0)),
                  per_b(kc), per_b(vc), per_b(selk), per_b(selv), per_b(wink), per_b(winv),
                  pl.BlockSpec(cmap.shape, lambda i, j: (0, 0))],
        out_specs=pl.BlockSpec((1, tq, QP_WIDTH), lambda i, j: (i, j, 0)),
        out_shape=jax.ShapeDtypeStruct((b, t, QP_WIDTH), MXU_DTYPE),
        scratch_shapes=[pltpu.VMEM((nsr, tq), F32),
                        pltpu.VMEM((nsr, tq), F32),
                        pltpu.VMEM((2 * NSA_KV_HEADS, NSA_GQ * tq, LANES), MXU_DTYPE),
                        pltpu.VMEM((NSA_HEADS, tq, LANES), F32),
                        pltpu.VMEM((2 * NSA_KV_HEADS, NSA_GQ * tq, LANES), F32),
                        pltpu.VMEM((2 * NSA_KV_HEADS, NSA_GQ * tq, LANES), F32),
                        pltpu.VMEM((2 * NSA_KV_HEADS, NSA_GQ * tq, LANES), F32)],
        compiler_params=_params(("parallel", "parallel"), big=True),
        name="nsa_prompt",
    )(qp, gates, kc, vc, selk, selv, wink, winv, cmap)


PAGE_CHUNKS = PAGE_SIZE // CMP_STRIDE
PAGE_BLOCKS = PAGE_SIZE // SEL_LEN
PAGE_GROUP = 16
CHUNK_PITCH = 24


def _pages_view(pool):
    depth, n_pool, page = pool.shape[:3]
    return pool.transpose(0, 1, 3, 4, 5, 2).reshape(depth, n_pool, 2, KV_WIDTH, page)


def _paged_hidden_kernel(pt_ref, pool_ref, wk_ref, wv_ref, ha_ref, hb_ref, buf_ref, rows_ref, sem,
                         *, layer, pg, n_steps):
    step = pl.program_id(0) * pl.num_programs(1) + pl.program_id(1)
    slot = lax.rem(step, 2)

    def page_copy(s, sl, r):
        return pltpu.make_async_copy(pool_ref.at[layer, pt_ref[s * pg + r]], buf_ref.at[sl, r], sem.at[sl])

    def start_all(s, sl):
        lax.fori_loop(0, pg, lambda r, c: (page_copy(s, sl, r).start(), c)[1], 0, unroll=8)

    @pl.when(step == 0)
    def _():
        start_all(step, slot)

    @pl.when(step + 1 < n_steps)
    def _():
        start_all(step + 1, 1 - slot)

    lax.fori_loop(0, pg, lambda r, c: (page_copy(step, slot, r).wait(), c)[1], 0, unroll=8)

    half = NSA_KV_HEADS * CMP_HIDDEN
    for g0 in range(0, pg, PAGE_GROUP):
        for r in range(g0, min(g0 + PAGE_GROUP, pg)):
            for kv in range(2):
                page_rows = buf_ref[slot, r, kv].T
                for c in range(PAGE_CHUNKS):
                    dst = (r * PAGE_CHUNKS + c) * CHUNK_PITCH
                    rows_ref[kv, dst:dst + CMP_STRIDE, :] = page_rows[c * CMP_STRIDE:(c + 1) * CMP_STRIDE, :]
        n = min(PAGE_GROUP, pg - g0) * PAGE_CHUNKS
        c0 = g0 * PAGE_CHUNKS
        for kv, w_ref in enumerate((wk_ref, wv_ref)):
            acc = jnp.zeros((n, 2 * half), F32)
            for q in range(CMP_STRIDE // 2):
                x = jnp.concatenate([rows_ref[kv, pl.ds(c0 * CHUNK_PITCH + 2 * q + i, n, stride=CHUNK_PITCH), :]
                                     for i in range(2)], axis=1)
                acc = acc + _dot(x, w_ref[q])
            ha_ref[0, c0:c0 + n, kv * half:(kv + 1) * half] = acc[:, 0:half]
            hb_ref[0, c0:c0 + n, kv * half:(kv + 1) * half] = acc[:, half:2 * half]


def paged_hidden(pool, page_table, cw, *, layer):
    b, n_pages = page_table.shape
    pg = math.gcd(n_pages, 64)
    pages = _pages_view(pool)
    full = lambda a: pl.BlockSpec(a.shape, lambda i, j, pt: (0,) * a.ndim)
    out = jax.ShapeDtypeStruct((b, n_pages * PAGE_CHUNKS, HID_WIDTH), F32)
    out_spec = pl.BlockSpec((1, pg * PAGE_CHUNKS, HID_WIDTH), lambda i, j, pt: (i, j, 0))
    ws = (cw['wk_pair'], cw['wv_pair'])
    return pl.pallas_call(
        functools.partial(_paged_hidden_kernel, layer=layer, pg=pg, n_steps=b * (n_pages // pg)),
        grid_spec=pltpu.PrefetchScalarGridSpec(
            num_scalar_prefetch=1,
            grid=(b, n_pages // pg),
            in_specs=[pl.BlockSpec(memory_space=pl.ANY)] + [full(a) for a in ws],
            out_specs=(out_spec, out_spec),
            scratch_shapes=[pltpu.VMEM((2, pg, 2, KV_WIDTH, PAGE_SIZE), F32),
                            pltpu.VMEM((2, pg * PAGE_CHUNKS * CHUNK_PITCH, KV_WIDTH), F32),
                            pltpu.SemaphoreType.DMA((2,))]),
        out_shape=(out, out),
        compiler_params=_params(("arbitrary", "arbitrary"), big=True),
        name="paged_hidden",
    )(page_table.reshape(-1), pages, *ws)


def _own_half_queries(q8):
    q = q8.astype(F32)
    own_first = lax.div(lax.broadcasted_iota(jnp.int32, q.shape, 0), NSA_GQ) == 0
    return jnp.where(own_first, q, pltpu.roll(q, NSA_HEAD_DIM, 1))


SEL_KEYS = SEL_TOP * PAGE_SIZE
NO_KEY = 2 ** 30


def _decode_cmp_kernel(ha_ref, hb_ref, hpe_ref, new_ref, q_ref, w1b0_ref, w2_ref, cm_ref, sa_ref, sb_ref, map_ref,
                       oc_ref, idx_ref, kpos_ref, hbs_ref, *, q_pos, n_sel):
    n = ha_ref.shape[1]
    hbs_ref[0:n, :] = hb_ref[0]
    new8 = jnp.concatenate([new_ref[0], jnp.zeros((7, ROW_WIDTH), F32)], axis=0)
    hbs_ref[n:n + 8, :] = _dot(new8, w1b0_ref[...])
    kc, vc = _compress_tail(ha_ref[0], hbs_ref[1:n + 1, :], hpe_ref[0:1, :], w2_ref,
                            cm_ref[...], sa_ref[...], sb_ref[...])
    q8 = _own_half_queries(q_ref[0])
    n_idx = lax.broadcasted_iota(jnp.int32, (1, n), 1)
    p = _masked_softmax_rows(_dot_nt(q8, kc), CMP_STRIDE * n_idx + (CMP_LEN - 1) <= q_pos)
    oc_ref[0] = _dot(p, vc)

    row = lax.broadcasted_iota(jnp.int32, p.shape, 0)
    psum = jnp.zeros(p.shape, F32)
    for k in range(NSA_KV_HEADS):
        grp = jnp.sum(jnp.where(lax.div(row, NSA_GQ) == k, p, 0.0), axis=0, keepdims=True)
        psum = jnp.where(row == k, grp, psum)
    imp = jnp.dot(psum, map_ref[...], precision=lax.Precision.HIGHEST, preferred_element_type=F32)
    nsp = imp.shape[1]
    blk = lax.broadcasted_iota(jnp.int32, (NSA_HEADS, nsp), 1)
    cur = q_pos // SEL_LEN
    forced = (blk == 0) | (blk >= cur - 1)
    score = jnp.where(blk < n_sel, jnp.where(forced, FORCE_SCORE, imp), -jnp.inf)
    blk_f = blk.astype(F32)
    lane = lax.broadcasted_iota(jnp.int32, (NSA_HEADS, LANES), 1)
    key = lax.broadcasted_iota(jnp.int32, (NSA_HEADS, SEL_KEYS), 1)
    key_slot = lax.div(key, PAGE_SIZE)
    key_off = key - key_slot * PAGE_SIZE
    idx = jnp.zeros((NSA_HEADS, LANES), jnp.int32)
    kpos = jnp.zeros((NSA_HEADS, SEL_KEYS), jnp.int32)
    for i in range(SEL_TOP):
        top = jnp.max(score, axis=1, keepdims=True)
        pick_f = jnp.min(jnp.where(score == top, blk_f, float(nsp)), axis=1, keepdims=True)
        pick = pick_f.astype(jnp.int32)
        idx = jnp.where(lane == i, pick, idx)
        pos = lax.div(pick, PAGE_BLOCKS) * PAGE_SIZE + key_off
        kpos = jnp.where(key_slot == i, jnp.where(lax.div(pos, SEL_LEN) == pick, pos, NO_KEY), kpos)
        score = jnp.where(blk == pick, -jnp.inf, score)
    idx_ref[0] = idx
    kpos_ref[0] = kpos


def decode_cmp(ha, hb, hpe, cmp_new, q8, cw, tables, cmap, *, q_pos, n_sel):
    b, n, _ = ha.shape
    cm, sa, sb = tables
    w1b0 = cw['w1b'][0:ROW_WIDTH]
    per_b = lambda a: pl.BlockSpec((1,) + a.shape[1:], lambda i: (i, 0, 0))
    full = lambda a: pl.BlockSpec(a.shape, lambda i: (0,) * a.ndim)
    out8 = lambda w_, dt: (jax.ShapeDtypeStruct((b, NSA_HEADS, w_), dt), pl.BlockSpec((1, NSA_HEADS, w_), lambda i: (i, 0, 0)))
    outs = (out8(LANES, F32), out8(LANES, jnp.int32), out8(SEL_KEYS, jnp.int32))
    return pl.pallas_call(
        functools.partial(_decode_cmp_kernel, q_pos=q_pos, n_sel=n_sel),
        grid=(b,),
        in_specs=[per_b(ha), per_b(hb), full(hpe), per_b(cmp_new), per_b(q8), full(w1b0), full(cw['w2']),
                  full(cm), full(sa), full(sb), full(cmap)],
        out_specs=tuple(o[1] for o in outs),
        out_shape=tuple(o[0] for o in outs),
        scratch_shapes=[pltpu.VMEM((n + 8, HID_WIDTH), F32)],
        compiler_params=_params(("parallel",), big=True),
        name="decode_cmp",
    )(ha, hb, hpe, cmp_new, q8, w1b0, cw['w2'], cm, sa, sb, cmap)


def _decode_attn_kernel(idx_ref, pt_ref, pool_ref, win_ref, seln_ref, winn_ref, q_ref, g_ref, oc_ref, kpos_ref,
                        o_ref, wst_ref, k_ref, v_ref, sem, *, layer, n_pages, q_pos):
    b = pl.program_id(0)
    n_past = n_pages * PAGE_BLOCKS
    keep = win_ref.shape[4]
    hd = NSA_HEAD_DIM

    def slot(k, n_):
        blk = idx_ref[(b * NSA_KV_HEADS + k) * SEL_TOP + n_]
        page = pt_ref[b * n_pages + jnp.minimum(lax.div(blk, PAGE_BLOCKS), n_pages - 1)]
        lanes = slice(n_ * PAGE_SIZE, (n_ + 1) * PAGE_SIZE)
        dims = slice(k * hd, (k + 1) * hd)
        return blk, lanes, (pltpu.make_async_copy(pool_ref.at[layer, page, 0, dims, :], k_ref.at[k, :, lanes], sem),
                            pltpu.make_async_copy(pool_ref.at[layer, page, 1, dims, :], v_ref.at[k, :, lanes], sem))

    for k in range(NSA_KV_HEADS):
        for n_ in range(SEL_TOP):
            blk, lanes, copies = slot(k, n_)

            @pl.when(blk < n_past)
            def _():
                for cp in copies:
                    cp.start()

            @pl.when(blk >= n_past)
            def _():
                k_ref[k, :, lanes] = jnp.zeros((hd, PAGE_SIZE), F32)
                v_ref[k, :, lanes] = jnp.zeros((hd, PAGE_SIZE), F32)

    row0 = lax.broadcasted_iota(jnp.int32, (LANES, ROW_WIDTH), 0) == 0
    win_new_t = jnp.where(row0, jnp.broadcast_to(winn_ref[0], (LANES, ROW_WIDTH)), 0.0).T
    last_lane = lax.broadcasted_iota(jnp.int32, (KV_WIDTH, keep), 1) == keep - 1
    state = []
    for kv in range(2):
        shifted = pltpu.roll(win_ref[0, 0, kv], keep - 1, 1)
        state.append(jnp.where(last_lane, win_new_t[kv * KV_WIDTH:(kv + 1) * KV_WIDTH, 0:1], shifted))
        wst_ref[0, kv] = state[kv]

    q8 = q_ref[0]
    qk = q8[:, 0:hd]
    row = lax.broadcasted_iota(jnp.int32, (NSA_HEADS, LANES), 0)
    lane_half = lax.div(lax.broadcasted_iota(jnp.int32, (NSA_HEADS, LANES), 1), hd)
    own_half = lane_half == lax.div(row, NSA_GQ)

    def to_half(o, k):
        z = jnp.zeros_like(o)
        return jnp.concatenate([o, z] if k == 0 else [z, o], axis=1)

    o_w = jnp.zeros((NSA_HEADS, LANES), F32)
    for k in range(NSA_KV_HEADS):
        s = _dot(qk, state[0][k * hd:(k + 1) * hd, :])
        e = jnp.exp(s - jnp.max(s, axis=1, keepdims=True))
        p = e / jnp.maximum(jnp.sum(e, axis=1, keepdims=True), 1e-30)
        o_w = jnp.where(lax.div(row, NSA_GQ) == k, to_half(_dot_nt(p, state[1][k * hd:(k + 1) * hd, :]), k), o_w)

    for k in range(NSA_KV_HEADS):
        for n_ in range(SEL_TOP):
            blk, lanes, copies = slot(k, n_)

            @pl.when(blk < n_past)
            def _():
                for cp in copies:
                    cp.wait()

    new_k = seln_ref[0][:, 0:KV_WIDTH]
    new_v = seln_ref[0][:, KV_WIDTH:ROW_WIDTH]
    s_new = jnp.sum(_own_half_queries(q8) * new_k, axis=1, keepdims=True)
    grp = lax.div(lax.broadcasted_iota(jnp.int32, (NSA_HEADS, 1), 0), NSA_GQ)
    o_s = jnp.zeros((NSA_HEADS, LANES), F32)
    e_new = jnp.zeros((NSA_HEADS, 1), F32)
    denom = jnp.ones((NSA_HEADS, 1), F32)
    for k in range(NSA_KV_HEADS):
        mask = kpos_ref[0, k:k + 1, :] < q_pos
        s = jnp.where(mask, _dot(qk, k_ref[k]), MASK_NEG)
        m = jnp.maximum(jnp.max(s, axis=1, keepdims=True), s_new)
        e = jnp.where(mask, jnp.exp(s - m), 0.0)
        en = jnp.exp(s_new - m)
        o_s = jnp.where(lax.div(row, NSA_GQ) == k, to_half(_dot_nt(e, v_ref[k]), k), o_s)
        e_new = jnp.where(grp == k, en, e_new)
        denom = jnp.where(grp == k, jnp.sum(e, axis=1, keepdims=True) + en, denom)
    o_s = (o_s + e_new * jnp.where(own_half, jnp.broadcast_to(new_v, (NSA_HEADS, LANES)), 0.0)) / denom

    g = g_ref[0]
    out = g[:, 0:1] * oc_ref[0] + g[:, 1:2] * o_s + g[:, 2:3] * o_w
    o_ref[0] = jnp.where(own_half, out, 0.0).astype(o_ref.dtype)


def decode_attn(idx, page_table, pool, win_cache, sel_new, win_new, q8, gates3, oc, kpos, *, layer, q_pos):
    depth = pool.shape[0]
    b, n_pages = page_table.shape
    keep = win_cache.shape[2]
    assert keep == WINDOW
    win = win_cache.transpose(0, 1, 3, 4, 5, 2).reshape(depth, b, 2, KV_WIDTH, keep)
    per_b = lambda a: pl.BlockSpec((1,) + a.shape[1:], lambda i, ix, pt: (i, 0, 0))
    idx_flat = idx[:, 0:NSA_KV_HEADS, 0:SEL_TOP].reshape(-1)
    return pl.pallas_call(
        functools.partial(_decode_attn_kernel, layer=layer, n_pages=n_pages, q_pos=q_pos),
        grid_spec=pltpu.PrefetchScalarGridSpec(
            num_scalar_prefetch=2,
            grid=(b,),
            in_specs=[pl.BlockSpec(memory_space=pl.ANY),
                      pl.BlockSpec((1, 1, 2, KV_WIDTH, keep), lambda i, ix, pt: (layer, i, 0, 0, 0)),
                      per_b(sel_new), per_b(win_new), per_b(q8), per_b(gates3), per_b(oc), per_b(kpos)],
            out_specs=(pl.BlockSpec((1, NSA_HEADS, LANES), lambda i, ix, pt: (i, 0, 0)),
                       pl.BlockSpec((1, 2, KV_WIDTH, keep), lambda i, ix, pt: (i, 0, 0, 0))),
            scratch_shapes=[pltpu.VMEM((NSA_KV_HEADS, NSA_HEAD_DIM, SEL_KEYS), F32),
                            pltpu.VMEM((NSA_KV_HEADS, NSA_HEAD_DIM, SEL_KEYS), F32),
                            pltpu.SemaphoreType.DMA(())]),
        out_shape=(jax.ShapeDtypeStruct((b, NSA_HEADS, LANES), MXU_DTYPE),
                   jax.ShapeDtypeStruct((b, 2, KV_WIDTH, keep), F32)),
        compiler_params=_params(("arbitrary",), big=True),
        name="decode_attn",
    )(idx_flat, page_table.reshape(-1), _pages_view(pool), win, sel_new, win_new, q8, gates3, oc, kpos)


def _merge_ln_kernel(x_ref, ya_ref, yb_ref, yc_ref, wgm_ref, wa_ref, wb_ref, wc_ref, wo_ref, lng_ref, lnb_ref, o_ref):
    x = x_ref[0]
    xb = x.astype(MXU_DTYPE)
    gate = lambda j: _sigmoid(jnp.dot(xb, wgm_ref[:, j * D_MODEL:(j + 1) * D_MODEL], preferred_element_type=F32))
    merged = (gate(0) * _dot(ya_ref[...], wa_ref[...]) + gate(1) * _dot(yb_ref[0], wb_ref[...])
              + gate(2) * _dot(yc_ref[0], wc_ref[...]))
    o_ref[0] = _layer_norm(DN_ALPHA * x + _dot(merged, wo_ref[...]), lng_ref[...], lnb_ref[...])


def merge_ln(x, ya, yb, yc, mw, lng, lnb):
    nb, t, d = x.shape
    tt = min(t, 256)
    row3 = lambda w_: pl.BlockSpec((1, tt, w_), lambda b, i: (b, i, 0))
    full = lambda a: pl.BlockSpec(a.shape, lambda b, i: (0,) * a.ndim)
    ws = (mw['w_gm'], mw['w_a'], mw['w_b'], mw['w_c'], mw['w_o'], lng, lnb)
    return pl.pallas_call(
        _merge_ln_kernel,
        grid=(nb, t // tt),
        in_specs=[row3(d), pl.BlockSpec((tt, S5_WIDTH), lambda b, i: (i, b)), row3(QP_WIDTH), row3(CONV_CH)]
                 + [full(a) for a in ws],
        out_specs=row3(d),
        out_shape=jax.ShapeDtypeStruct((nb, t, d), F32),
        compiler_params=_params(("parallel", "parallel"), big=True),
        name="merge_ln",
    )(x, ya, yb, yc, *ws)


def _layer_weights(w, l):
    bf = lambda a: a.astype(MXU_DTYPE)
    row = lambda a: a.reshape(1, -1).astype(F32)
    w_in = w['w_in'][l]
    w_u, w_q, w_kv, w_gn, w_conv, w_gm = jnp.split(w_in, IN_OFFSETS, axis=-1)
    head_half = (jnp.arange(NSA_HEADS) // NSA_GQ)[None, :, None]

    def pad_heads(a3, own_half):
        z = jnp.zeros_like(a3)
        lo = jnp.concatenate([a3, z], axis=-1)
        hi = jnp.concatenate([z, a3], axis=-1)
        return (jnp.where(head_half == 0, lo, hi) if own_half else lo).reshape(a3.shape[0], QP_WIDTH)

    w_qp = pad_heads(w_q.reshape(D_MODEL, NSA_HEADS, NSA_HEAD_DIM), False)
    w_gpad = jnp.concatenate([w_gn, jnp.zeros((D_MODEL, LANES - w_gn.shape[1]), F32)], axis=1)
    w_proj = bf(jnp.concatenate([w_u, w_qp, w_kv, w_gpad, w_conv], axis=1))
    assert w_proj.shape[1] == C_END
    w_b = pad_heads(w['w_proj_b'][l].T.reshape(D_MODEL, NSA_HEADS, NSA_HEAD_DIM), True).T
    conv_w = jnp.concatenate([w['conv_w'][l], jnp.zeros((8 - CONV_K, CONV_CH), F32)], axis=0)
    gu = lambda name: jnp.split(w[name][l], 2, axis=-1)
    return dict(
        ffn1=(bf(gu('ffn1_w_gu')[0]), bf(gu('ffn1_w_gu')[1]), bf(w['ffn1_w_down'][l])),
        ffn2=(bf(gu('ffn2_w_gu')[0]), bf(gu('ffn2_w_gu')[1]), bf(w['ffn2_w_down'][l])),
        ln=[(row(w['ln_g'][l, j]), row(w['ln_b'][l, j])) for j in range(3)],
        w_proj=w_proj, conv_w=conv_w,
        s5=_s5_params(w['s5_lambda_re'][l], w['s5_lambda_im'][l], w['s5_log_dt'][l], w['s5_b_re'][l], w['s5_b_im'][l],
                      w['s5_c_re'][l], w['s5_c_im'][l], w['s5_d'][l], w['s5_w_glu'][l], w['s5_b_glu'][l]),
        cmp=_compress_params(w['nsa_phi_k1'][l], w['nsa_phi_k2'][l], w['nsa_phi_v1'][l], w['nsa_phi_v2'][l],
                             w['nsa_pe_k'][l], w['nsa_pe_v'][l]),
        merge=dict(w_gm=bf(w_gm), w_a=bf(w['w_proj_a'][l]), w_b=bf(w_b), w_c=bf(w['w_proj_c'][l]), w_o=bf(w['w_o'][l])),
    )


def _prompt_trunk(x, layers):
    b, t, d = x.shape
    n_chunks = t // CMP_STRIDE
    tok_tables = _rope_tables(jnp.arange(t))
    cmp_tables = _rope_tables(jnp.arange(n_chunks) * CMP_STRIDE + (CMP_LEN - 1))
    zero_state = jnp.zeros((b, S5_FLAT), F32)
    keep = min(WINDOW, t)
    states, hpes = [], []
    x2 = x.reshape(b * t, d)
    for lw in layers:
        x2 = ffn_ln(x2, *lw['ffn1'], *lw['ln'][0])
        u, qp, cmp_rows, sel_t, win_t, gates, yc, conv_state, selk, selv, wink, winv, cmp_t = in_proj(
            x2.reshape(b, t, d), lw['w_proj'], tok_tables, lw['conv_w'], jnp.zeros((8, LANES), F32), decode=False)
        ya, s_re, s_im = s5_branch(u, zero_state, zero_state, lw['s5'], nb=b, wide=True)
        kc, vc, hpe = compress(cmp_rows.reshape(b, n_chunks, CHUNK_WIDTH), lw['cmp'], cmp_tables)
        hpes.append(hpe)
        yb = nsa_prompt(qp, gates, kc, vc, selk, selv, wink, winv)
        x3 = merge_ln(x2.reshape(b, t, d), ya, yb, yc, lw['merge'], *lw['ln'][1])
        x2 = ffn_ln(x3.reshape(b * t, d), *lw['ffn2'], *lw['ln'][2])
        rows = lambda a: a.reshape(b, 2, NSA_KV_HEADS, NSA_HEAD_DIM, -1).transpose(0, 4, 1, 2, 3)
        states.append((rows(cmp_t), rows(sel_t), rows(win_t[..., t - keep:]), conv_state,
                       s_re.reshape(b, S5_GROUPS, S5_STATE), s_im.reshape(b, S5_GROUPS, S5_STATE)))
    return x2.reshape(b, t, d), [jnp.stack(s, axis=0) for s in zip(*states)], hpes


def _sample_trunk(x, layers, hpes, cache_cmp, cache_sel, cache_win, cache_conv, s5_re, s5_im, page_table):
    bs, dec_seq, d = x.shape
    assert dec_seq == 1
    n_pages = page_table.shape[1]
    q_pos = n_pages * PAGE_SIZE
    n_blocks = n_pages * PAGE_CHUNKS
    n_sel = q_pos // SEL_LEN + 1
    assert cache_win.shape[2] == min(WINDOW, q_pos) and n_sel >= SEL_TOP
    tok_tables = _rope_tables(jnp.full((bs,), q_pos))
    cmp_tables = _rope_tables(jnp.arange(n_blocks) * CMP_STRIDE + (CMP_LEN - 1))
    cmap = _cmp_to_sel_map(n_blocks, n_blocks, n_sel)
    kv_shape = (2, NSA_KV_HEADS, NSA_HEAD_DIM)
    states = []
    x2 = x.reshape(bs, d)
    for l, lw in enumerate(layers):
        x2 = ffn_ln(x2, *lw['ffn1'], *lw['ln'][0])
        u, qp, cmp_new, sel_new, win_new, gates, yc, conv_state = in_proj(
            x2[None], lw['w_proj'], tok_tables, lw['conv_w'], cache_conv[l].reshape(bs, (CONV_K - 1) * CONV_CH),
            decode=True)
        ya, s_re, s_im = s5_branch(u, s5_re[l].reshape(bs, S5_FLAT), s5_im[l].reshape(bs, S5_FLAT), lw['s5'], nb=bs,
                                   wide=False)
        ha, hb = paged_hidden(cache_cmp, page_table, lw['cmp'], layer=l)
        hpe = hpes[l]
        q8 = qp.reshape(bs, NSA_HEADS, LANES)
        per_seq = lambda a: a.reshape(bs, 1, ROW_WIDTH)
        oc, idx, kpos = decode_cmp(ha, hb, hpe, per_seq(cmp_new), q8, lw['cmp'], cmp_tables, cmap,
                                   q_pos=q_pos, n_sel=n_sel)
        g3 = gates[0, :, 0:3 * NSA_HEADS].reshape(bs, NSA_HEADS, 3)
        g3 = jnp.concatenate([g3, jnp.zeros((bs, NSA_HEADS, LANES - 3), F32)], axis=-1)
        yb, win_state = decode_attn(idx, page_table, cache_sel, cache_win, per_seq(sel_new), per_seq(win_new), q8, g3,
                                    oc, kpos, layer=l, q_pos=q_pos)
        x3 = merge_ln(x2[None], ya, yb.reshape(1, bs, QP_WIDTH), yc, lw['merge'], *lw['ln'][1])
        x2 = ffn_ln(x3[0], *lw['ffn2'], *lw['ln'][2])
        states.append((cmp_new.reshape((bs, 1) + kv_shape), sel_new.reshape((bs, 1) + kv_shape),
                       win_state.reshape((bs,) + kv_shape + (-1,)).transpose(0, 4, 1, 2, 3),
                       conv_state.reshape(bs, CONV_K - 1, CONV_CH),
                       s_re.reshape(bs, S5_GROUPS, S5_STATE), s_im.reshape(bs, S5_GROUPS, S5_STATE)))
    return x2.reshape(bs, 1, d), [jnp.stack(s, axis=0) for s in zip(*states)]


def kernel(x_prompt, x_sample, cache_cmp_kv, cache_sel_kv, cache_win_kv, cache_conv, state_s5_re, state_s5_im, page_table, ln_g, ln_b, ffn1_w_gu, ffn1_w_down, ffn2_w_gu, ffn2_w_down, w_in, s5_lambda_re, s5_lambda_im, s5_log_dt, s5_b_re, s5_b_im, s5_c_re, s5_c_im, s5_d, s5_w_glu, s5_b_glu, nsa_pe_k, nsa_pe_v, nsa_phi_k1, nsa_phi_k2, nsa_phi_v1, nsa_phi_v2, conv_w, w_proj_a, w_proj_b, w_proj_c, w_o):
    w = dict(ln_g=ln_g, ln_b=ln_b, ffn1_w_gu=ffn1_w_gu, ffn1_w_down=ffn1_w_down, ffn2_w_gu=ffn2_w_gu,
             ffn2_w_down=ffn2_w_down, w_in=w_in, s5_lambda_re=s5_lambda_re, s5_lambda_im=s5_lambda_im,
             s5_log_dt=s5_log_dt, s5_b_re=s5_b_re, s5_b_im=s5_b_im, s5_c_re=s5_c_re, s5_c_im=s5_c_im, s5_d=s5_d,
             s5_w_glu=s5_w_glu, s5_b_glu=s5_b_glu, nsa_pe_k=nsa_pe_k, nsa_pe_v=nsa_pe_v, nsa_phi_k1=nsa_phi_k1,
             nsa_phi_k2=nsa_phi_k2, nsa_phi_v1=nsa_phi_v1, nsa_phi_v2=nsa_phi_v2, conv_w=conv_w,
             w_proj_a=w_proj_a, w_proj_b=w_proj_b, w_proj_c=w_proj_c, w_o=w_o)
    layers = [_layer_weights(w, l) for l in range(DEPTH)]
    y_prompt, (p_cmp, p_sel, p_win, p_conv, p_re, p_im), hpes = _prompt_trunk(x_prompt, layers)
    y_sample, (s_cmp, s_sel, s_win, s_conv, s_re, s_im) = _sample_trunk(
        x_sample, layers, hpes, cache_cmp_kv, cache_sel_kv, cache_win_kv, cache_conv, state_s5_re, state_s5_im, page_table)
    return (y_prompt, y_sample, p_cmp, s_cmp, p_sel, s_sel, p_win, s_win, p_conv, s_conv, p_re, s_re, p_im, s_im)
```

```python
import functools
import math

import numpy as np
import jax
import jax.numpy as jnp
from jax import lax
from jax.experimental import pallas as pl
from jax.experimental.pallas import tpu as pltpu

F32 = jnp.float32
MXU_DTYPE = jnp.bfloat16

D_MODEL = 1024
DEPTH = 2
PAGE_SIZE = 128
DN_ALPHA = (2.0 * DEPTH) ** 0.25
LN_EPS = 1e-5
D_FF = 2816
FFN_RES = 0.5
S5_WIDTH = 512
S5_GROUP = 16
S5_GROUPS = S5_WIDTH // S5_GROUP
S5_STATE = 64
S5_FLAT = S5_GROUPS * S5_STATE
NSA_HEADS = 8
NSA_KV_HEADS = 2
NSA_HEAD_DIM = 64
NSA_GQ = NSA_HEADS // NSA_KV_HEADS
CMP_LEN = 32
CMP_STRIDE = 16
CMP_HIDDEN = 128
SEL_LEN = 64
SEL_TOP = 16
WINDOW = 512
ROPE_THETA = 500000.0
ROT_DIM = NSA_HEAD_DIM // 4
ROT_HALF = ROT_DIM // 2
FORCE_SCORE = 1e9
CONV_CH = 512
CONV_K = 3
N_BRANCH = 3
KV_WIDTH = NSA_KV_HEADS * NSA_HEAD_DIM
ROW_WIDTH = 2 * KV_WIDTH
IN_WIDTHS = (S5_WIDTH, NSA_HEADS * NSA_HEAD_DIM, 6 * KV_WIDTH, 3 * NSA_HEADS, 3 * CONV_CH, N_BRANCH * D_MODEL)
IN_OFFSETS = tuple(int(o) for o in np.cumsum(IN_WIDTHS)[:-1])

LANES = 128
V7X_VMEM_BYTES = 64 * 1024 * 1024
VMEM_LIMIT = (V7X_VMEM_BYTES * 7) // 8

QP_WIDTH = NSA_HEADS * LANES
MASK_NEG = -1e30
M_INIT = -1e29

C_U = 0
C_Q = C_U + S5_WIDTH
C_KV = C_Q + QP_WIDTH
C_G = C_KV + 6 * KV_WIDTH
C_CONV = C_G + LANES
C_END = C_CONV + 3 * CONV_CH


def _sigmoid(x):
    return 1.0 / (1.0 + jnp.exp(-x))


def _gelu_tanh(x):
    return 0.5 * x * (1.0 + jnp.tanh(math.sqrt(2.0 / math.pi) * (x + 0.044715 * (x * x * x))))


def _layer_norm(x, g, b):
    mu = jnp.mean(x, axis=-1, keepdims=True)
    xc = x - mu
    var = jnp.mean(xc * xc, axis=-1, keepdims=True)
    return xc * lax.rsqrt(var + LN_EPS) * g + b


def _dot(a, b):
    return jnp.dot(a.astype(MXU_DTYPE), b.astype(MXU_DTYPE), preferred_element_type=F32)


def _dot_nt(a, b):
    return lax.dot_general(a.astype(MXU_DTYPE), b.astype(MXU_DTYPE), (((1,), (1,)), ((), ())),
                           preferred_element_type=F32)


def _rope_lanes(x, cm, sa, sb):
    return x * cm + pltpu.roll(x, LANES - ROT_HALF, 1) * sa + pltpu.roll(x, ROT_HALF, 1) * sb


def _rope_tables(pos):
    inv_freq = ROPE_THETA ** (-jnp.arange(ROT_HALF, dtype=F32) / ROT_HALF)
    ang = pos.astype(F32)[:, None] * inv_freq
    cos, sin = jnp.cos(ang), jnp.sin(ang)
    n = pos.shape[0]
    rest = NSA_HEAD_DIM - ROT_DIM
    cm = jnp.concatenate([cos, cos, jnp.ones((n, rest), F32)], axis=1)
    sa = jnp.concatenate([-sin, jnp.zeros((n, ROT_HALF + rest), F32)], axis=1)
    sb = jnp.concatenate([jnp.zeros((n, ROT_HALF), F32), sin, jnp.zeros((n, rest), F32)], axis=1)
    return tuple(jnp.tile(t, (1, LANES // NSA_HEAD_DIM)) for t in (cm, sa, sb))


def _params(sem, big=False):
    return pltpu.CompilerParams(dimension_semantics=sem, vmem_limit_bytes=VMEM_LIMIT if big else None)


FF_CHUNK = D_FF


def _ffn_ln_kernel(x_ref, wg_ref, wu_ref, wd_ref, lng_ref, lnb_ref, o_ref):
    x = x_ref[...]
    xb = x.astype(MXU_DTYPE)
    acc = jnp.zeros(x.shape, F32)
    for c in range(0, D_FF, FF_CHUNK):
        gate = jnp.dot(xb, wg_ref[:, c:c + FF_CHUNK], preferred_element_type=F32)
        up = jnp.dot(xb, wu_ref[:, c:c + FF_CHUNK], preferred_element_type=F32)
        acc = acc + _dot(gate * _sigmoid(gate) * up, wd_ref[c:c + FF_CHUNK, :])
    o_ref[...] = _layer_norm(DN_ALPHA * x + FFN_RES * acc, lng_ref[...], lnb_ref[...])


def ffn_ln(x, wg, wu, wd, lng, lnb):
    n, d = x.shape
    tm = min(n, 512)
    resident = lambda a: pl.BlockSpec(a.shape, lambda i: (0, 0), pipeline_mode=pl.Buffered(1))
    return pl.pallas_call(
        _ffn_ln_kernel,
        grid=(n // tm,),
        in_specs=[pl.BlockSpec((tm, d), lambda i: (i, 0)), resident(wg), resident(wu), resident(wd),
                  resident(lng), resident(lnb)],
        out_specs=pl.BlockSpec((tm, d), lambda i: (i, 0)),
        out_shape=jax.ShapeDtypeStruct((n, d), F32),
        compiler_params=_params(("parallel",), big=True),
        name="ffn_ln",
    )(x, wg, wu, wd, lng, lnb)


def _in_proj_kernel(x_ref, w_ref, cm_ref, sa_ref, sb_ref, cw_ref, cprev_ref,
                    u_ref, q_ref, cmp_ref, sel_ref, win_ref, g_ref, yc_ref, cst_ref, *rest, decode):
    if decode:
        (vbuf_ref,) = rest
    else:
        selk_ref, selv_ref, wink_ref, winv_ref, cmpt_ref, vbuf_ref = rest
    x = x_ref[0]
    rows = x.shape[0]
    z = _dot(x, w_ref[...])
    cm, sa, sb = cm_ref[...], sa_ref[...], sb_ref[...]

    u_ref[...] = z[:, C_U:C_U + S5_WIDTH]
    scale = NSA_HEAD_DIM ** -0.5
    for h in range(NSA_HEADS):
        c = C_Q + h * LANES
        q_ref[0, :, h * LANES:(h + 1) * LANES] = (_rope_lanes(z[:, c:c + LANES], cm, sa, sb) * scale).astype(q_ref.dtype)
    cmp_ref[0] = z[:, C_KV:C_KV + ROW_WIDTH]
    ks = _rope_lanes(z[:, C_KV + 2 * KV_WIDTH:C_KV + 3 * KV_WIDTH], cm, sa, sb)
    vs = z[:, C_KV + 3 * KV_WIDTH:C_KV + 4 * KV_WIDTH]
    kw = _rope_lanes(z[:, C_KV + 4 * KV_WIDTH:C_KV + 5 * KV_WIDTH], cm, sa, sb)
    vw = z[:, C_KV + 5 * KV_WIDTH:C_KV + 6 * KV_WIDTH]
    g_ref[0] = _sigmoid(z[:, C_G:C_G + LANES])
    if decode:
        sel_ref[0, :, 0:KV_WIDTH] = ks
        sel_ref[0, :, KV_WIDTH:ROW_WIDTH] = vs
        win_ref[0, :, 0:KV_WIDTH] = kw
        win_ref[0, :, KV_WIDTH:ROW_WIDTH] = vw
    else:
        for kv, (s_kv, w_kv) in enumerate(((ks, kw), (vs, vw))):
            cmpt_ref[0, kv] = z[:, C_KV + kv * KV_WIDTH:C_KV + (kv + 1) * KV_WIDTH].T
            sel_ref[0, kv] = s_kv.T
            win_ref[0, kv] = w_kv.T
        lane = lax.broadcasted_iota(jnp.int32, (rows, LANES), 1)
        pos = pl.program_id(1) * rows + lax.broadcasted_iota(jnp.int32, (rows, LANES), 0)
        first = lane < NSA_HEAD_DIM
        tag = jnp.where(lane - NSA_HEAD_DIM == lax.div(pos, SEL_LEN), 1.0, 0.0)
        for k in range(NSA_KV_HEADS):
            ks_k = ks if k == 0 else pltpu.roll(ks, NSA_HEAD_DIM, 1)
            kw_k = kw if k == 0 else pltpu.roll(kw, NSA_HEAD_DIM, 1)
            selk_ref[0, k] = jnp.where(first, ks_k, tag).astype(selk_ref.dtype)
            wink_ref[0, k] = jnp.where(first, kw_k, 0.0).astype(wink_ref.dtype)
        selv_ref[0] = vs.astype(selv_ref.dtype)
        winv_ref[0] = vw.astype(winv_ref.dtype)

    cb = z[:, C_CONV:C_CONV + CONV_CH]
    v = z[:, C_CONV + CONV_CH:C_CONV + 2 * CONV_CH] * z[:, C_CONV + 2 * CONV_CH:C_CONV + 3 * CONV_CH]
    w0, w1, w2 = cw_ref[0:1, :], cw_ref[1:2, :], cw_ref[2:3, :]
    if decode:
        p0 = cprev_ref[:, 0:CONV_CH]
        p1 = cprev_ref[:, CONV_CH:2 * CONV_CH]
        yc_ref[0] = (cb * (w0 * p0 + w1 * p1 + w2 * v)).astype(yc_ref.dtype)
        cst_ref[:, 0:CONV_CH] = p1
        cst_ref[:, CONV_CH:2 * CONV_CH] = v
    else:
        @pl.when(pl.program_id(1) == 0)
        def _():
            vbuf_ref[0:8, :] = jnp.zeros((8, CONV_CH), F32)

        vbuf_ref[8:8 + rows, :] = v
        conv = w0 * vbuf_ref[6:6 + rows, :] + w1 * vbuf_ref[7:7 + rows, :] + w2 * v
        yc_ref[0] = (cb * conv).astype(yc_ref.dtype)
        last = vbuf_ref[rows:rows + 8, :]
        vbuf_ref[0:8, :] = last
        cst_ref[0] = last[6:8, :]


def in_proj(x, w, tables, conv_w, conv_prev, *, decode):
    nb, t, d = x.shape
    tt = t if decode else min(t, 512)
    grid = (nb, t // tt)
    keep = min(WINDOW, t)
    assert decode or (keep % tt == 0 and t % tt == 0)
    first_kept = (t - keep) // tt
    cm, sa, sb = tables
    row3 = lambda w_: pl.BlockSpec((1, tt, w_), lambda b, i: (b, i, 0))
    tab = pl.BlockSpec((tt, LANES), lambda b, i: (i, 0))
    if decode:
        cprev_spec = pl.BlockSpec((t, 2 * CONV_CH), lambda b, i: (0, 0))
        cst_spec = pl.BlockSpec((t, 2 * CONV_CH), lambda b, i: (0, 0))
        cst_shape = jax.ShapeDtypeStruct((t, 2 * CONV_CH), F32)
    else:
        cprev_spec = pl.BlockSpec((8, LANES), lambda b, i: (0, 0))
        cst_spec = pl.BlockSpec((1, CONV_K - 1, CONV_CH), lambda b, i: (b, 0, 0))
        cst_shape = jax.ShapeDtypeStruct((nb, CONV_K - 1, CONV_CH), F32)
    slabs = (jax.ShapeDtypeStruct((nb, 2, KV_WIDTH, t), F32),
             pl.BlockSpec((1, 2, KV_WIDTH, tt), lambda b, i: (b, 0, 0, i)))
    win_slabs = (jax.ShapeDtypeStruct((nb, 2, KV_WIDTH, keep), F32),
                 pl.BlockSpec((1, 2, KV_WIDTH, tt), lambda b, i: (b, 0, 0, jnp.maximum(i - first_kept, 0))))
    rows_out = (jax.ShapeDtypeStruct((nb, t, ROW_WIDTH), F32), row3(ROW_WIDTH))
    sel_out, win_out = (rows_out, rows_out) if decode else (slabs, win_slabs)
    out_shape = (
        jax.ShapeDtypeStruct((t, nb * S5_WIDTH), F32),
        jax.ShapeDtypeStruct((nb, t, QP_WIDTH), MXU_DTYPE),
        rows_out[0],
        sel_out[0], win_out[0],
        jax.ShapeDtypeStruct((nb, t, LANES), F32),
        jax.ShapeDtypeStruct((nb, t, CONV_CH), MXU_DTYPE),
        cst_shape,
    )
    out_specs = (pl.BlockSpec((tt, S5_WIDTH), lambda b, i: (i, b)), row3(QP_WIDTH), rows_out[1], sel_out[1], win_out[1],
                 row3(LANES), row3(CONV_CH), cst_spec)
    if not decode:
        assert t <= NSA_HEAD_DIM * SEL_LEN
        per_head = (jax.ShapeDtypeStruct((nb, NSA_KV_HEADS, t, LANES), MXU_DTYPE),
                    pl.BlockSpec((1, NSA_KV_HEADS, tt, LANES), lambda b, i: (b, 0, i, 0)))
        packed = (jax.ShapeDtypeStruct((nb, t, KV_WIDTH), MXU_DTYPE), row3(KV_WIDTH))
        extra = (per_head, packed, per_head, packed, slabs)
        out_shape += tuple(e[0] for e in extra)
        out_specs += tuple(e[1] for e in extra)
    return pl.pallas_call(
        functools.partial(_in_proj_kernel, decode=decode),
        grid=grid,
        in_specs=[row3(d), pl.BlockSpec((d, C_END), lambda b, i: (0, 0)), tab, tab, tab,
                  pl.BlockSpec((8, CONV_CH), lambda b, i: (0, 0)), cprev_spec],
        out_specs=out_specs,
        out_shape=out_shape,
        scratch_shapes=[pltpu.VMEM((tt + 8, CONV_CH), F32)],
        compiler_params=_params(("parallel", "arbitrary"), big=True),
        name="in_proj",
    )(x, w, cm, sa, sb, conv_w, conv_prev)


S5_COLS = 512


def _s5_kernel(u_ref, s0r_ref, s0i_ref, ar_ref, ai_ref, br_ref, bi_ref, cr_ref, ci_ref, d_ref, wglu_ref, bglu_ref,
               y_ref, sr_ref, si_ref, xr_ref, xi_ref, *shuffle, nb, tc, wide):
    @pl.when(pl.program_id(0) == 0)
    def _():
        sr_ref[...] = s0r_ref[...]
        si_ref[...] = s0i_ref[...]

    slabs = S5_WIDTH // LANES
    if wide:
        uio_ref, yio_ref = shuffle
        for b in range(nb):
            for c in range(slabs):
                lanes = slice(b * S5_WIDTH + c * LANES, b * S5_WIDTH + (c + 1) * LANES)
                uio_ref[c, pl.ds(b, tc, stride=nb), :] = u_ref[:, lanes]
        u = jnp.concatenate([uio_ref[c] for c in range(slabs)], axis=1)
    else:
        u = u_ref[...]
    ub = u.astype(MXU_DTYPE)
    hw, hf = S5_WIDTH // 2, S5_FLAT // 2
    for h in range(2):
        xr_ref[:, h * hf:(h + 1) * hf] = jnp.dot(ub[:, h * hw:(h + 1) * hw], br_ref[h], preferred_element_type=F32)
        xi_ref[:, h * hf:(h + 1) * hf] = jnp.dot(ub[:, h * hw:(h + 1) * hw], bi_ref[h], preferred_element_type=F32)

    for c in range(S5_FLAT // S5_COLS):
        cols = slice(c * S5_COLS, (c + 1) * S5_COLS)
        ar = jnp.broadcast_to(ar_ref[:, cols], (nb, S5_COLS))
        ai = jnp.broadcast_to(ai_ref[:, cols], (nb, S5_COLS))

        def step(t, carry):
            sr, si = carry
            rows = pl.ds(pl.multiple_of(t * nb, nb), nb)
            nr = ar * sr - ai * si + xr_ref[rows, cols]
            ni = ar * si + ai * sr + xi_ref[rows, cols]
            xr_ref[rows, cols] = nr
            xi_ref[rows, cols] = ni
            return nr, ni

        sr, si = lax.fori_loop(0, tc, step, (sr_ref[:, cols], si_ref[:, cols]), unroll=min(tc, 8))
        sr_ref[:, cols] = sr
        si_ref[:, cols] = si

    y = jnp.concatenate([_dot(xr_ref[:, h * hf:(h + 1) * hf], cr_ref[h]) - _dot(xi_ref[:, h * hf:(h + 1) * hf], ci_ref[h])
                         for h in range(2)], axis=1) + d_ref[...] * u
    y = _gelu_tanh(y)
    y = y * _sigmoid(_dot(y, wglu_ref[...]) + bglu_ref[...])
    if wide:
        for c in range(slabs):
            yio_ref[c] = y[:, c * LANES:(c + 1) * LANES]
        for b in range(nb):
            for c in range(slabs):
                lanes = slice(b * S5_WIDTH + c * LANES, b * S5_WIDTH + (c + 1) * LANES)
                y_ref[:, lanes] = yio_ref[c, pl.ds(b, tc, stride=nb), :].astype(y_ref.dtype)
    else:
        y_ref[...] = y.astype(y_ref.dtype)


def s5_branch(u, s0r, s0i, p, *, nb, wide):
    if wide:
        t = u.shape[0]
        tc = min(t, 128)
        block = (tc, nb * S5_WIDTH)
        shuffle = [pltpu.VMEM((S5_WIDTH // LANES, tc * nb, LANES), F32)] * 2
    else:
        t = u.shape[0] // nb
        tc = min(t, 128)
        block = (tc * nb, S5_WIDTH)
        shuffle = []
    full = lambda a: pl.BlockSpec(a.shape, lambda i: (0,) * a.ndim)
    args = (u, s0r, s0i, p['a_re'], p['a_im'], p['bb_re'], p['bb_im'], p['c_re'], p['c_im'], p['d'], p['w_glu'], p['b_glu'])
    return pl.pallas_call(
        functools.partial(_s5_kernel, nb=nb, tc=tc, wide=wide),
        grid=(t // tc,),
        in_specs=[pl.BlockSpec(block, lambda i: (i, 0))] + [full(a) for a in args[1:]],
        out_specs=(pl.BlockSpec(block, lambda i: (i, 0)),
                   pl.BlockSpec((nb, S5_FLAT), lambda i: (0, 0)),
                   pl.BlockSpec((nb, S5_FLAT), lambda i: (0, 0))),
        out_shape=(jax.ShapeDtypeStruct(u.shape, MXU_DTYPE),
                   jax.ShapeDtypeStruct((nb, S5_FLAT), F32),
                   jax.ShapeDtypeStruct((nb, S5_FLAT), F32)),
        scratch_shapes=[pltpu.VMEM((tc * nb, S5_FLAT), F32), pltpu.VMEM((tc * nb, S5_FLAT), F32)] + shuffle,
        compiler_params=_params(("arbitrary",), big=True),
        name="s5",
    )(*args)


def _s5_params(lam_re, lam_im, log_dt, b_re, b_im, c_re, c_im, d_skip, w_glu, b_glu):
    dt = jnp.exp(log_dt.astype(F32))[:, None]
    lr, li = lam_re.astype(F32), lam_im.astype(F32)
    mag = jnp.exp(lr * dt)
    a_re = mag * jnp.cos(li * dt)
    a_im = mag * jnp.sin(li * dt)
    den = lr * lr + li * li
    r_re = ((a_re - 1.0) * lr + a_im * li) / den
    r_im = (a_im * lr - (a_re - 1.0) * li) / den
    bb_re = r_re[..., None] * b_re - r_im[..., None] * b_im
    bb_im = r_re[..., None] * b_im + r_im[..., None] * b_re
    eye = jnp.eye(S5_GROUPS, dtype=F32)
    hw, hf = S5_WIDTH // 2, S5_FLAT // 2
    halves = lambda m: jnp.stack([m[:m.shape[0] // 2, :m.shape[1] // 2], m[m.shape[0] // 2:, m.shape[1] // 2:]])
    blk_in = lambda bb: halves(jnp.einsum('gpi,gh->gihp', bb, eye).reshape(S5_WIDTH, S5_FLAT)).astype(MXU_DTYPE)
    blk_out = lambda c: halves(jnp.einsum('gop,gh->gpho', c.astype(F32), eye).reshape(S5_FLAT, S5_WIDTH)).astype(MXU_DTYPE)
    return dict(a_re=a_re.reshape(1, S5_FLAT), a_im=a_im.reshape(1, S5_FLAT),
                bb_re=blk_in(bb_re), bb_im=blk_in(bb_im), c_re=blk_out(c_re), c_im=blk_out(c_im),
                d=d_skip.reshape(1, S5_WIDTH).astype(F32), w_glu=w_glu.astype(MXU_DTYPE),
                b_glu=b_glu.reshape(1, S5_WIDTH).astype(F32))


CHUNK_WIDTH = CMP_STRIDE * ROW_WIDTH
HID_WIDTH = 4 * CMP_HIDDEN


def _compress_tail(ha, hb_next, hpe, w2_ref, cm, sa, sb):
    kv = _dot(_gelu_tanh(ha + hb_next + hpe), w2_ref[...])
    return _rope_lanes(kv[:, 0:KV_WIDTH], cm, sa, sb), kv[:, KV_WIDTH:ROW_WIDTH]


def _compress_kernel(x_ref, w1a_ref, w1b_ref, pea_ref, peb_ref, w2_ref, cm_ref, sa_ref, sb_ref,
                     kc_ref, vc_ref, hpe_ref, hb_ref):
    x = x_ref[0].astype(MXU_DTYPE)
    n = x.shape[0]
    ha = jnp.dot(x, w1a_ref[...], preferred_element_type=F32)
    hb_ref[0:n, :] = jnp.dot(x, w1b_ref[...], preferred_element_type=F32)
    hb_ref[n:n + 8, :] = jnp.zeros((8, HID_WIDTH), F32)
    hpe = _dot(pea_ref[...], w1a_ref[...]) + _dot(peb_ref[...], w1b_ref[...])
    hpe_ref[...] = hpe
    kc, vc = _compress_tail(ha, hb_ref[1:n + 1, :], hpe[0:1, :], w2_ref, cm_ref[...], sa_ref[...], sb_ref[...])
    first = lax.broadcasted_iota(jnp.int32, kc.shape, 1) < NSA_HEAD_DIM
    kc_ref[0, 0] = jnp.where(first, kc, 0.0).astype(kc_ref.dtype)
    kc_ref[0, 1] = jnp.where(first, pltpu.roll(kc, NSA_HEAD_DIM, 1), 0.0).astype(kc_ref.dtype)
    vc_ref[0] = vc.astype(vc_ref.dtype)


def compress(rows, cw, tables):
    b, n, _ = rows.shape
    cm, sa, sb = tables
    full = lambda a: pl.BlockSpec(a.shape, lambda i: (0,) * a.ndim)
    args = (rows, cw['w1a'], cw['w1b'], cw['pe_a'], cw['pe_b'], cw['w2'], cm, sa, sb)
    return pl.pallas_call(
        _compress_kernel,
        grid=(b,),
        in_specs=[pl.BlockSpec((1, n, CHUNK_WIDTH), lambda i: (i, 0, 0))] + [full(a) for a in args[1:]],
        out_specs=(pl.BlockSpec((1, NSA_KV_HEADS, n, LANES), lambda i: (i, 0, 0, 0)),
                   pl.BlockSpec((1, n, KV_WIDTH), lambda i: (i, 0, 0)),
                   pl.BlockSpec((8, HID_WIDTH), lambda i: (0, 0))),
        out_shape=(jax.ShapeDtypeStruct((b, NSA_KV_HEADS, n, LANES), MXU_DTYPE),
                   jax.ShapeDtypeStruct((b, n, KV_WIDTH), MXU_DTYPE),
                   jax.ShapeDtypeStruct((8, HID_WIDTH), F32)),
        scratch_shapes=[pltpu.VMEM((n + 8, HID_WIDTH), F32)],
        compiler_params=_params(("arbitrary",), big=True),
        name="compress",
    )(*args)


def _compress_params(phi_k1, phi_k2, phi_v1, phi_v2, pe_k, pe_v):
    wk = phi_k1.reshape(CMP_LEN, NSA_HEAD_DIM, CMP_HIDDEN)
    wv = phi_v1.reshape(CMP_LEN, NSA_HEAD_DIM, CMP_HIDDEN)
    w = jnp.stack([wk, wk, wv, wv], axis=1)
    w1 = jnp.einsum('sjdu,jm->sjdmu', w, jnp.eye(4, dtype=F32)).reshape(CMP_LEN * ROW_WIDTH, HID_WIDTH)
    w2 = jnp.einsum('jud,jm->jumd', jnp.stack([phi_k2, phi_k2, phi_v2, phi_v2]), jnp.eye(4, dtype=F32))
    pe = jnp.concatenate([pe_k, pe_k, pe_v, pe_v], axis=1)
    pad8 = lambda r: jnp.concatenate([r, jnp.zeros((7, CHUNK_WIDTH), F32)], axis=0)

    def pair_weights(w3):
        halves = jnp.stack([w3[:CMP_STRIDE], w3[CMP_STRIDE:]]).reshape(2, CMP_STRIDE // 2, 2, NSA_HEAD_DIM, CMP_HIDDEN)
        full = jnp.einsum('pqidu,hg->qihdpgu', halves, jnp.eye(NSA_KV_HEADS, dtype=F32))
        return full.reshape(CMP_STRIDE // 2, 2 * KV_WIDTH, 2 * NSA_KV_HEADS * CMP_HIDDEN).astype(MXU_DTYPE)

    return dict(wk_pair=pair_weights(wk), wv_pair=pair_weights(wv),
                w1a=w1[:CHUNK_WIDTH].astype(MXU_DTYPE), w1b=w1[CHUNK_WIDTH:].astype(MXU_DTYPE),
                w2=w2.reshape(HID_WIDTH, ROW_WIDTH).astype(MXU_DTYPE),
                pe_a=pad8(pe[:CMP_STRIDE].reshape(1, CHUNK_WIDTH)), pe_b=pad8(pe[CMP_STRIDE:].reshape(1, CHUNK_WIDTH)))


def _cmp_to_sel_map(n_cmp, n_rows, n_sel):
    n_cols = -(-n_sel // LANES) * LANES
    c0 = np.arange(n_rows) * CMP_STRIDE
    s0 = np.arange(n_cols) * SEL_LEN
    m = (c0[:, None] < s0[None, :] + SEL_LEN) & (s0[None, :] < c0[:, None] + CMP_LEN)
    m &= (np.arange(n_rows) < n_cmp)[:, None] & (np.arange(n_cols) < n_sel)[None, :]
    return jnp.asarray(m.astype(np.float32))


def _masked_softmax_rows(s, mask):
    s = jnp.where(mask, s, MASK_NEG)
    m = jnp.max(s, axis=-1, keepdims=True)
    e = jnp.where(mask, jnp.exp(s - m), 0.0)
    return e * (1.0 / jnp.maximum(jnp.sum(e, axis=-1, keepdims=True), 1e-30))


def _biased_softmax_rows(s):
    m = jnp.maximum(jnp.max(s, axis=-1, keepdims=True), M_INIT)
    e = jnp.exp(s - m)
    return e * (1.0 / jnp.maximum(jnp.sum(e, axis=-1, keepdims=True), 1e-30))


def _flash_update(slot, rows, parts, m_ref, l_ref, acc_ref):
    m_prev = m_ref[slot, rows, :]
    s_max = functools.reduce(jnp.maximum, [s for s, _ in parts])
    m_new = jnp.maximum(m_prev, jnp.max(s_max, axis=-1, keepdims=True))
    alpha = jnp.exp(m_prev - m_new)
    acc_new = alpha * acc_ref[slot, rows, :]
    p_sum = None
    for s, v in parts:
        p = jnp.exp(s - jnp.concatenate([m_new] * (s.shape[1] // LANES), axis=1))
        p_sum = p if p_sum is None else p_sum + p
        acc_new = acc_new + _dot(p, v)
    l_ref[slot, rows, :] = alpha * l_ref[slot, rows, :] + jnp.sum(p_sum, axis=-1, keepdims=True)
    acc_ref[slot, rows, :] = acc_new
    m_ref[slot, rows, :] = m_new


SEL, WIN = 0, 1


def _block_bias_t(score_ref, rank_ref, n_sel, last_valid):
    rows, tq = score_ref.shape
    n_groups = rows // 8
    rank_ref[...] = jnp.zeros((rows, tq), F32)
    sub = lax.broadcasted_iota(jnp.int32, (8, tq), 0)
    for c in range(-(-n_sel // 8)):
        @pl.when(c * 8 <= last_valid)
        def _():
            groups = [score_ref[v * 8:(v + 1) * 8, :] for v in range(n_groups)]
            rank = [rank_ref[v * 8:(v + 1) * 8, :] for v in range(n_groups)]
            for j in range(c * 8, min(c * 8 + 8, n_sel)):
                r = jnp.broadcast_to(score_ref[j:j + 1, :], (8, tq))
                for v, s in enumerate(groups):
                    if v * 8 > j:
                        ahead = r >= s
                    elif v * 8 + 7 <= j:
                        ahead = r > s
                    else:
                        ahead = (r > s) | ((r == s) & (sub + v * 8 > j))
                    rank[v] = rank[v] + jnp.where(ahead, 1.0, 0.0)
            for v in range(n_groups):
                rank_ref[v * 8:(v + 1) * 8, :] = rank[v]
    return jnp.where(rank_ref[...] < float(min(SEL_TOP, n_sel)), 0.0, MASK_NEG)


def _nsa_prompt_kernel(q_ref, g_ref, kc_ref, vc_ref, selk_ref, selv_ref, wink_ref, winv_ref, map_ref, o_ref,
                       sc_ref, rank_ref, qa_ref, oc_ref, m_ref, l_ref, acc_ref, *, tq, nc, n_sel):
    qi = pl.program_id(1)
    tk = tq
    nw = WINDOW // tk
    q0 = qi * tq
    qpos = q0 + lax.broadcasted_iota(jnp.int32, (tq, 1), 0)
    ncp = vc_ref.shape[1]
    n_idx = lax.broadcasted_iota(jnp.int32, (1, ncp), 1)
    cbias = jnp.where((CMP_STRIDE * n_idx + (CMP_LEN - 1) <= qpos) & (n_idx < nc), 0.0, MASK_NEG)
    last_valid = lax.div(q0 + tq - 1, SEL_LEN)
    vc = vc_ref[0]
    nsr = sc_ref.shape[0]
    blk_t = lax.broadcasted_iota(jnp.int32, (nsr, tq), 0)
    cur_t = lax.div(q0 + lax.broadcasted_iota(jnp.int32, (nsr, tq), 1), SEL_LEN)
    valid_t = blk_t <= cur_t
    forced_t = valid_t & ((blk_t == 0) | (blk_t >= cur_t - 1))
    gates = g_ref[0]
    lane_half = lax.div(lax.broadcasted_iota(jnp.int32, (tq, LANES), 1), NSA_HEAD_DIM)
    r_idx = lax.broadcasted_iota(jnp.int32, (tq, tk), 0)
    c_idx = lax.broadcasted_iota(jnp.int32, (tq, tk), 1)
    causal_bias = jnp.where(c_idx <= r_idx, 0.0, MASK_NEG)
    oldest_bias = jnp.where(c_idx > r_idx, 0.0, MASK_NEG)

    qslice = lambda h: q_ref[0, :, h * LANES:(h + 1) * LANES]
    group_rows = lambda gq: slice(gq * tq, (gq + 1) * tq)
    slot = lambda k, branch: k * 2 + branch

    for k in range(NSA_KV_HEADS):
        kc = kc_ref[0, k]
        psum = jnp.zeros((tq, ncp), F32)
        for gq in range(NSA_GQ):
            p = _biased_softmax_rows(_dot_nt(qslice(k * NSA_GQ + gq), kc) + cbias)
            psum = psum + p
            oc_ref[k * NSA_GQ + gq] = _dot(p, vc)
        psum_hi = psum.astype(MXU_DTYPE)
        psum_lo = psum - psum_hi.astype(F32)
        imp = _dot(psum_hi, map_ref[...]) + _dot(psum_lo, map_ref[...])
        sc_ref[...] = jnp.where(forced_t, FORCE_SCORE, jnp.where(valid_t, imp.T[0:nsr, :], -jnp.inf))

        bias_t = jnp.concatenate([jnp.zeros((NSA_HEAD_DIM, tq), F32), _block_bias_t(sc_ref, rank_ref, n_sel, last_valid)]
                                 + ([jnp.zeros((NSA_HEAD_DIM - nsr, tq), F32)] if nsr < NSA_HEAD_DIM else []), axis=0)
        bias = bias_t.T.astype(MXU_DTYPE)
        for gq in range(NSA_GQ):
            qa_ref[slot(k, SEL), group_rows(gq), :] = qslice(k * NSA_GQ + gq) + bias
            qa_ref[slot(k, WIN), group_rows(gq), :] = qslice(k * NSA_GQ + gq)
        for branch in (SEL, WIN):
            m_ref[slot(k, branch)] = jnp.full((NSA_GQ * tq, LANES), M_INIT, F32)
            l_ref[slot(k, branch)] = jnp.zeros((NSA_GQ * tq, LANES), F32)
            acc_ref[slot(k, branch)] = jnp.zeros((NSA_GQ * tq, LANES), F32)

    def attend(tiles, k_ref, v_ref, branch):
        for r0 in range(0, NSA_GQ * tq, 2 * tq):
            rows = slice(r0, r0 + 2 * tq)
            for k in range(NSA_KV_HEADS):
                q = qa_ref[slot(k, branch), rows, :]
                parts = []
                for j, bias_tile in tiles:
                    k0 = pl.multiple_of(j * tk, tk)
                    s = _dot_nt(q, k_ref[0, k, pl.ds(k0, tk), :])
                    if bias_tile is not None:
                        s = s + jnp.concatenate([bias_tile] * 2, axis=0)
                    parts.append((s, v_ref[0, pl.ds(k0, tk), :]))
                _flash_update(slot(k, branch), rows, parts, m_ref, l_ref, acc_ref)

    lax.fori_loop(0, lax.div(qi, 2),
                  lambda jj, c: (attend([(2 * jj, None), (2 * jj + 1, None)], selk_ref, selv_ref, SEL), c)[1], 0)

    @pl.when(lax.rem(qi, 2) == 1)
    def _():
        attend([(qi - 1, None), (qi, causal_bias)], selk_ref, selv_ref, SEL)

    @pl.when(lax.rem(qi, 2) == 0)
    def _():
        attend([(qi, causal_bias)], selk_ref, selv_ref, SEL)

    win_tiles = []
    for i in range(nw + 1):
        j = qi - nw + i
        edge = oldest_bias if i == 0 else causal_bias if i == nw else jnp.zeros((tq, tk), F32)
        win_tiles.append((jnp.maximum(j, 0), edge if i == nw else jnp.where(j >= 0, edge, MASK_NEG)))
    attend(win_tiles, wink_ref, winv_ref, WIN)

    for h in range(NSA_HEADS):
        k, gq = divmod(h, NSA_GQ)
        rows = group_rows(gq)
        o_s = acc_ref[slot(k, SEL), rows, :] / jnp.maximum(l_ref[slot(k, SEL), rows, :], 1e-30)
        o_w = acc_ref[slot(k, WIN), rows, :] / jnp.maximum(l_ref[slot(k, WIN), rows, :], 1e-30)
        c = 3 * h
        out = gates[:, c:c + 1] * oc_ref[h] + gates[:, c + 1:c + 2] * o_s + gates[:, c + 2:c + 3] * o_w
        o_ref[0, :, h * LANES:(h + 1) * LANES] = jnp.where(lane_half == k, out, 0.0).astype(o_ref.dtype)


def nsa_prompt(qp, gates, kc, vc, selk, selv, wink, winv):
    b, t, _ = qp.shape
    tq = min(t, 256)
    assert WINDOW % tq == 0 and t % tq == 0
    ncp = vc.shape[1]
    nc = ncp - 1
    n_sel = -(-t // SEL_LEN)
    nsr = -(-n_sel // 8) * 8
    assert nsr <= NSA_HEAD_DIM
    cmap = _cmp_to_sel_map(nc, ncp, n_sel).astype(MXU_DTYPE)
    per_b = lambda a: pl.BlockSpec((1,) + a.shape[1:], lambda i, j: (i,) + (0,) * (a.ndim - 1))
    return pl.pallas_call(
        functools.partial(_nsa_prompt_kernel, tq=tq, nc=nc, n_sel=n_sel),
        grid=(b, t // tq),
        in_specs=[pl.BlockSpec((1, tq, QP_WIDTH), lambda i, j: (i, j, 0)),
                  pl.BlockSpec((1, tq, LANES), lambda i, j: (i, j, 0)),
                  per_b(kc), per_b(vc), per_b(selk), per_b(selv), per_b(wink), per_b(winv),
                  pl.BlockSpec(cmap.shape, lambda i, j: (0, 0))],
        out_specs=pl.BlockSpec((1, tq, QP_WIDTH), lambda i, j: (i, j, 0)),
        out_shape=jax.ShapeDtypeStruct((b, t, QP_WIDTH), MXU_DTYPE),
        scratch_shapes=[pltpu.VMEM((nsr, tq), F32),
                        pltpu.VMEM((nsr, tq), F32),
                        pltpu.VMEM((2 * NSA_KV_HEADS, NSA_GQ * tq, LANES), MXU_DTYPE),
                        pltpu.VMEM((NSA_HEADS, tq, LANES), F32),
                        pltpu.VMEM((2 * NSA_KV_HEADS, NSA_GQ * tq, LANES), F32),
                        pltpu.VMEM((2 * NSA_KV_HEADS, NSA_GQ * tq, LANES), F32),
                        pltpu.VMEM((2 * NSA_KV_HEADS, NSA_GQ * tq, LANES), F32)],
        compiler_params=_params(("parallel", "parallel"), big=True),
        name="nsa_prompt",
    )(qp, gates, kc, vc, selk, selv, wink, winv, cmap)


PAGE_CHUNKS = PAGE_SIZE // CMP_STRIDE
PAGE_BLOCKS = PAGE_SIZE // SEL_LEN
PAGE_GROUP = 16
CHUNK_PITCH = 24


def _pages_view(pool):
    depth, n_pool, page = pool.shape[:3]
    return pool.transpose(0, 1, 3, 4, 5, 2).reshape(depth, n_pool, 2, KV_WIDTH, page)


def _paged_hidden_kernel(pt_ref, pool_ref, wk_ref, wv_ref, ha_ref, hb_ref, buf_ref, rows_ref, sem,
                         *, layer, pg, n_steps):
    step = pl.program_id(0) * pl.num_programs(1) + pl.program_id(1)
    slot = lax.rem(step, 2)

    def page_copy(s, sl, r):
        return pltpu.make_async_copy(pool_ref.at[layer, pt_ref[s * pg + r]], buf_ref.at[sl, r], sem.at[sl])

    def start_all(s, sl):
        lax.fori_loop(0, pg, lambda r, c: (page_copy(s, sl, r).start(), c)[1], 0, unroll=8)

    @pl.when(step == 0)
    def _():
        start_all(step, slot)

    @pl.when(step + 1 < n_steps)
    def _():
        start_all(step + 1, 1 - slot)

    lax.fori_loop(0, pg, lambda r, c: (page_copy(step, slot, r).wait(), c)[1], 0, unroll=8)

    half = NSA_KV_HEADS * CMP_HIDDEN
    for g0 in range(0, pg, PAGE_GROUP):
        for r in range(g0, min(g0 + PAGE_GROUP, pg)):
            for kv in range(2):
                page_rows = buf_ref[slot, r, kv].T
                for c in range(PAGE_CHUNKS):
                    dst = (r * PAGE_CHUNKS + c) * CHUNK_PITCH
                    rows_ref[kv, dst:dst + CMP_STRIDE, :] = page_rows[c * CMP_STRIDE:(c + 1) * CMP_STRIDE, :]
        n = min(PAGE_GROUP, pg - g0) * PAGE_CHUNKS
        c0 = g0 * PAGE_CHUNKS
        for kv, w_ref in enumerate((wk_ref, wv_ref)):
            acc = jnp.zeros((n, 2 * half), F32)
            for q in range(CMP_STRIDE // 2):
                x = jnp.concatenate([rows_ref[kv, pl.ds(c0 * CHUNK_PITCH + 2 * q + i, n, stride=CHUNK_PITCH), :]
                                     for i in range(2)], axis=1)
                acc = acc + _dot(x, w_ref[q])
            ha_ref[0, c0:c0 + n, kv * half:(kv + 1) * half] = acc[:, 0:half]
            hb_ref[0, c0:c0 + n, kv * half:(kv + 1) * half] = acc[:, half:2 * half]


def paged_hidden(pool, page_table, cw, *, layer):
    b, n_pages = page_table.shape
    pg = math.gcd(n_pages, 64)
    pages = _pages_view(pool)
    full = lambda a: pl.BlockSpec(a.shape, lambda i, j, pt: (0,) * a.ndim)
    out = jax.ShapeDtypeStruct((b, n_pages * PAGE_CHUNKS, HID_WIDTH), F32)
    out_spec = pl.BlockSpec((1, pg * PAGE_CHUNKS, HID_WIDTH), lambda i, j, pt: (i, j, 0))
    ws = (cw['wk_pair'], cw['wv_pair'])
    return pl.pallas_call(
        functools.partial(_paged_hidden_kernel, layer=layer, pg=pg, n_steps=b * (n_pages // pg)),
        grid_spec=pltpu.PrefetchScalarGridSpec(
            num_scalar_prefetch=1,
            grid=(b, n_pages // pg),
            in_specs=[pl.BlockSpec(memory_space=pl.ANY)] + [full(a) for a in ws],
            out_specs=(out_spec, out_spec),
            scratch_shapes=[pltpu.VMEM((2, pg, 2, KV_WIDTH, PAGE_SIZE), F32),
                            pltpu.VMEM((2, pg * PAGE_CHUNKS * CHUNK_PITCH, KV_WIDTH), F32),
                            pltpu.SemaphoreType.DMA((2,))]),
        out_shape=(out, out),
        compiler_params=_params(("arbitrary", "arbitrary"), big=True),
        name="paged_hidden",
    )(page_table.reshape(-1), pages, *ws)


def _own_half_queries(q8):
    q = q8.astype(F32)
    own_first = lax.div(lax.broadcasted_iota(jnp.int32, q.shape, 0), NSA_GQ) == 0
    return jnp.where(own_first, q, pltpu.roll(q, NSA_HEAD_DIM, 1))


SEL_KEYS = SEL_TOP * PAGE_SIZE
NO_KEY = 2 ** 30


def _decode_cmp_kernel(ha_ref, hb_ref, hpe_ref, new_ref, q_ref, w1b0_ref, w2_ref, cm_ref, sa_ref, sb_ref, map_ref,
                       oc_ref, idx_ref, kpos_ref, hbs_ref, *, q_pos, n_sel):
    for g in range(ha_ref.shape[0]):
        _decode_cmp_one(g, ha_ref, hb_ref, hpe_ref, new_ref, q_ref, w1b0_ref, w2_ref, cm_ref, sa_ref, sb_ref, map_ref,
                        oc_ref, idx_ref, kpos_ref, hbs_ref, q_pos=q_pos, n_sel=n_sel)


def _decode_cmp_one(g, ha_ref, hb_ref, hpe_ref, new_ref, q_ref, w1b0_ref, w2_ref, cm_ref, sa_ref, sb_ref, map_ref,
                    oc_ref, idx_ref, kpos_ref, hbs_ref, *, q_pos, n_sel):
    n = ha_ref.shape[1]
    hbs_ref[g, 0:n, :] = hb_ref[g]
    new8 = jnp.concatenate([new_ref[g], jnp.zeros((7, ROW_WIDTH), F32)], axis=0)
    hbs_ref[g, n:n + 8, :] = _dot(new8, w1b0_ref[...])
    kc, vc = _compress_tail(ha_ref[g], hbs_ref[g, 1:n + 1, :], hpe_ref[0:1, :], w2_ref,
                            cm_ref[...], sa_ref[...], sb_ref[...])
    q8 = _own_half_queries(q_ref[g])
    n_idx = lax.broadcasted_iota(jnp.int32, (1, n), 1)
    p = _masked_softmax_rows(_dot_nt(q8, kc), CMP_STRIDE * n_idx + (CMP_LEN - 1) <= q_pos)
    oc_ref[g] = _dot(p, vc)

    row = lax.broadcasted_iota(jnp.int32, p.shape, 0)
    psum = jnp.zeros(p.shape, F32)
    for k in range(NSA_KV_HEADS):
        grp = jnp.sum(jnp.where(lax.div(row, NSA_GQ) == k, p, 0.0), axis=0, keepdims=True)
        psum = jnp.where(row == k, grp, psum)
    psum_hi = psum.astype(MXU_DTYPE)
    imp = _dot(psum_hi, map_ref[...]) + _dot(psum - psum_hi.astype(F32), map_ref[...])
    nsp = imp.shape[1]
    blk = lax.broadcasted_iota(jnp.int32, (NSA_HEADS, nsp), 1)
    cur = q_pos // SEL_LEN
    forced = (blk == 0) | (blk >= cur - 1)
    score = jnp.where(blk < n_sel, jnp.where(forced, FORCE_SCORE, imp), -jnp.inf)
    blk_f = blk.astype(F32)
    lane = lax.broadcasted_iota(jnp.int32, (NSA_HEADS, LANES), 1)
    key = lax.broadcasted_iota(jnp.int32, (NSA_HEADS, SEL_KEYS), 1)
    key_slot = lax.div(key, PAGE_SIZE)
    key_off = key - key_slot * PAGE_SIZE
    idx = jnp.zeros((NSA_HEADS, LANES), jnp.int32)
    kpos = jnp.zeros((NSA_HEADS, SEL_KEYS), jnp.int32)
    for i in range(SEL_TOP):
        top = jnp.max(score, axis=1, keepdims=True)
        pick_f = jnp.min(jnp.where(score == top, blk_f, float(nsp)), axis=1, keepdims=True)
        pick = pick_f.astype(jnp.int32)
        idx = jnp.where(lane == i, pick, idx)
        pos = lax.div(pick, PAGE_BLOCKS) * PAGE_SIZE + key_off
        kpos = jnp.where(key_slot == i, jnp.where(lax.div(pos, SEL_LEN) == pick, pos, NO_KEY), kpos)
        score = jnp.where(blk == pick, -jnp.inf, score)
    idx_ref[g] = idx
    kpos_ref[g] = kpos


def decode_cmp(ha, hb, hpe, cmp_new, q8, cw, tables, cmap, *, q_pos, n_sel):
    b, n, _ = ha.shape
    cm, sa, sb = tables
    w1b0 = cw['w1b'][0:ROW_WIDTH]
    g = math.gcd(b, 2)
    per_b = lambda a: pl.BlockSpec((g,) + a.shape[1:], lambda i: (i, 0, 0))
    full = lambda a: pl.BlockSpec(a.shape, lambda i: (0,) * a.ndim)
    out8 = lambda w_, dt: (jax.ShapeDtypeStruct((b, NSA_HEADS, w_), dt), pl.BlockSpec((g, NSA_HEADS, w_), lambda i: (i, 0, 0)))
    outs = (out8(LANES, F32), out8(LANES, jnp.int32), out8(SEL_KEYS, jnp.int32))
    cmap = cmap.astype(MXU_DTYPE)
    return pl.pallas_call(
        functools.partial(_decode_cmp_kernel, q_pos=q_pos, n_sel=n_sel),
        grid=(b // g,),
        in_specs=[per_b(ha), per_b(hb), full(hpe), per_b(cmp_new), per_b(q8), full(w1b0), full(cw['w2']),
                  full(cm), full(sa), full(sb), full(cmap)],
        out_specs=tuple(o[1] for o in outs),
        out_shape=tuple(o[0] for o in outs),
        scratch_shapes=[pltpu.VMEM((g, n + 8, HID_WIDTH), F32)],
        compiler_params=_params(("parallel",), big=True),
        name="decode_cmp",
    )(ha, hb, hpe, cmp_new, q8, w1b0, cw['w2'], cm, sa, sb, cmap)


def _decode_attn_kernel(idx_ref, pt_ref, pool_ref, win_ref, seln_ref, winn_ref, q_ref, g_ref, oc_ref, kpos_ref,
                        o_ref, wst_ref, k_ref, v_ref, sem, *, layer, n_pages, q_pos):
    b = pl.program_id(0)
    n_past = n_pages * PAGE_BLOCKS
    keep = win_ref.shape[4]
    hd = NSA_HEAD_DIM

    def slot(k, n_):
        blk = idx_ref[(b * NSA_KV_HEADS + k) * SEL_TOP + n_]
        page = pt_ref[b * n_pages + jnp.minimum(lax.div(blk, PAGE_BLOCKS), n_pages - 1)]
        lanes = slice(n_ * PAGE_SIZE, (n_ + 1) * PAGE_SIZE)
        dims = slice(k * hd, (k + 1) * hd)
        return blk, lanes, (pltpu.make_async_copy(pool_ref.at[layer, page, 0, dims, :], k_ref.at[k, :, lanes], sem),
                            pltpu.make_async_copy(pool_ref.at[layer, page, 1, dims, :], v_ref.at[k, :, lanes], sem))

    for k in range(NSA_KV_HEADS):
        for n_ in range(SEL_TOP):
            blk, lanes, copies = slot(k, n_)

            @pl.when(blk < n_past)
            def _():
                for cp in copies:
                    cp.start()

            @pl.when(blk >= n_past)
            def _():
                k_ref[k, :, lanes] = jnp.zeros((hd, PAGE_SIZE), F32)
                v_ref[k, :, lanes] = jnp.zeros((hd, PAGE_SIZE), F32)

    row0 = lax.broadcasted_iota(jnp.int32, (LANES, ROW_WIDTH), 0) == 0
    win_new_t = jnp.where(row0, jnp.broadcast_to(winn_ref[0], (LANES, ROW_WIDTH)), 0.0).T
    last_lane = lax.broadcasted_iota(jnp.int32, (KV_WIDTH, keep), 1) == keep - 1
    state = []
    for kv in range(2):
        shifted = pltpu.roll(win_ref[0, 0, kv], keep - 1, 1)
        state.append(jnp.where(last_lane, win_new_t[kv * KV_WIDTH:(kv + 1) * KV_WIDTH, 0:1], shifted))
        wst_ref[0, kv] = state[kv]

    q8 = q_ref[0]
    qk = q8[:, 0:hd]
    row = lax.broadcasted_iota(jnp.int32, (NSA_HEADS, LANES), 0)
    lane_half = lax.div(lax.broadcasted_iota(jnp.int32, (NSA_HEADS, LANES), 1), hd)
    own_half = lane_half == lax.div(row, NSA_GQ)

    def to_half(o, k):
        z = jnp.zeros_like(o)
        return jnp.concatenate([o, z] if k == 0 else [z, o], axis=1)

    o_w = jnp.zeros((NSA_HEADS, LANES), F32)
    for k in range(NSA_KV_HEADS):
        s = _dot(qk, state[0][k * hd:(k + 1) * hd, :])
        e = jnp.exp(s - jnp.max(s, axis=1, keepdims=True))
        p = e / jnp.maximum(jnp.sum(e, axis=1, keepdims=True), 1e-30)
        o_w = jnp.where(lax.div(row, NSA_GQ) == k, to_half(_dot_nt(p, state[1][k * hd:(k + 1) * hd, :]), k), o_w)

    for k in range(NSA_KV_HEADS):
        for n_ in range(SEL_TOP):
            blk, lanes, copies = slot(k, n_)

            @pl.when(blk < n_past)
            def _():
                for cp in copies:
                    cp.wait()

    new_k = seln_ref[0][:, 0:KV_WIDTH]
    new_v = seln_ref[0][:, KV_WIDTH:ROW_WIDTH]
    s_new = jnp.sum(_own_half_queries(q8) * new_k, axis=1, keepdims=True)
    grp = lax.div(lax.broadcasted_iota(jnp.int32, (NSA_HEADS, 1), 0), NSA_GQ)
    o_s = jnp.zeros((NSA_HEADS, LANES), F32)
    e_new = jnp.zeros((NSA_HEADS, 1), F32)
    denom = jnp.ones((NSA_HEADS, 1), F32)
    for k in range(NSA_KV_HEADS):
        mask = kpos_ref[0, k:k + 1, :] < q_pos
        s = jnp.where(mask, _dot(qk, k_ref[k]), MASK_NEG)
        m = jnp.maximum(jnp.max(s, axis=1, keepdims=True), s_new)
        e = jnp.where(mask, jnp.exp(s - m), 0.0)
        en = jnp.exp(s_new - m)
        o_s = jnp.where(lax.div(row, NSA_GQ) == k, to_half(_dot_nt(e, v_ref[k]), k), o_s)
        e_new = jnp.where(grp == k, en, e_new)
        denom = jnp.where(grp == k, jnp.sum(e, axis=1, keepdims=True) + en, denom)
    o_s = (o_s + e_new * jnp.where(own_half, jnp.broadcast_to(new_v, (NSA_HEADS, LANES)), 0.0)) / denom

    g = g_ref[0]
    out = g[:, 0:1] * oc_ref[0] + g[:, 1:2] * o_s + g[:, 2:3] * o_w
    o_ref[0] = jnp.where(own_half, out, 0.0).astype(o_ref.dtype)


def decode_attn(idx, page_table, pool, win_cache, sel_new, win_new, q8, gates3, oc, kpos, *, layer, q_pos):
    depth = pool.shape[0]
    b, n_pages = page_table.shape
    keep = win_cache.shape[2]
    assert keep == WINDOW
    win = win_cache.transpose(0, 1, 3, 4, 5, 2).reshape(depth, b, 2, KV_WIDTH, keep)
    per_b = lambda a: pl.BlockSpec((1,) + a.shape[1:], lambda i, ix, pt: (i, 0, 0))
    idx_flat = idx[:, 0:NSA_KV_HEADS, 0:SEL_TOP].reshape(-1)
    return pl.pallas_call(
        functools.partial(_decode_attn_kernel, layer=layer, n_pages=n_pages, q_pos=q_pos),
        grid_spec=pltpu.PrefetchScalarGridSpec(
            num_scalar_prefetch=2,
            grid=(b,),
            in_specs=[pl.BlockSpec(memory_space=pl.ANY),
                      pl.BlockSpec((1, 1, 2, KV_WIDTH, keep), lambda i, ix, pt: (layer, i, 0, 0, 0)),
                      per_b(sel_new), per_b(win_new), per_b(q8), per_b(gates3), per_b(oc), per_b(kpos)],
            out_specs=(pl.BlockSpec((1, NSA_HEADS, LANES), lambda i, ix, pt: (i, 0, 0)),
                       pl.BlockSpec((1, 2, KV_WIDTH, keep), lambda i, ix, pt: (i, 0, 0, 0))),
            scratch_shapes=[pltpu.VMEM((NSA_KV_HEADS, NSA_HEAD_DIM, SEL_KEYS), F32),
                            pltpu.VMEM((NSA_KV_HEADS, NSA_HEAD_DIM, SEL_KEYS), F32),
                            pltpu.SemaphoreType.DMA(())]),
        out_shape=(jax.ShapeDtypeStruct((b, NSA_HEADS, LANES), MXU_DTYPE),
                   jax.ShapeDtypeStruct((b, 2, KV_WIDTH, keep), F32)),
        compiler_params=_params(("arbitrary",), big=True),
        name="decode_attn",
    )(idx_flat, page_table.reshape(-1), _pages_view(pool), win, sel_new, win_new, q8, gates3, oc, kpos)


def _merge_ln_kernel(x_ref, ya_ref, yb_ref, yc_ref, wgm_ref, wa_ref, wb_ref, wc_ref, wo_ref, lng_ref, lnb_ref, o_ref):
    x = x_ref[0]
    xb = x.astype(MXU_DTYPE)
    gate = lambda j: _sigmoid(jnp.dot(xb, wgm_ref[:, j * D_MODEL:(j + 1) * D_MODEL], preferred_element_type=F32))
    merged = (gate(0) * _dot(ya_ref[...], wa_ref[...]) + gate(1) * _dot(yb_ref[0], wb_ref[...])
              + gate(2) * _dot(yc_ref[0], wc_ref[...]))
    o_ref[0] = _layer_norm(DN_ALPHA * x + _dot(merged, wo_ref[...]), lng_ref[...], lnb_ref[...])


def merge_ln(x, ya, yb, yc, mw, lng, lnb):
    nb, t, d = x.shape
    tt = min(t, 256)
    row3 = lambda w_: pl.BlockSpec((1, tt, w_), lambda b, i: (b, i, 0))
    full = lambda a: pl.BlockSpec(a.shape, lambda b, i: (0,) * a.ndim)
    ws = (mw['w_gm'], mw['w_a'], mw['w_b'], mw['w_c'], mw['w_o'], lng, lnb)
    return pl.pallas_call(
        _merge_ln_kernel,
        grid=(nb, t // tt),
        in_specs=[row3(d), pl.BlockSpec((tt, S5_WIDTH), lambda b, i: (i, b)), row3(QP_WIDTH), row3(CONV_CH)]
                 + [full(a) for a in ws],
        out_specs=row3(d),
        out_shape=jax.ShapeDtypeStruct((nb, t, d), F32),
        compiler_params=_params(("parallel", "parallel"), big=True),
        name="merge_ln",
    )(x, ya, yb, yc, *ws)


def _layer_weights(w, l):
    bf = lambda a: a.astype(MXU_DTYPE)
    row = lambda a: a.reshape(1, -1).astype(F32)
    w_in = w['w_in'][l]
    w_u, w_q, w_kv, w_gn, w_conv, w_gm = jnp.split(w_in, IN_OFFSETS, axis=-1)
    head_half = (jnp.arange(NSA_HEADS) // NSA_GQ)[None, :, None]

    def pad_heads(a3, own_half):
        z = jnp.zeros_like(a3)
        lo = jnp.concatenate([a3, z], axis=-1)
        hi = jnp.concatenate([z, a3], axis=-1)
        return (jnp.where(head_half == 0, lo, hi) if own_half else lo).reshape(a3.shape[0], QP_WIDTH)

    w_qp = pad_heads(w_q.reshape(D_MODEL, NSA_HEADS, NSA_HEAD_DIM), False)
    w_gpad = jnp.concatenate([w_gn, jnp.zeros((D_MODEL, LANES - w_gn.shape[1]), F32)], axis=1)
    w_proj = bf(jnp.concatenate([w_u, w_qp, w_kv, w_gpad, w_conv], axis=1))
    assert w_proj.shape[1] == C_END
    w_b = pad_heads(w['w_proj_b'][l].T.reshape(D_MODEL, NSA_HEADS, NSA_HEAD_DIM), True).T
    conv_w = jnp.concatenate([w['conv_w'][l], jnp.zeros((8 - CONV_K, CONV_CH), F32)], axis=0)
    gu = lambda name: jnp.split(w[name][l], 2, axis=-1)
    return dict(
        ffn1=(bf(gu('ffn1_w_gu')[0]), bf(gu('ffn1_w_gu')[1]), bf(w['ffn1_w_down'][l])),
        ffn2=(bf(gu('ffn2_w_gu')[0]), bf(gu('ffn2_w_gu')[1]), bf(w['ffn2_w_down'][l])),
        ln=[(row(w['ln_g'][l, j]), row(w['ln_b'][l, j])) for j in range(3)],
        w_proj=w_proj, conv_w=conv_w,
        s5=_s5_params(w['s5_lambda_re'][l], w['s5_lambda_im'][l], w['s5_log_dt'][l], w['s5_b_re'][l], w['s5_b_im'][l],
                      w['s5_c_re'][l], w['s5_c_im'][l], w['s5_d'][l], w['s5_w_glu'][l], w['s5_b_glu'][l]),
        cmp=_compress_params(w['nsa_phi_k1'][l], w['nsa_phi_k2'][l], w['nsa_phi_v1'][l], w['nsa_phi_v2'][l],
                             w['nsa_pe_k'][l], w['nsa_pe_v'][l]),
        merge=dict(w_gm=bf(w_gm), w_a=bf(w['w_proj_a'][l]), w_b=bf(w_b), w_c=bf(w['w_proj_c'][l]), w_o=bf(w['w_o'][l])),
    )


def _prompt_trunk(x, layers):
    b, t, d = x.shape
    n_chunks = t // CMP_STRIDE
    tok_tables = _rope_tables(jnp.arange(t))
    cmp_tables = _rope_tables(jnp.arange(n_chunks) * CMP_STRIDE + (CMP_LEN - 1))
    zero_state = jnp.zeros((b, S5_FLAT), F32)
    states, hpes = [], []
    x2 = x.reshape(b * t, d)
    for lw in layers:
        x2 = ffn_ln(x2, *lw['ffn1'], *lw['ln'][0])
        u, qp, cmp_rows, sel_t, win_t, gates, yc, conv_state, selk, selv, wink, winv, cmp_t = in_proj(
            x2.reshape(b, t, d), lw['w_proj'], tok_tables, lw['conv_w'], jnp.zeros((8, LANES), F32), decode=False)
        ya, s_re, s_im = s5_branch(u, zero_state, zero_state, lw['s5'], nb=b, wide=True)
        kc, vc, hpe = compress(cmp_rows.reshape(b, n_chunks, CHUNK_WIDTH), lw['cmp'], cmp_tables)
        hpes.append(hpe)
        yb = nsa_prompt(qp, gates, kc, vc, selk, selv, wink, winv)
        x3 = merge_ln(x2.reshape(b, t, d), ya, yb, yc, lw['merge'], *lw['ln'][1])
        x2 = ffn_ln(x3.reshape(b * t, d), *lw['ffn2'], *lw['ln'][2])
        rows = lambda a: a.reshape(b, 2, NSA_KV_HEADS, NSA_HEAD_DIM, -1).transpose(0, 4, 1, 2, 3)
        states.append((rows(cmp_t), rows(sel_t), rows(win_t), conv_state,
                       s_re.reshape(b, S5_GROUPS, S5_STATE), s_im.reshape(b, S5_GROUPS, S5_STATE)))
    return x2.reshape(b, t, d), [jnp.stack(s, axis=0) for s in zip(*states)], hpes


def _sample_trunk(x, layers, hpes, cache_cmp, cache_sel, cache_win, cache_conv, s5_re, s5_im, page_table):
    bs, dec_seq, d = x.shape
    assert dec_seq == 1
    n_pages = page_table.shape[1]
    q_pos = n_pages * PAGE_SIZE
    n_blocks = n_pages * PAGE_CHUNKS
    n_sel = q_pos // SEL_LEN + 1
    assert cache_win.shape[2] == min(WINDOW, q_pos) and n_sel >= SEL_TOP
    tok_tables = _rope_tables(jnp.full((bs,), q_pos))
    cmp_tables = _rope_tables(jnp.arange(n_blocks) * CMP_STRIDE + (CMP_LEN - 1))
    cmap = _cmp_to_sel_map(n_blocks, n_blocks, n_sel)
    kv_shape = (2, NSA_KV_HEADS, NSA_HEAD_DIM)
    states = []
    x2 = x.reshape(bs, d)
    for l, lw in enumerate(layers):
        x2 = ffn_ln(x2, *lw['ffn1'], *lw['ln'][0])
        u, qp, cmp_new, sel_new, win_new, gates, yc, conv_state = in_proj(
            x2[None], lw['w_proj'], tok_tables, lw['conv_w'], cache_conv[l].reshape(bs, (CONV_K - 1) * CONV_CH),
            decode=True)
        ya, s_re, s_im = s5_branch(u, s5_re[l].reshape(bs, S5_FLAT), s5_im[l].reshape(bs, S5_FLAT), lw['s5'], nb=bs,
                                   wide=False)
        ha, hb = paged_hidden(cache_cmp, page_table, lw['cmp'], layer=l)
        hpe = hpes[l]
        q8 = qp.reshape(bs, NSA_HEADS, LANES)
        per_seq = lambda a: a.reshape(bs, 1, ROW_WIDTH)
        oc, idx, kpos = decode_cmp(ha, hb, hpe, per_seq(cmp_new), q8, lw['cmp'], cmp_tables, cmap,
                                   q_pos=q_pos, n_sel=n_sel)
        g3 = gates[0, :, 0:3 * NSA_HEADS].reshape(bs, NSA_HEADS, 3)
        g3 = jnp.concatenate([g3, jnp.zeros((bs, NSA_HEADS, LANES - 3), F32)], axis=-1)
        yb, win_state = decode_attn(idx, page_table, cache_sel, cache_win, per_seq(sel_new), per_seq(win_new), q8, g3,
                                    oc, kpos, layer=l, q_pos=q_pos)
        x3 = merge_ln(x2[None], ya, yb.reshape(1, bs, QP_WIDTH), yc, lw['merge'], *lw['ln'][1])
        x2 = ffn_ln(x3[0], *lw['ffn2'], *lw['ln'][2])
        states.append((cmp_new.reshape((bs, 1) + kv_shape), sel_new.reshape((bs, 1) + kv_shape),
                       win_state.reshape((bs,) + kv_shape + (-1,)).transpose(0, 4, 1, 2, 3),
                       conv_state.reshape(bs, CONV_K - 1, CONV_CH),
                       s_re.reshape(bs, S5_GROUPS, S5_STATE), s_im.reshape(bs, S5_GROUPS, S5_STATE)))
    return x2.reshape(bs, 1, d), [jnp.stack(s, axis=0) for s in zip(*states)]


def kernel(x_prompt, x_sample, cache_cmp_kv, cache_sel_kv, cache_win_kv, cache_conv, state_s5_re, state_s5_im, page_table, ln_g, ln_b, ffn1_w_gu, ffn1_w_down, ffn2_w_gu, ffn2_w_down, w_in, s5_lambda_re, s5_lambda_im, s5_log_dt, s5_b_re, s5_b_im, s5_c_re, s5_c_im, s5_d, s5_w_glu, s5_b_glu, nsa_pe_k, nsa_pe_v, nsa_phi_k1, nsa_phi_k2, nsa_phi_v1, nsa_phi_v2, conv_w, w_proj_a, w_proj_b, w_proj_c, w_o):
    w = dict(ln_g=ln_g, ln_b=ln_b, ffn1_w_gu=ffn1_w_gu, ffn1_w_down=ffn1_w_down, ffn2_w_gu=ffn2_w_gu,
             ffn2_w_down=ffn2_w_down, w_in=w_in, s5_lambda_re=s5_lambda_re, s5_lambda_im=s5_lambda_im,
             s5_log_dt=s5_log_dt, s5_b_re=s5_b_re, s5_b_im=s5_b_im, s5_c_re=s5_c_re, s5_c_im=s5_c_im, s5_d=s5_d,
             s5_w_glu=s5_w_glu, s5_b_glu=s5_b_glu, nsa_pe_k=nsa_pe_k, nsa_pe_v=nsa_pe_v, nsa_phi_k1=nsa_phi_k1,
             nsa_phi_k2=nsa_phi_k2, nsa_phi_v1=nsa_phi_v1, nsa_phi_v2=nsa_phi_v2, conv_w=conv_w,
             w_proj_a=w_proj_a, w_proj_b=w_proj_b, w_proj_c=w_proj_c, w_o=w_o)
    layers = [_layer_weights(w, l) for l in range(DEPTH)]
    y_prompt, (p_cmp, p_sel, p_win, p_conv, p_re, p_im), hpes = _prompt_trunk(x_prompt, layers)
    y_sample, (s_cmp, s_sel, s_win, s_conv, s_re, s_im) = _sample_trunk(
        x_sample, layers, hpes, cache_cmp_kv, cache_sel_kv, cache_win_kv, cache_conv, state_s5_re, state_s5_im, page_table)
    return (y_prompt, y_sample, p_cmp, s_cmp, p_sel, s_sel, p_win, s_win, p_conv, s_conv, p_re, s_re, p_im, s_im)
```

```python
import functools
import math

import numpy as np
import jax
import jax.numpy as jnp
from jax import lax
from jax.experimental import pallas as pl
from jax.experimental.pallas import tpu as pltpu

F32 = jnp.float32
MXU_DTYPE = jnp.bfloat16

D_MODEL = 1024
DEPTH = 2
PAGE_SIZE = 128
DN_ALPHA = (2.0 * DEPTH) ** 0.25
LN_EPS = 1e-5
D_FF = 2816
FFN_RES = 0.5
S5_WIDTH = 512
S5_GROUP = 16
S5_GROUPS = S5_WIDTH // S5_GROUP
S5_STATE = 64
S5_FLAT = S5_GROUPS * S5_STATE
NSA_HEADS = 8
NSA_KV_HEADS = 2
NSA_HEAD_DIM = 64
NSA_GQ = NSA_HEADS // NSA_KV_HEADS
CMP_LEN = 32
CMP_STRIDE = 16
CMP_HIDDEN = 128
SEL_LEN = 64
SEL_TOP = 16
WINDOW = 512
ROPE_THETA = 500000.0
ROT_DIM = NSA_HEAD_DIM // 4
ROT_HALF = ROT_DIM // 2
FORCE_SCORE = 1e9
CONV_CH = 512
CONV_K = 3
N_BRANCH = 3
KV_WIDTH = NSA_KV_HEADS * NSA_HEAD_DIM
ROW_WIDTH = 2 * KV_WIDTH
IN_WIDTHS = (S5_WIDTH, NSA_HEADS * NSA_HEAD_DIM, 6 * KV_WIDTH, 3 * NSA_HEADS, 3 * CONV_CH, N_BRANCH * D_MODEL)
IN_OFFSETS = tuple(int(o) for o in np.cumsum(IN_WIDTHS)[:-1])

LANES = 128
V7X_VMEM_BYTES = 64 * 1024 * 1024
VMEM_LIMIT = (V7X_VMEM_BYTES * 7) // 8

QP_WIDTH = NSA_HEADS * LANES
MASK_NEG = -1e30
M_INIT = -1e29

C_U = 0
C_Q = C_U + S5_WIDTH
C_KV = C_Q + QP_WIDTH
C_G = C_KV + 6 * KV_WIDTH
C_CONV = C_G + LANES
C_END = C_CONV + 3 * CONV_CH


def _sigmoid(x):
    return 1.0 / (1.0 + jnp.exp(-x))


def _gelu_tanh(x):
    return 0.5 * x * (1.0 + jnp.tanh(math.sqrt(2.0 / math.pi) * (x + 0.044715 * (x * x * x))))


def _layer_norm(x, g, b):
    mu = jnp.mean(x, axis=-1, keepdims=True)
    xc = x - mu
    var = jnp.mean(xc * xc, axis=-1, keepdims=True)
    return xc * lax.rsqrt(var + LN_EPS) * g + b


def _dot(a, b):
    return jnp.dot(a.astype(MXU_DTYPE), b.astype(MXU_DTYPE), preferred_element_type=F32)


def _dot_nt(a, b):
    return lax.dot_general(a.astype(MXU_DTYPE), b.astype(MXU_DTYPE), (((1,), (1,)), ((), ())),
                           preferred_element_type=F32)


def _rope_lanes(x, cm, sa, sb):
    return x * cm + pltpu.roll(x, LANES - ROT_HALF, 1) * sa + pltpu.roll(x, ROT_HALF, 1) * sb


def _rope_tables(pos):
    inv_freq = ROPE_THETA ** (-jnp.arange(ROT_HALF, dtype=F32) / ROT_HALF)
    ang = pos.astype(F32)[:, None] * inv_freq
    cos, sin = jnp.cos(ang), jnp.sin(ang)
    n = pos.shape[0]
    rest = NSA_HEAD_DIM - ROT_DIM
    cm = jnp.concatenate([cos, cos, jnp.ones((n, rest), F32)], axis=1)
    sa = jnp.concatenate([-sin, jnp.zeros((n, ROT_HALF + rest), F32)], axis=1)
    sb = jnp.concatenate([jnp.zeros((n, ROT_HALF), F32), sin, jnp.zeros((n, rest), F32)], axis=1)
    return tuple(jnp.tile(t, (1, LANES // NSA_HEAD_DIM)) for t in (cm, sa, sb))


def _params(sem, big=False):
    return pltpu.CompilerParams(dimension_semantics=sem, vmem_limit_bytes=VMEM_LIMIT if big else None)


FF_CHUNK = D_FF


def _ffn_ln_kernel(x_ref, wg_ref, wu_ref, wd_ref, lng_ref, lnb_ref, o_ref):
    x = x_ref[...]
    xb = x.astype(MXU_DTYPE)
    acc = jnp.zeros(x.shape, F32)
    for c in range(0, D_FF, FF_CHUNK):
        gate = jnp.dot(xb, wg_ref[:, c:c + FF_CHUNK], preferred_element_type=F32)
        up = jnp.dot(xb, wu_ref[:, c:c + FF_CHUNK], preferred_element_type=F32)
        acc = acc + _dot(gate * _sigmoid(gate) * up, wd_ref[c:c + FF_CHUNK, :])
    o_ref[...] = _layer_norm(DN_ALPHA * x + FFN_RES * acc, lng_ref[...], lnb_ref[...])


def ffn_ln(x, wg, wu, wd, lng, lnb):
    n, d = x.shape
    tm = min(n, 512)
    resident = lambda a: pl.BlockSpec(a.shape, lambda i: (0, 0), pipeline_mode=pl.Buffered(1))
    return pl.pallas_call(
        _ffn_ln_kernel,
        grid=(n // tm,),
        in_specs=[pl.BlockSpec((tm, d), lambda i: (i, 0)), resident(wg), resident(wu), resident(wd),
                  resident(lng), resident(lnb)],
        out_specs=pl.BlockSpec((tm, d), lambda i: (i, 0)),
        out_shape=jax.ShapeDtypeStruct((n, d), F32),
        compiler_params=_params(("parallel",), big=True),
        name="ffn_ln",
    )(x, wg, wu, wd, lng, lnb)


def _in_proj_kernel(x_ref, w_ref, cm_ref, sa_ref, sb_ref, cw_ref, cprev_ref,
                    u_ref, q_ref, cmp_ref, sel_ref, win_ref, g_ref, yc_ref, cst_ref, *rest, decode):
    if decode:
        (vbuf_ref,) = rest
    else:
        selk_ref, selv_ref, wink_ref, winv_ref, cmpt_ref, vbuf_ref = rest
    x = x_ref[0]
    rows = x.shape[0]
    z = _dot(x, w_ref[...])
    cm, sa, sb = cm_ref[...], sa_ref[...], sb_ref[...]

    u_ref[...] = z[:, C_U:C_U + S5_WIDTH]
    scale = NSA_HEAD_DIM ** -0.5
    for h in range(NSA_HEADS):
        c = C_Q + h * LANES
        q_ref[0, :, h * LANES:(h + 1) * LANES] = (_rope_lanes(z[:, c:c + LANES], cm, sa, sb) * scale).astype(q_ref.dtype)
    cmp_ref[0] = z[:, C_KV:C_KV + ROW_WIDTH]
    ks = _rope_lanes(z[:, C_KV + 2 * KV_WIDTH:C_KV + 3 * KV_WIDTH], cm, sa, sb)
    vs = z[:, C_KV + 3 * KV_WIDTH:C_KV + 4 * KV_WIDTH]
    kw = _rope_lanes(z[:, C_KV + 4 * KV_WIDTH:C_KV + 5 * KV_WIDTH], cm, sa, sb)
    vw = z[:, C_KV + 5 * KV_WIDTH:C_KV + 6 * KV_WIDTH]
    g_ref[0] = _sigmoid(z[:, C_G:C_G + LANES])
    if decode:
        sel_ref[0, :, 0:KV_WIDTH] = ks
        sel_ref[0, :, KV_WIDTH:ROW_WIDTH] = vs
        win_ref[0, :, 0:KV_WIDTH] = kw
        win_ref[0, :, KV_WIDTH:ROW_WIDTH] = vw
    else:
        for kv, (s_kv, w_kv) in enumerate(((ks, kw), (vs, vw))):
            cmpt_ref[0, kv] = z[:, C_KV + kv * KV_WIDTH:C_KV + (kv + 1) * KV_WIDTH].T
            sel_ref[0, kv] = s_kv.T
            win_ref[0, kv] = w_kv.T
        lane = lax.broadcasted_iota(jnp.int32, (rows, LANES), 1)
        pos = pl.program_id(1) * rows + lax.broadcasted_iota(jnp.int32, (rows, LANES), 0)
        first = lane < NSA_HEAD_DIM
        tag = jnp.where(lane - NSA_HEAD_DIM == lax.div(pos, SEL_LEN), 1.0, 0.0)
        for k in range(NSA_KV_HEADS):
            ks_k = ks if k == 0 else pltpu.roll(ks, NSA_HEAD_DIM, 1)
            kw_k = kw if k == 0 else pltpu.roll(kw, NSA_HEAD_DIM, 1)
            selk_ref[0, k] = jnp.where(first, ks_k, tag).astype(selk_ref.dtype)
            wink_ref[0, k] = jnp.where(first, kw_k, 0.0).astype(wink_ref.dtype)
        selv_ref[0] = vs.astype(selv_ref.dtype)
        winv_ref[0] = vw.astype(winv_ref.dtype)

    cb = z[:, C_CONV:C_CONV + CONV_CH]
    v = z[:, C_CONV + CONV_CH:C_CONV + 2 * CONV_CH] * z[:, C_CONV + 2 * CONV_CH:C_CONV + 3 * CONV_CH]
    w0, w1, w2 = cw_ref[0:1, :], cw_ref[1:2, :], cw_ref[2:3, :]
    if decode:
        p0 = cprev_ref[:, 0:CONV_CH]
        p1 = cprev_ref[:, CONV_CH:2 * CONV_CH]
        yc_ref[0] = (cb * (w0 * p0 + w1 * p1 + w2 * v)).astype(yc_ref.dtype)
        cst_ref[:, 0:CONV_CH] = p1
        cst_ref[:, CONV_CH:2 * CONV_CH] = v
    else:
        @pl.when(pl.program_id(1) == 0)
        def _():
            vbuf_ref[0:8, :] = jnp.zeros((8, CONV_CH), F32)

        vbuf_ref[8:8 + rows, :] = v
        conv = w0 * vbuf_ref[6:6 + rows, :] + w1 * vbuf_ref[7:7 + rows, :] + w2 * v
        yc_ref[0] = (cb * conv).astype(yc_ref.dtype)
        last = vbuf_ref[rows:rows + 8, :]
        vbuf_ref[0:8, :] = last
        cst_ref[0] = last[6:8, :]


def in_proj(x, w, tables, conv_w, conv_prev, *, decode):
    nb, t, d = x.shape
    tt = t if decode else min(t, 512)
    grid = (nb, t // tt)
    keep = min(WINDOW, t)
    assert decode or (keep % tt == 0 and t % tt == 0)
    first_kept = (t - keep) // tt
    cm, sa, sb = tables
    row3 = lambda w_: pl.BlockSpec((1, tt, w_), lambda b, i: (b, i, 0))
    tab = pl.BlockSpec((tt, LANES), lambda b, i: (i, 0))
    if decode:
        cprev_spec = pl.BlockSpec((t, 2 * CONV_CH), lambda b, i: (0, 0))
        cst_spec = pl.BlockSpec((t, 2 * CONV_CH), lambda b, i: (0, 0))
        cst_shape = jax.ShapeDtypeStruct((t, 2 * CONV_CH), F32)
    else:
        cprev_spec = pl.BlockSpec((8, LANES), lambda b, i: (0, 0))
        cst_spec = pl.BlockSpec((1, CONV_K - 1, CONV_CH), lambda b, i: (b, 0, 0))
        cst_shape = jax.ShapeDtypeStruct((nb, CONV_K - 1, CONV_CH), F32)
    slabs = (jax.ShapeDtypeStruct((nb, 2, KV_WIDTH, t), F32),
             pl.BlockSpec((1, 2, KV_WIDTH, tt), lambda b, i: (b, 0, 0, i)))
    win_slabs = (jax.ShapeDtypeStruct((nb, 2, KV_WIDTH, keep), F32),
                 pl.BlockSpec((1, 2, KV_WIDTH, tt), lambda b, i: (b, 0, 0, jnp.maximum(i - first_kept, 0))))
    rows_out = (jax.ShapeDtypeStruct((nb, t, ROW_WIDTH), F32), row3(ROW_WIDTH))
    sel_out, win_out = (rows_out, rows_out) if decode else (slabs, win_slabs)
    out_shape = (
        jax.ShapeDtypeStruct((t, nb * S5_WIDTH), F32),
        jax.ShapeDtypeStruct((nb, t, QP_WIDTH), MXU_DTYPE),
        rows_out[0],
        sel_out[0], win_out[0],
        jax.ShapeDtypeStruct((nb, t, LANES), F32),
        jax.ShapeDtypeStruct((nb, t, CONV_CH), MXU_DTYPE),
        cst_shape,
    )
    out_specs = (pl.BlockSpec((tt, S5_WIDTH), lambda b, i: (i, b)), row3(QP_WIDTH), rows_out[1], sel_out[1], win_out[1],
                 row3(LANES), row3(CONV_CH), cst_spec)
    if not decode:
        assert t <= NSA_HEAD_DIM * SEL_LEN
        per_head = (jax.ShapeDtypeStruct((nb, NSA_KV_HEADS, t, LANES), MXU_DTYPE),
                    pl.BlockSpec((1, NSA_KV_HEADS, tt, LANES), lambda b, i: (b, 0, i, 0)))
        packed = (jax.ShapeDtypeStruct((nb, t, KV_WIDTH), MXU_DTYPE), row3(KV_WIDTH))
        extra = (per_head, packed, per_head, packed, slabs)
        out_shape += tuple(e[0] for e in extra)
        out_specs += tuple(e[1] for e in extra)
    return pl.pallas_call(
        functools.partial(_in_proj_kernel, decode=decode),
        grid=grid,
        in_specs=[row3(d), pl.BlockSpec((d, C_END), lambda b, i: (0, 0)), tab, tab, tab,
                  pl.BlockSpec((8, CONV_CH), lambda b, i: (0, 0)), cprev_spec],
        out_specs=out_specs,
        out_shape=out_shape,
        scratch_shapes=[pltpu.VMEM((tt + 8, CONV_CH), F32)],
        compiler_params=_params(("parallel", "arbitrary"), big=True),
        name="in_proj",
    )(x, w, cm, sa, sb, conv_w, conv_prev)


S5_COLS = 512


def _s5_kernel(u_ref, s0r_ref, s0i_ref, ar_ref, ai_ref, br_ref, bi_ref, cr_ref, ci_ref, d_ref, wglu_ref, bglu_ref,
               y_ref, sr_ref, si_ref, xr_ref, xi_ref, *shuffle, nb, tc, wide):
    @pl.when(pl.program_id(0) == 0)
    def _():
        sr_ref[...] = s0r_ref[...]
        si_ref[...] = s0i_ref[...]

    slabs = S5_WIDTH // LANES
    if wide:
        uio_ref, yio_ref = shuffle
        for b in range(nb):
            for c in range(slabs):
                lanes = slice(b * S5_WIDTH + c * LANES, b * S5_WIDTH + (c + 1) * LANES)
                uio_ref[c, pl.ds(b, tc, stride=nb), :] = u_ref[:, lanes]
        u = jnp.concatenate([uio_ref[c] for c in range(slabs)], axis=1)
    else:
        u = u_ref[...]
    ub = u.astype(MXU_DTYPE)
    hw, hf = S5_WIDTH // 2, S5_FLAT // 2
    for h in range(2):
        xr_ref[:, h * hf:(h + 1) * hf] = jnp.dot(ub[:, h * hw:(h + 1) * hw], br_ref[h], preferred_element_type=F32)
        xi_ref[:, h * hf:(h + 1) * hf] = jnp.dot(ub[:, h * hw:(h + 1) * hw], bi_ref[h], preferred_element_type=F32)

    for c in range(S5_FLAT // S5_COLS):
        cols = slice(c * S5_COLS, (c + 1) * S5_COLS)
        ar = jnp.broadcast_to(ar_ref[:, cols], (nb, S5_COLS))
        ai = jnp.broadcast_to(ai_ref[:, cols], (nb, S5_COLS))

        def step(t, carry):
            sr, si = carry
            rows = pl.ds(pl.multiple_of(t * nb, nb), nb)
            nr = ar * sr - ai * si + xr_ref[rows, cols]
            ni = ar * si + ai * sr + xi_ref[rows, cols]
            xr_ref[rows, cols] = nr
            xi_ref[rows, cols] = ni
            return nr, ni

        sr, si = lax.fori_loop(0, tc, step, (sr_ref[:, cols], si_ref[:, cols]), unroll=min(tc, 8))
        sr_ref[:, cols] = sr
        si_ref[:, cols] = si

    y = jnp.concatenate([_dot(xr_ref[:, h * hf:(h + 1) * hf], cr_ref[h]) - _dot(xi_ref[:, h * hf:(h + 1) * hf], ci_ref[h])
                         for h in range(2)], axis=1) + d_ref[...] * u
    y = _gelu_tanh(y)
    y = y * _sigmoid(_dot(y, wglu_ref[...]) + bglu_ref[...])
    if wide:
        for c in range(slabs):
            yio_ref[c] = y[:, c * LANES:(c + 1) * LANES]
        for b in range(nb):
            for c in range(slabs):
                lanes = slice(b * S5_WIDTH + c * LANES, b * S5_WIDTH + (c + 1) * LANES)
                y_ref[:, lanes] = yio_ref[c, pl.ds(b, tc, stride=nb), :].astype(y_ref.dtype)
    else:
        y_ref[...] = y.astype(y_ref.dtype)


def s5_branch(u, s0r, s0i, p, *, nb, wide):
    if wide:
        t = u.shape[0]
        tc = min(t, 128)
        block = (tc, nb * S5_WIDTH)
        shuffle = [pltpu.VMEM((S5_WIDTH // LANES, tc * nb, LANES), F32)] * 2
    else:
        t = u.shape[0] // nb
        tc = min(t, 128)
        block = (tc * nb, S5_WIDTH)
        shuffle = []
    full = lambda a: pl.BlockSpec(a.shape, lambda i: (0,) * a.ndim)
    args = (u, s0r, s0i, p['a_re'], p['a_im'], p['bb_re'], p['bb_im'], p['c_re'], p['c_im'], p['d'], p['w_glu'], p['b_glu'])
    return pl.pallas_call(
        functools.partial(_s5_kernel, nb=nb, tc=tc, wide=wide),
        grid=(t // tc,),
        in_specs=[pl.BlockSpec(block, lambda i: (i, 0))] + [full(a) for a in args[1:]],
        out_specs=(pl.BlockSpec(block, lambda i: (i, 0)),
                   pl.BlockSpec((nb, S5_FLAT), lambda i: (0, 0)),
                   pl.BlockSpec((nb, S5_FLAT), lambda i: (0, 0))),
        out_shape=(jax.ShapeDtypeStruct(u.shape, MXU_DTYPE),
                   jax.ShapeDtypeStruct((nb, S5_FLAT), F32),
                   jax.ShapeDtypeStruct((nb, S5_FLAT), F32)),
        scratch_shapes=[pltpu.VMEM((tc * nb, S5_FLAT), F32), pltpu.VMEM((tc * nb, S5_FLAT), F32)] + shuffle,
        compiler_params=_params(("arbitrary",), big=True),
        name="s5",
    )(*args)


def _s5_params(lam_re, lam_im, log_dt, b_re, b_im, c_re, c_im, d_skip, w_glu, b_glu):
    dt = jnp.exp(log_dt.astype(F32))[:, None]
    lr, li = lam_re.astype(F32), lam_im.astype(F32)
    mag = jnp.exp(lr * dt)
    a_re = mag * jnp.cos(li * dt)
    a_im = mag * jnp.sin(li * dt)
    den = lr * lr + li * li
    r_re = ((a_re - 1.0) * lr + a_im * li) / den
    r_im = (a_im * lr - (a_re - 1.0) * li) / den
    bb_re = r_re[..., None] * b_re - r_im[..., None] * b_im
    bb_im = r_re[..., None] * b_im + r_im[..., None] * b_re
    eye = jnp.eye(S5_GROUPS, dtype=F32)
    hw, hf = S5_WIDTH // 2, S5_FLAT // 2
    halves = lambda m: jnp.stack([m[:m.shape[0] // 2, :m.shape[1] // 2], m[m.shape[0] // 2:, m.shape[1] // 2:]])
    blk_in = lambda bb: halves(jnp.einsum('gpi,gh->gihp', bb, eye).reshape(S5_WIDTH, S5_FLAT)).astype(MXU_DTYPE)
    blk_out = lambda c: halves(jnp.einsum('gop,gh->gpho', c.astype(F32), eye).reshape(S5_FLAT, S5_WIDTH)).astype(MXU_DTYPE)
    return dict(a_re=a_re.reshape(1, S5_FLAT), a_im=a_im.reshape(1, S5_FLAT),
                bb_re=blk_in(bb_re), bb_im=blk_in(bb_im), c_re=blk_out(c_re), c_im=blk_out(c_im),
                d=d_skip.reshape(1, S5_WIDTH).astype(F32), w_glu=w_glu.astype(MXU_DTYPE),
                b_glu=b_glu.reshape(1, S5_WIDTH).astype(F32))


CHUNK_WIDTH = CMP_STRIDE * ROW_WIDTH
HID_WIDTH = 4 * CMP_HIDDEN


def _compress_tail(ha, hb_next, hpe, w2_ref, cm, sa, sb):
    kv = _dot(_gelu_tanh(ha + hb_next + hpe), w2_ref[...])
    return _rope_lanes(kv[:, 0:KV_WIDTH], cm, sa, sb), kv[:, KV_WIDTH:ROW_WIDTH]


def _compress_kernel(x_ref, w1a_ref, w1b_ref, pea_ref, peb_ref, w2_ref, cm_ref, sa_ref, sb_ref,
                     kc_ref, vc_ref, hpe_ref, hb_ref):
    x = x_ref[0].astype(MXU_DTYPE)
    n = x.shape[0]
    ha = jnp.dot(x, w1a_ref[...], preferred_element_type=F32)
    hb_ref[0:n, :] = jnp.dot(x, w1b_ref[...], preferred_element_type=F32)
    hb_ref[n:n + 8, :] = jnp.zeros((8, HID_WIDTH), F32)
    hpe = _dot(pea_ref[...], w1a_ref[...]) + _dot(peb_ref[...], w1b_ref[...])
    hpe_ref[...] = hpe
    kc, vc = _compress_tail(ha, hb_ref[1:n + 1, :], hpe[0:1, :], w2_ref, cm_ref[...], sa_ref[...], sb_ref[...])
    first = lax.broadcasted_iota(jnp.int32, kc.shape, 1) < NSA_HEAD_DIM
    kc_ref[0, 0] = jnp.where(first, kc, 0.0).astype(kc_ref.dtype)
    kc_ref[0, 1] = jnp.where(first, pltpu.roll(kc, NSA_HEAD_DIM, 1), 0.0).astype(kc_ref.dtype)
    vc_ref[0] = vc.astype(vc_ref.dtype)


def compress(rows, cw, tables):
    b, n, _ = rows.shape
    cm, sa, sb = tables
    full = lambda a: pl.BlockSpec(a.shape, lambda i: (0,) * a.ndim)
    args = (rows, cw['w1a'], cw['w1b'], cw['pe_a'], cw['pe_b'], cw['w2'], cm, sa, sb)
    return pl.pallas_call(
        _compress_kernel,
        grid=(b,),
        in_specs=[pl.BlockSpec((1, n, CHUNK_WIDTH), lambda i: (i, 0, 0))] + [full(a) for a in args[1:]],
        out_specs=(pl.BlockSpec((1, NSA_KV_HEADS, n, LANES), lambda i: (i, 0, 0, 0)),
                   pl.BlockSpec((1, n, KV_WIDTH), lambda i: (i, 0, 0)),
                   pl.BlockSpec((8, HID_WIDTH), lambda i: (0, 0))),
        out_shape=(jax.ShapeDtypeStruct((b, NSA_KV_HEADS, n, LANES), MXU_DTYPE),
                   jax.ShapeDtypeStruct((b, n, KV_WIDTH), MXU_DTYPE),
                   jax.ShapeDtypeStruct((8, HID_WIDTH), F32)),
        scratch_shapes=[pltpu.VMEM((n + 8, HID_WIDTH), F32)],
        compiler_params=_params(("arbitrary",), big=True),
        name="compress",
    )(*args)


def _compress_params(phi_k1, phi_k2, phi_v1, phi_v2, pe_k, pe_v):
    wk = phi_k1.reshape(CMP_LEN, NSA_HEAD_DIM, CMP_HIDDEN)
    wv = phi_v1.reshape(CMP_LEN, NSA_HEAD_DIM, CMP_HIDDEN)
    w = jnp.stack([wk, wk, wv, wv], axis=1)
    w1 = jnp.einsum('sjdu,jm->sjdmu', w, jnp.eye(4, dtype=F32)).reshape(CMP_LEN * ROW_WIDTH, HID_WIDTH)
    w2 = jnp.einsum('jud,jm->jumd', jnp.stack([phi_k2, phi_k2, phi_v2, phi_v2]), jnp.eye(4, dtype=F32))
    pe = jnp.concatenate([pe_k, pe_k, pe_v, pe_v], axis=1)
    pad8 = lambda r: jnp.concatenate([r, jnp.zeros((7, CHUNK_WIDTH), F32)], axis=0)

    def pair_weights(w3):
        halves = jnp.stack([w3[:CMP_STRIDE], w3[CMP_STRIDE:]]).reshape(2, CMP_STRIDE // 2, 2, NSA_HEAD_DIM, CMP_HIDDEN)
        full = jnp.einsum('pqidu,hg->qihdpgu', halves, jnp.eye(NSA_KV_HEADS, dtype=F32))
        return full.reshape(CMP_STRIDE // 2, 2 * KV_WIDTH, 2 * NSA_KV_HEADS * CMP_HIDDEN).astype(MXU_DTYPE)

    return dict(wk_pair=pair_weights(wk), wv_pair=pair_weights(wv),
                w1a=w1[:CHUNK_WIDTH].astype(MXU_DTYPE), w1b=w1[CHUNK_WIDTH:].astype(MXU_DTYPE),
                w2=w2.reshape(HID_WIDTH, ROW_WIDTH).astype(MXU_DTYPE),
                pe_a=pad8(pe[:CMP_STRIDE].reshape(1, CHUNK_WIDTH)), pe_b=pad8(pe[CMP_STRIDE:].reshape(1, CHUNK_WIDTH)))


def _cmp_to_sel_map(n_cmp, n_rows, n_sel):
    n_cols = -(-n_sel // LANES) * LANES
    c0 = np.arange(n_rows) * CMP_STRIDE
    s0 = np.arange(n_cols) * SEL_LEN
    m = (c0[:, None] < s0[None, :] + SEL_LEN) & (s0[None, :] < c0[:, None] + CMP_LEN)
    m &= (np.arange(n_rows) < n_cmp)[:, None] & (np.arange(n_cols) < n_sel)[None, :]
    return jnp.asarray(m.astype(np.float32))


def _masked_softmax_rows(s, mask):
    s = jnp.where(mask, s, MASK_NEG)
    m = jnp.max(s, axis=-1, keepdims=True)
    e = jnp.where(mask, jnp.exp(s - m), 0.0)
    return e * (1.0 / jnp.maximum(jnp.sum(e, axis=-1, keepdims=True), 1e-30))


def _biased_softmax_rows(s):
    m = jnp.maximum(jnp.max(s, axis=-1, keepdims=True), M_INIT)
    e = jnp.exp(s - m)
    return e * (1.0 / jnp.maximum(jnp.sum(e, axis=-1, keepdims=True), 1e-30))


def _flash_update(slot, rows, parts, m_ref, l_ref, acc_ref):
    m_prev = m_ref[slot, rows, :]
    s_max = functools.reduce(jnp.maximum, [s for s, _ in parts])
    m_new = jnp.maximum(m_prev, jnp.max(s_max, axis=-1, keepdims=True))
    alpha = jnp.exp(m_prev - m_new)
    acc_new = alpha * acc_ref[slot, rows, :]
    p_sum = None
    for s, v in parts:
        p = jnp.exp(s - jnp.concatenate([m_new] * (s.shape[1] // LANES), axis=1))
        p_sum = p if p_sum is None else p_sum + p
        acc_new = acc_new + _dot(p, v)
    l_ref[slot, rows, :] = alpha * l_ref[slot, rows, :] + jnp.sum(p_sum, axis=-1, keepdims=True)
    acc_ref[slot, rows, :] = acc_new
    m_ref[slot, rows, :] = m_new


SEL, WIN = 0, 1


def _block_bias_t(score_ref, rank_ref, n_sel, last_valid):
    rows, tq = score_ref.shape
    n_groups = rows // 8
    rank_ref[...] = jnp.zeros((rows, tq), F32)
    sub = lax.broadcasted_iota(jnp.int32, (8, tq), 0)
    for c in range(-(-n_sel // 8)):
        @pl.when(c * 8 <= last_valid)
        def _():
            groups = [score_ref[v * 8:(v + 1) * 8, :] for v in range(n_groups)]
            rank = [rank_ref[v * 8:(v + 1) * 8, :] for v in range(n_groups)]
            for j in range(c * 8, min(c * 8 + 8, n_sel)):
                r = jnp.broadcast_to(score_ref[j:j + 1, :], (8, tq))
                for v, s in enumerate(groups):
                    if v * 8 > j:
                        ahead = r >= s
                    elif v * 8 + 7 <= j:
                        ahead = r > s
                    else:
                        ahead = (r > s) | ((r == s) & (sub + v * 8 > j))
                    rank[v] = rank[v] + jnp.where(ahead, 1.0, 0.0)
            for v in range(n_groups):
                rank_ref[v * 8:(v + 1) * 8, :] = rank[v]
    return jnp.where(rank_ref[...] < float(min(SEL_TOP, n_sel)), 0.0, MASK_NEG)


def _nsa_prompt_kernel(q_ref, g_ref, kc_ref, vc_ref, selk_ref, selv_ref, wink_ref, winv_ref, map_ref, o_ref,
                       sc_ref, rank_ref, qa_ref, oc_ref, m_ref, l_ref, acc_ref, *, tq, nc, n_sel):
    qi = pl.program_id(1)
    tk = tq
    nw = WINDOW // tk
    q0 = qi * tq
    qpos = q0 + lax.broadcasted_iota(jnp.int32, (tq, 1), 0)
    ncp = vc_ref.shape[1]
    n_idx = lax.broadcasted_iota(jnp.int32, (1, ncp), 1)
    cbias = jnp.where((CMP_STRIDE * n_idx + (CMP_LEN - 1) <= qpos) & (n_idx < nc), 0.0, MASK_NEG)
    last_valid = lax.div(q0 + tq - 1, SEL_LEN)
    vc = vc_ref[0]
    nsr = sc_ref.shape[0]
    blk_t = lax.broadcasted_iota(jnp.int32, (nsr, tq), 0)
    cur_t = lax.div(q0 + lax.broadcasted_iota(jnp.int32, (nsr, tq), 1), SEL_LEN)
    valid_t = blk_t <= cur_t
    forced_t = valid_t & ((blk_t == 0) | (blk_t >= cur_t - 1))
    gates = g_ref[0]
    lane_half = lax.div(lax.broadcasted_iota(jnp.int32, (tq, LANES), 1), NSA_HEAD_DIM)
    r_idx = lax.broadcasted_iota(jnp.int32, (tq, tk), 0)
    c_idx = lax.broadcasted_iota(jnp.int32, (tq, tk), 1)
    causal_bias = jnp.where(c_idx <= r_idx, 0.0, MASK_NEG)
    oldest_bias = jnp.where(c_idx > r_idx, 0.0, MASK_NEG)

    qslice = lambda h: q_ref[0, :, h * LANES:(h + 1) * LANES]
    group_rows = lambda gq: slice(gq * tq, (gq + 1) * tq)
    slot = lambda k, branch: k * 2 + branch

    for k in range(NSA_KV_HEADS):
        for gq in range(NSA_GQ):
            qa_ref[slot(k, WIN), group_rows(gq), :] = qslice(k * NSA_GQ + gq)
        p = _biased_softmax_rows(_dot_nt(qa_ref[slot(k, WIN)], kc_ref[0, k]) + jnp.concatenate([cbias] * NSA_GQ, axis=0))
        oc = _dot(p, vc)
        psum = jnp.zeros((tq, ncp), F32)
        for gq in range(NSA_GQ):
            psum = psum + p[group_rows(gq), :]
            h = k * NSA_GQ + gq
            oc_ref[h] = gates[:, 3 * h:3 * h + 1] * oc[group_rows(gq), :]
        psum_hi = psum.astype(MXU_DTYPE)
        psum_lo = psum - psum_hi.astype(F32)
        imp = _dot(psum_hi, map_ref[...]) + _dot(psum_lo, map_ref[...])
        sc_ref[...] = jnp.where(forced_t, FORCE_SCORE, jnp.where(valid_t, imp.T[0:nsr, :], -jnp.inf))

        bias_t = jnp.concatenate([jnp.zeros((NSA_HEAD_DIM, tq), F32), _block_bias_t(sc_ref, rank_ref, n_sel, last_valid)]
                                 + ([jnp.zeros((NSA_HEAD_DIM - nsr, tq), F32)] if nsr < NSA_HEAD_DIM else []), axis=0)
        bias = bias_t.T.astype(MXU_DTYPE)
        for gq in range(NSA_GQ):
            qa_ref[slot(k, SEL), group_rows(gq), :] = qslice(k * NSA_GQ + gq) + bias
        for branch in (SEL, WIN):
            m_ref[slot(k, branch)] = jnp.full((NSA_GQ * tq, LANES), M_INIT, F32)
            l_ref[slot(k, branch)] = jnp.zeros((NSA_GQ * tq, LANES), F32)
            acc_ref[slot(k, branch)] = jnp.zeros((NSA_GQ * tq, LANES), F32)

    def attend(tiles, k_ref, v_ref, branch):
        for r0 in range(0, NSA_GQ * tq, 2 * tq):
            rows = slice(r0, r0 + 2 * tq)
            for k in range(NSA_KV_HEADS):
                q = qa_ref[slot(k, branch), rows, :]
                parts = []
                for j, bias_tile in tiles:
                    k0 = pl.multiple_of(j * tk, tk)
                    s = _dot_nt(q, k_ref[0, k, pl.ds(k0, tk), :])
                    if bias_tile is not None:
                        s = s + jnp.concatenate([bias_tile] * 2, axis=0)
                    parts.append((s, v_ref[0, pl.ds(k0, tk), :]))
                _flash_update(slot(k, branch), rows, parts, m_ref, l_ref, acc_ref)

    lax.fori_loop(0, lax.div(qi, 2),
                  lambda jj, c: (attend([(2 * jj, None), (2 * jj + 1, None)], selk_ref, selv_ref, SEL), c)[1], 0)

    @pl.when(lax.rem(qi, 2) == 1)
    def _():
        attend([(qi - 1, None), (qi, causal_bias)], selk_ref, selv_ref, SEL)

    @pl.when(lax.rem(qi, 2) == 0)
    def _():
        attend([(qi, causal_bias)], selk_ref, selv_ref, SEL)

    win_tiles = []
    for i in range(nw + 1):
        j = qi - nw + i
        edge = oldest_bias if i == 0 else causal_bias if i == nw else jnp.zeros((tq, tk), F32)
        win_tiles.append((jnp.maximum(j, 0), edge if i == nw else jnp.where(j >= 0, edge, MASK_NEG)))
    attend(win_tiles, wink_ref, winv_ref, WIN)

    for h in range(NSA_HEADS):
        k, gq = divmod(h, NSA_GQ)
        rows = group_rows(gq)
        o_s = acc_ref[slot(k, SEL), rows, :] / jnp.maximum(l_ref[slot(k, SEL), rows, :], 1e-30)
        o_w = acc_ref[slot(k, WIN), rows, :] / jnp.maximum(l_ref[slot(k, WIN), rows, :], 1e-30)
        c = 3 * h
        out = oc_ref[h] + gates[:, c + 1:c + 2] * o_s + gates[:, c + 2:c + 3] * o_w
        o_ref[0, :, h * LANES:(h + 1) * LANES] = jnp.where(lane_half == k, out, 0.0).astype(o_ref.dtype)


def nsa_prompt(qp, gates, kc, vc, selk, selv, wink, winv):
    b, t, _ = qp.shape
    tq = min(t, 256)
    assert WINDOW % tq == 0 and t % tq == 0
    ncp = vc.shape[1]
    nc = ncp - 1
    n_sel = -(-t // SEL_LEN)
    nsr = -(-n_sel // 8) * 8
    assert nsr <= NSA_HEAD_DIM
    cmap = _cmp_to_sel_map(nc, ncp, n_sel).astype(MXU_DTYPE)
    per_b = lambda a: pl.BlockSpec((1,) + a.shape[1:], lambda i, j: (i,) + (0,) * (a.ndim - 1))
    return pl.pallas_call(
        functools.partial(_nsa_prompt_kernel, tq=tq, nc=nc, n_sel=n_sel),
        grid=(b, t // tq),
        in_specs=[pl.BlockSpec((1, tq, QP_WIDTH), lambda i, j: (i, j, 0)),
                  pl.BlockSpec((1, tq, LANES), lambda i, j: (i, j, 0)),
                  per_b(kc), per_b(vc), per_b(selk), per_b(selv), per_b(wink), per_b(winv),
                  pl.BlockSpec(cmap.shape, lambda i, j: (0, 0))],
        out_specs=pl.BlockSpec((1, tq, QP_WIDTH), lambda i, j: (i, j, 0)),
        out_shape=jax.ShapeDtypeStruct((b, t, QP_WIDTH), MXU_DTYPE),
        scratch_shapes=[pltpu.VMEM((nsr, tq), F32),
                        pltpu.VMEM((nsr, tq), F32),
                        pltpu.VMEM((2 * NSA_KV_HEADS, NSA_GQ * tq, LANES), MXU_DTYPE),
                        pltpu.VMEM((NSA_HEADS, tq, LANES), F32),
                        pltpu.VMEM((2 * NSA_KV_HEADS, NSA_GQ * tq, LANES), F32),
                        pltpu.VMEM((2 * NSA_KV_HEADS, NSA_GQ * tq, LANES), F32),
                        pltpu.VMEM((2 * NSA_KV_HEADS, NSA_GQ * tq, LANES), F32)],
        compiler_params=_params(("parallel", "parallel"), big=True),
        name="nsa_prompt",
    )(qp, gates, kc, vc, selk, selv, wink, winv, cmap)


PAGE_CHUNKS = PAGE_SIZE // CMP_STRIDE
PAGE_BLOCKS = PAGE_SIZE // SEL_LEN
PAGE_GROUP = 16
CHUNK_PITCH = 24


def _pages_view(pool):
    depth, n_pool, page = pool.shape[:3]
    return pool.transpose(0, 1, 3, 4, 5, 2).reshape(depth, n_pool, 2, KV_WIDTH, page)


def _paged_hidden_kernel(pt_ref, pool_ref, wk_ref, wv_ref, ha_ref, hb_ref, buf_ref, rows_ref, sem,
                         *, layer, pg, n_steps):
    step = pl.program_id(0) * pl.num_programs(1) + pl.program_id(1)
    slot = lax.rem(step, 2)

    def page_copy(s, sl, r):
        return pltpu.make_async_copy(pool_ref.at[layer, pt_ref[s * pg + r]], buf_ref.at[sl, r], sem.at[sl])

    def start_all(s, sl):
        lax.fori_loop(0, pg, lambda r, c: (page_copy(s, sl, r).start(), c)[1], 0, unroll=8)

    @pl.when(step == 0)
    def _():
        start_all(step, slot)

    @pl.when(step + 1 < n_steps)
    def _():
        start_all(step + 1, 1 - slot)

    lax.fori_loop(0, pg, lambda r, c: (page_copy(step, slot, r).wait(), c)[1], 0, unroll=8)

    half = NSA_KV_HEADS * CMP_HIDDEN
    for g0 in range(0, pg, PAGE_GROUP):
        for r in range(g0, min(g0 + PAGE_GROUP, pg)):
            for kv in range(2):
                page_rows = buf_ref[slot, r, kv].T
                for c in range(PAGE_CHUNKS):
                    dst = (r * PAGE_CHUNKS + c) * CHUNK_PITCH
                    rows_ref[kv, dst:dst + CMP_STRIDE, :] = page_rows[c * CMP_STRIDE:(c + 1) * CMP_STRIDE, :]
        n = min(PAGE_GROUP, pg - g0) * PAGE_CHUNKS
        c0 = g0 * PAGE_CHUNKS
        for kv, w_ref in enumerate((wk_ref, wv_ref)):
            acc = jnp.zeros((n, 2 * half), F32)
            for q in range(CMP_STRIDE // 2):
                x = jnp.concatenate([rows_ref[kv, pl.ds(c0 * CHUNK_PITCH + 2 * q + i, n, stride=CHUNK_PITCH), :]
                                     for i in range(2)], axis=1)
                acc = acc + _dot(x, w_ref[q])
            ha_ref[0, c0:c0 + n, kv * half:(kv + 1) * half] = acc[:, 0:half]
            hb_ref[0, c0:c0 + n, kv * half:(kv + 1) * half] = acc[:, half:2 * half]


def paged_hidden(pool, page_table, cw, *, layer):
    b, n_pages = page_table.shape
    pg = math.gcd(n_pages, 64)
    pages = _pages_view(pool)
    full = lambda a: pl.BlockSpec(a.shape, lambda i, j, pt: (0,) * a.ndim)
    out = jax.ShapeDtypeStruct((b, n_pages * PAGE_CHUNKS, HID_WIDTH), F32)
    out_spec = pl.BlockSpec((1, pg * PAGE_CHUNKS, HID_WIDTH), lambda i, j, pt: (i, j, 0))
    ws = (cw['wk_pair'], cw['wv_pair'])
    return pl.pallas_call(
        functools.partial(_paged_hidden_kernel, layer=layer, pg=pg, n_steps=b * (n_pages // pg)),
        grid_spec=pltpu.PrefetchScalarGridSpec(
            num_scalar_prefetch=1,
            grid=(b, n_pages // pg),
            in_specs=[pl.BlockSpec(memory_space=pl.ANY)] + [full(a) for a in ws],
            out_specs=(out_spec, out_spec),
            scratch_shapes=[pltpu.VMEM((2, pg, 2, KV_WIDTH, PAGE_SIZE), F32),
                            pltpu.VMEM((2, pg * PAGE_CHUNKS * CHUNK_PITCH, KV_WIDTH), F32),
                            pltpu.SemaphoreType.DMA((2,))]),
        out_shape=(out, out),
        compiler_params=_params(("arbitrary", "arbitrary"), big=True),
        name="paged_hidden",
    )(page_table.reshape(-1), pages, *ws)


def _own_half_queries(q8):
    q = q8.astype(F32)
    own_first = lax.div(lax.broadcasted_iota(jnp.int32, q.shape, 0), NSA_GQ) == 0
    return jnp.where(own_first, q, pltpu.roll(q, NSA_HEAD_DIM, 1))


SEL_KEYS = SEL_TOP * PAGE_SIZE
NO_KEY = 2 ** 30


def _decode_cmp_kernel(ha_ref, hb_ref, hpe_ref, new_ref, q_ref, w1b0_ref, w2_ref, cm_ref, sa_ref, sb_ref, map_ref,
                       oc_ref, idx_ref, kpos_ref, hbs_ref, *, q_pos, n_sel):
    for g in range(ha_ref.shape[0]):
        _decode_cmp_one(g, ha_ref, hb_ref, hpe_ref, new_ref, q_ref, w1b0_ref, w2_ref, cm_ref, sa_ref, sb_ref, map_ref,
                        oc_ref, idx_ref, kpos_ref, hbs_ref, q_pos=q_pos, n_sel=n_sel)


def _decode_cmp_one(g, ha_ref, hb_ref, hpe_ref, new_ref, q_ref, w1b0_ref, w2_ref, cm_ref, sa_ref, sb_ref, map_ref,
                    oc_ref, idx_ref, kpos_ref, hbs_ref, *, q_pos, n_sel):
    n = ha_ref.shape[1]
    hbs_ref[g, 0:n, :] = hb_ref[g]
    new8 = jnp.concatenate([new_ref[g], jnp.zeros((7, ROW_WIDTH), F32)], axis=0)
    hbs_ref[g, n:n + 8, :] = _dot(new8, w1b0_ref[...])
    kc, vc = _compress_tail(ha_ref[g], hbs_ref[g, 1:n + 1, :], hpe_ref[0:1, :], w2_ref,
                            cm_ref[...], sa_ref[...], sb_ref[...])
    q8 = _own_half_queries(q_ref[g])
    n_idx = lax.broadcasted_iota(jnp.int32, (1, n), 1)
    p = _masked_softmax_rows(_dot_nt(q8, kc), CMP_STRIDE * n_idx + (CMP_LEN - 1) <= q_pos)
    oc_ref[g] = _dot(p, vc)

    row = lax.broadcasted_iota(jnp.int32, p.shape, 0)
    psum = jnp.zeros(p.shape, F32)
    for k in range(NSA_KV_HEADS):
        grp = jnp.sum(jnp.where(lax.div(row, NSA_GQ) == k, p, 0.0), axis=0, keepdims=True)
        psum = jnp.where(row == k, grp, psum)
    psum_hi = psum.astype(MXU_DTYPE)
    imp = _dot(psum_hi, map_ref[...]) + _dot(psum - psum_hi.astype(F32), map_ref[...])
    nsp = imp.shape[1]
    blk = lax.broadcasted_iota(jnp.int32, (NSA_HEADS, nsp), 1)
    cur = q_pos // SEL_LEN
    forced = (blk == 0) | (blk >= cur - 1)
    score = jnp.where(blk < n_sel, jnp.where(forced, FORCE_SCORE, imp), -jnp.inf)
    blk_f = blk.astype(F32)
    lane = lax.broadcasted_iota(jnp.int32, (NSA_HEADS, LANES), 1)
    key = lax.broadcasted_iota(jnp.int32, (NSA_HEADS, SEL_KEYS), 1)
    key_slot = lax.div(key, PAGE_SIZE)
    key_off = key - key_slot * PAGE_SIZE
    idx = jnp.zeros((NSA_HEADS, LANES), jnp.int32)
    kpos = jnp.zeros((NSA_HEADS, SEL_KEYS), jnp.int32)
    for i in range(SEL_TOP):
        top = jnp.max(score, axis=1, keepdims=True)
        pick_f = jnp.min(jnp.where(score == top, blk_f, float(nsp)), axis=1, keepdims=True)
        pick = pick_f.astype(jnp.int32)
        idx = jnp.where(lane == i, pick, idx)
        pos = lax.div(pick, PAGE_BLOCKS) * PAGE_SIZE + key_off
        kpos = jnp.where(key_slot == i, jnp.where(lax.div(pos, SEL_LEN) == pick, pos, NO_KEY), kpos)
        score = jnp.where(blk == pick, -jnp.inf, score)
    idx_ref[g] = idx
    kpos_ref[g] = kpos


def decode_cmp(ha, hb, hpe, cmp_new, q8, cw, tables, cmap, *, q_pos, n_sel):
    b, n, _ = ha.shape
    cm, sa, sb = tables
    w1b0 = cw['w1b'][0:ROW_WIDTH]
    g = math.gcd(b, 2)
    per_b = lambda a: pl.BlockSpec((g,) + a.shape[1:], lambda i: (i, 0, 0))
    full = lambda a: pl.BlockSpec(a.shape, lambda i: (0,) * a.ndim)
    out8 = lambda w_, dt: (jax.ShapeDtypeStruct((b, NSA_HEADS, w_), dt), pl.BlockSpec((g, NSA_HEADS, w_), lambda i: (i, 0, 0)))
    outs = (out8(LANES, F32), out8(LANES, jnp.int32), out8(SEL_KEYS, jnp.int32))
    cmap = cmap.astype(MXU_DTYPE)
    return pl.pallas_call(
        functools.partial(_decode_cmp_kernel, q_pos=q_pos, n_sel=n_sel),
        grid=(b // g,),
        in_specs=[per_b(ha), per_b(hb), full(hpe), per_b(cmp_new), per_b(q8), full(w1b0), full(cw['w2']),
                  full(cm), full(sa), full(sb), full(cmap)],
        out_specs=tuple(o[1] for o in outs),
        out_shape=tuple(o[0] for o in outs),
        scratch_shapes=[pltpu.VMEM((g, n + 8, HID_WIDTH), F32)],
        compiler_params=_params(("parallel",), big=True),
        name="decode_cmp",
    )(ha, hb, hpe, cmp_new, q8, w1b0, cw['w2'], cm, sa, sb, cmap)


def _decode_attn_kernel(idx_ref, pt_ref, pool_ref, win_ref, seln_ref, winn_ref, q_ref, g_ref, oc_ref, kpos_ref,
                        o_ref, wst_ref, k_ref, v_ref, sem, *, layer, n_pages, q_pos):
    b = pl.program_id(0)
    n_past = n_pages * PAGE_BLOCKS
    keep = win_ref.shape[4]
    hd = NSA_HEAD_DIM

    def slot(k, n_):
        blk = idx_ref[(b * NSA_KV_HEADS + k) * SEL_TOP + n_]
        page = pt_ref[b * n_pages + jnp.minimum(lax.div(blk, PAGE_BLOCKS), n_pages - 1)]
        lanes = slice(n_ * PAGE_SIZE, (n_ + 1) * PAGE_SIZE)
        dims = slice(k * hd, (k + 1) * hd)
        return blk, lanes, (pltpu.make_async_copy(pool_ref.at[layer, page, 0, dims, :], k_ref.at[k, :, lanes], sem),
                            pltpu.make_async_copy(pool_ref.at[layer, page, 1, dims, :], v_ref.at[k, :, lanes], sem))

    for k in range(NSA_KV_HEADS):
        for n_ in range(SEL_TOP):
            blk, lanes, copies = slot(k, n_)

            @pl.when(blk < n_past)
            def _():
                for cp in copies:
                    cp.start()

            @pl.when(blk >= n_past)
            def _():
                k_ref[k, :, lanes] = jnp.zeros((hd, PAGE_SIZE), F32)
                v_ref[k, :, lanes] = jnp.zeros((hd, PAGE_SIZE), F32)

    row0 = lax.broadcasted_iota(jnp.int32, (LANES, ROW_WIDTH), 0) == 0
    win_new_t = jnp.where(row0, jnp.broadcast_to(winn_ref[0], (LANES, ROW_WIDTH)), 0.0).T
    last_lane = lax.broadcasted_iota(jnp.int32, (KV_WIDTH, keep), 1) == keep - 1
    state = []
    for kv in range(2):
        shifted = pltpu.roll(win_ref[0, 0, kv], keep - 1, 1)
        state.append(jnp.where(last_lane, win_new_t[kv * KV_WIDTH:(kv + 1) * KV_WIDTH, 0:1], shifted))
        wst_ref[0, kv] = state[kv]

    q8 = q_ref[0]
    qk = q8[:, 0:hd]
    row = lax.broadcasted_iota(jnp.int32, (NSA_HEADS, LANES), 0)
    lane_half = lax.div(lax.broadcasted_iota(jnp.int32, (NSA_HEADS, LANES), 1), hd)
    own_half = lane_half == lax.div(row, NSA_GQ)

    def to_half(o, k):
        z = jnp.zeros_like(o)
        return jnp.concatenate([o, z] if k == 0 else [z, o], axis=1)

    o_w = jnp.zeros((NSA_HEADS, LANES), F32)
    for k in range(NSA_KV_HEADS):
        s = _dot(qk, state[0][k * hd:(k + 1) * hd, :])
        e = jnp.exp(s - jnp.max(s, axis=1, keepdims=True))
        p = e / jnp.maximum(jnp.sum(e, axis=1, keepdims=True), 1e-30)
        o_w = jnp.where(lax.div(row, NSA_GQ) == k, to_half(_dot_nt(p, state[1][k * hd:(k + 1) * hd, :]), k), o_w)

    for k in range(NSA_KV_HEADS):
        for n_ in range(SEL_TOP):
            blk, lanes, copies = slot(k, n_)

            @pl.when(blk < n_past)
            def _():
                for cp in copies:
                    cp.wait()

    new_k = seln_ref[0][:, 0:KV_WIDTH]
    new_v = seln_ref[0][:, KV_WIDTH:ROW_WIDTH]
    s_new = jnp.sum(_own_half_queries(q8) * new_k, axis=1, keepdims=True)
    grp = lax.div(lax.broadcasted_iota(jnp.int32, (NSA_HEADS, 1), 0), NSA_GQ)
    o_s = jnp.zeros((NSA_HEADS, LANES), F32)
    e_new = jnp.zeros((NSA_HEADS, 1), F32)
    denom = jnp.ones((NSA_HEADS, 1), F32)
    for k in range(NSA_KV_HEADS):
        mask = kpos_ref[0, k:k + 1, :] < q_pos
        s = jnp.where(mask, _dot(qk, k_ref[k]), MASK_NEG)
        m = jnp.maximum(jnp.max(s, axis=1, keepdims=True), s_new)
        e = jnp.where(mask, jnp.exp(s - m), 0.0)
        en = jnp.exp(s_new - m)
        o_s = jnp.where(lax.div(row, NSA_GQ) == k, to_half(_dot_nt(e, v_ref[k]), k), o_s)
        e_new = jnp.where(grp == k, en, e_new)
        denom = jnp.where(grp == k, jnp.sum(e, axis=1, keepdims=True) + en, denom)
    o_s = (o_s + e_new * jnp.where(own_half, jnp.broadcast_to(new_v, (NSA_HEADS, LANES)), 0.0)) / denom

    g = g_ref[0]
    out = g[:, 0:1] * oc_ref[0] + g[:, 1:2] * o_s + g[:, 2:3] * o_w
    o_ref[0] = jnp.where(own_half, out, 0.0).astype(o_ref.dtype)


def decode_attn(idx, page_table, pool, win_cache, sel_new, win_new, q8, gates3, oc, kpos, *, layer, q_pos):
    depth = pool.shape[0]
    b, n_pages = page_table.shape
    keep = win_cache.shape[2]
    assert keep == WINDOW
    win = win_cache.transpose(0, 1, 3, 4, 5, 2).reshape(depth, b, 2, KV_WIDTH, keep)
    per_b = lambda a: pl.BlockSpec((1,) + a.shape[1:], lambda i, ix, pt: (i, 0, 0))
    idx_flat = idx[:, 0:NSA_KV_HEADS, 0:SEL_TOP].reshape(-1)
    return pl.pallas_call(
        functools.partial(_decode_attn_kernel, layer=layer, n_pages=n_pages, q_pos=q_pos),
        grid_spec=pltpu.PrefetchScalarGridSpec(
            num_scalar_prefetch=2,
            grid=(b,),
            in_specs=[pl.BlockSpec(memory_space=pl.ANY),
                      pl.BlockSpec((1, 1, 2, KV_WIDTH, keep), lambda i, ix, pt: (layer, i, 0, 0, 0)),
                      per_b(sel_new), per_b(win_new), per_b(q8), per_b(gates3), per_b(oc), per_b(kpos)],
            out_specs=(pl.BlockSpec((1, NSA_HEADS, LANES), lambda i, ix, pt: (i, 0, 0)),
                       pl.BlockSpec((1, 2, KV_WIDTH, keep), lambda i, ix, pt: (i, 0, 0, 0))),
            scratch_shapes=[pltpu.VMEM((NSA_KV_HEADS, NSA_HEAD_DIM, SEL_KEYS), F32),
                            pltpu.VMEM((NSA_KV_HEADS, NSA_HEAD_DIM, SEL_KEYS), F32),
                            pltpu.SemaphoreType.DMA(())]),
        out_shape=(jax.ShapeDtypeStruct((b, NSA_HEADS, LANES), MXU_DTYPE),
                   jax.ShapeDtypeStruct((b, 2, KV_WIDTH, keep), F32)),
        compiler_params=_params(("arbitrary",), big=True),
        name="decode_attn",
    )(idx_flat, page_table.reshape(-1), _pages_view(pool), win, sel_new, win_new, q8, gates3, oc, kpos)


def _merge_ln_kernel(x_ref, ya_ref, yb_ref, yc_ref, wgm_ref, wa_ref, wb_ref, wc_ref, wo_ref, lng_ref, lnb_ref, o_ref):
    x = x_ref[0]
    xb = x.astype(MXU_DTYPE)
    gate = lambda j: _sigmoid(jnp.dot(xb, wgm_ref[:, j * D_MODEL:(j + 1) * D_MODEL], preferred_element_type=F32))
    merged = (gate(0) * _dot(ya_ref[...], wa_ref[...]) + gate(1) * _dot(yb_ref[0], wb_ref[...])
              + gate(2) * _dot(yc_ref[0], wc_ref[...]))
    o_ref[0] = _layer_norm(DN_ALPHA * x + _dot(merged, wo_ref[...]), lng_ref[...], lnb_ref[...])


def merge_ln(x, ya, yb, yc, mw, lng, lnb):
    nb, t, d = x.shape
    tt = min(t, 512)
    row3 = lambda w_: pl.BlockSpec((1, tt, w_), lambda b, i: (b, i, 0))
    full = lambda a: pl.BlockSpec(a.shape, lambda b, i: (0,) * a.ndim, pipeline_mode=pl.Buffered(1))
    ws = (mw['w_gm'], mw['w_a'], mw['w_b'], mw['w_c'], mw['w_o'], lng, lnb)
    return pl.pallas_call(
        _merge_ln_kernel,
        grid=(nb, t // tt),
        in_specs=[row3(d), pl.BlockSpec((tt, S5_WIDTH), lambda b, i: (i, b)), row3(QP_WIDTH), row3(CONV_CH)]
                 + [full(a) for a in ws],
        out_specs=row3(d),
        out_shape=jax.ShapeDtypeStruct((nb, t, d), F32),
        compiler_params=_params(("parallel", "parallel"), big=True),
        name="merge_ln",
    )(x, ya, yb, yc, *ws)


def _layer_weights(w, l):
    bf = lambda a: a.astype(MXU_DTYPE)
    row = lambda a: a.reshape(1, -1).astype(F32)
    w_in = w['w_in'][l]
    w_u, w_q, w_kv, w_gn, w_conv, w_gm = jnp.split(w_in, IN_OFFSETS, axis=-1)
    head_half = (jnp.arange(NSA_HEADS) // NSA_GQ)[None, :, None]

    def pad_heads(a3, own_half):
        z = jnp.zeros_like(a3)
        lo = jnp.concatenate([a3, z], axis=-1)
        hi = jnp.concatenate([z, a3], axis=-1)
        return (jnp.where(head_half == 0, lo, hi) if own_half else lo).reshape(a3.shape[0], QP_WIDTH)

    w_qp = pad_heads(w_q.reshape(D_MODEL, NSA_HEADS, NSA_HEAD_DIM), False)
    w_gpad = jnp.concatenate([w_gn, jnp.zeros((D_MODEL, LANES - w_gn.shape[1]), F32)], axis=1)
    w_proj = bf(jnp.concatenate([w_u, w_qp, w_kv, w_gpad, w_conv], axis=1))
    assert w_proj.shape[1] == C_END
    w_b = pad_heads(w['w_proj_b'][l].T.reshape(D_MODEL, NSA_HEADS, NSA_HEAD_DIM), True).T
    conv_w = jnp.concatenate([w['conv_w'][l], jnp.zeros((8 - CONV_K, CONV_CH), F32)], axis=0)
    gu = lambda name: jnp.split(w[name][l], 2, axis=-1)
    return dict(
        ffn1=(bf(gu('ffn1_w_gu')[0]), bf(gu('ffn1_w_gu')[1]), bf(w['ffn1_w_down'][l])),
        ffn2=(bf(gu('ffn2_w_gu')[0]), bf(gu('ffn2_w_gu')[1]), bf(w['ffn2_w_down'][l])),
        ln=[(row(w['ln_g'][l, j]), row(w['ln_b'][l, j])) for j in range(3)],
        w_proj=w_proj, conv_w=conv_w,
        s5=_s5_params(w['s5_lambda_re'][l], w['s5_lambda_im'][l], w['s5_log_dt'][l], w['s5_b_re'][l], w['s5_b_im'][l],
                      w['s5_c_re'][l], w['s5_c_im'][l], w['s5_d'][l], w['s5_w_glu'][l], w['s5_b_glu'][l]),
        cmp=_compress_params(w['nsa_phi_k1'][l], w['nsa_phi_k2'][l], w['nsa_phi_v1'][l], w['nsa_phi_v2'][l],
                             w['nsa_pe_k'][l], w['nsa_pe_v'][l]),
        merge=dict(w_gm=bf(w_gm), w_a=bf(w['w_proj_a'][l]), w_b=bf(w_b), w_c=bf(w['w_proj_c'][l]), w_o=bf(w['w_o'][l])),
    )


def _prompt_trunk(x, layers):
    b, t, d = x.shape
    n_chunks = t // CMP_STRIDE
    tok_tables = _rope_tables(jnp.arange(t))
    cmp_tables = _rope_tables(jnp.arange(n_chunks) * CMP_STRIDE + (CMP_LEN - 1))
    zero_state = jnp.zeros((b, S5_FLAT), F32)
    states, hpes = [], []
    x2 = x.reshape(b * t, d)
    for lw in layers:
        x2 = ffn_ln(x2, *lw['ffn1'], *lw['ln'][0])
        u, qp, cmp_rows, sel_t, win_t, gates, yc, conv_state, selk, selv, wink, winv, cmp_t = in_proj(
            x2.reshape(b, t, d), lw['w_proj'], tok_tables, lw['conv_w'], jnp.zeros((8, LANES), F32), decode=False)
        ya, s_re, s_im = s5_branch(u, zero_state, zero_state, lw['s5'], nb=b, wide=True)
        kc, vc, hpe = compress(cmp_rows.reshape(b, n_chunks, CHUNK_WIDTH), lw['cmp'], cmp_tables)
        hpes.append(hpe)
        yb = nsa_prompt(qp, gates, kc, vc, selk, selv, wink, winv)
        x3 = merge_ln(x2.reshape(b, t, d), ya, yb, yc, lw['merge'], *lw['ln'][1])
        x2 = ffn_ln(x3.reshape(b * t, d), *lw['ffn2'], *lw['ln'][2])
        rows = lambda a: a.reshape(b, 2, NSA_KV_HEADS, NSA_HEAD_DIM, -1).transpose(0, 4, 1, 2, 3)
        states.append((rows(cmp_t), rows(sel_t), rows(win_t), conv_state,
                       s_re.reshape(b, S5_GROUPS, S5_STATE), s_im.reshape(b, S5_GROUPS, S5_STATE)))
    return x2.reshape(b, t, d), [jnp.stack(s, axis=0) for s in zip(*states)], hpes


def _sample_trunk(x, layers, hpes, cache_cmp, cache_sel, cache_win, cache_conv, s5_re, s5_im, page_table):
    bs, dec_seq, d = x.shape
    assert dec_seq == 1
    n_pages = page_table.shape[1]
    q_pos = n_pages * PAGE_SIZE
    n_blocks = n_pages * PAGE_CHUNKS
    n_sel = q_pos // SEL_LEN + 1
    assert cache_win.shape[2] == min(WINDOW, q_pos) and n_sel >= SEL_TOP
    tok_tables = _rope_tables(jnp.full((bs,), q_pos))
    cmp_tables = _rope_tables(jnp.arange(n_blocks) * CMP_STRIDE + (CMP_LEN - 1))
    cmap = _cmp_to_sel_map(n_blocks, n_blocks, n_sel)
    kv_shape = (2, NSA_KV_HEADS, NSA_HEAD_DIM)
    states = []
    x2 = x.reshape(bs, d)
    for l, lw in enumerate(layers):
        x2 = ffn_ln(x2, *lw['ffn1'], *lw['ln'][0])
        u, qp, cmp_new, sel_new, win_new, gates, yc, conv_state = in_proj(
            x2[None], lw['w_proj'], tok_tables, lw['conv_w'], cache_conv[l].reshape(bs, (CONV_K - 1) * CONV_CH),
            decode=True)
        ya, s_re, s_im = s5_branch(u, s5_re[l].reshape(bs, S5_FLAT), s5_im[l].reshape(bs, S5_FLAT), lw['s5'], nb=bs,
                                   wide=False)
        ha, hb = paged_hidden(cache_cmp, page_table, lw['cmp'], layer=l)
        hpe = hpes[l]
        q8 = qp.reshape(bs, NSA_HEADS, LANES)
        per_seq = lambda a: a.reshape(bs, 1, ROW_WIDTH)
        oc, idx, kpos = decode_cmp(ha, hb, hpe, per_seq(cmp_new), q8, lw['cmp'], cmp_tables, cmap,
                                   q_pos=q_pos, n_sel=n_sel)
        g3 = gates[0, :, 0:3 * NSA_HEADS].reshape(bs, NSA_HEADS, 3)
        g3 = jnp.concatenate([g3, jnp.zeros((bs, NSA_HEADS, LANES - 3), F32)], axis=-1)
        yb, win_state = decode_attn(idx, page_table, cache_sel, cache_win, per_seq(sel_new), per_seq(win_new), q8, g3,
                                    oc, kpos, layer=l, q_pos=q_pos)
        x3 = merge_ln(x2[None], ya, yb.reshape(1, bs, QP_WIDTH), yc, lw['merge'], *lw['ln'][1])
        x2 = ffn_ln(x3[0], *lw['ffn2'], *lw['ln'][2])
        states.append((cmp_new.reshape((bs, 1) + kv_shape), sel_new.reshape((bs, 1) + kv_shape),
                       win_state.reshape((bs,) + kv_shape + (-1,)).transpose(0, 4, 1, 2, 3),
                       conv_state.reshape(bs, CONV_K - 1, CONV_CH),
                       s_re.reshape(bs, S5_GROUPS, S5_STATE), s_im.reshape(bs, S5_GROUPS, S5_STATE)))
    return x2.reshape(bs, 1, d), [jnp.stack(s, axis=0) for s in zip(*states)]


def kernel(x_prompt, x_sample, cache_cmp_kv, cache_sel_kv, cache_win_kv, cache_conv, state_s5_re, state_s5_im, page_table, ln_g, ln_b, ffn1_w_gu, ffn1_w_down, ffn2_w_gu, ffn2_w_down, w_in, s5_lambda_re, s5_lambda_im, s5_log_dt, s5_b_re, s5_b_im, s5_c_re, s5_c_im, s5_d, s5_w_glu, s5_b_glu, nsa_pe_k, nsa_pe_v, nsa_phi_k1, nsa_phi_k2, nsa_phi_v1, nsa_phi_v2, conv_w, w_proj_a, w_proj_b, w_proj_c, w_o):
    w = dict(ln_g=ln_g, ln_b=ln_b, ffn1_w_gu=ffn1_w_gu, ffn1_w_down=ffn1_w_down, ffn2_w_gu=ffn2_w_gu,
             ffn2_w_down=ffn2_w_down, w_in=w_in, s5_lambda_re=s5_lambda_re, s5_lambda_im=s5_lambda_im,
             s5_log_dt=s5_log_dt, s5_b_re=s5_b_re, s5_b_im=s5_b_im, s5_c_re=s5_c_re, s5_c_im=s5_c_im, s5_d=s5_d,
             s5_w_glu=s5_w_glu, s5_b_glu=s5_b_glu, nsa_pe_k=nsa_pe_k, nsa_pe_v=nsa_pe_v, nsa_phi_k1=nsa_phi_k1,
             nsa_phi_k2=nsa_phi_k2, nsa_phi_v1=nsa_phi_v1, nsa_phi_v2=nsa_phi_v2, conv_w=conv_w,
             w_proj_a=w_proj_a, w_proj_b=w_proj_b, w_proj_c=w_proj_c, w_o=w_o)
    layers = [_layer_weights(w, l) for l in range(DEPTH)]
    y_prompt, (p_cmp, p_sel, p_win, p_conv, p_re, p_im), hpes = _prompt_trunk(x_prompt, layers)
    y_sample, (s_cmp, s_sel, s_win, s_conv, s_re, s_im) = _sample_trunk(
        x_sample, layers, hpes, cache_cmp_kv, cache_sel_kv, cache_win_kv, cache_conv, state_s5_re, state_s5_im, page_table)
    return (y_prompt, y_sample, p_cmp, s_cmp, p_sel, s_sel, p_win, s_win, p_conv, s_conv, p_re, s_re, p_im, s_im)
```
